```python
import jax, jax.numpy as jnp
from jax import lax
import numpy as np

D_MODEL = 1024
BATCH = 4
SEQ = 4096
DEPTH = 1
DEC_BATCH = 16
DEC_SEQ = 32
PAST_LEN = 1024

CHUNK = 64
D_A = D_MODEL
D_B = D_MODEL
CONV_A = 3
CONV_B = 31
N_EXPERTS = 32
TOP_K = 4
D_FF = D_MODEL
SWIGLU_LIMIT = 7.0
SWIGLU_ALPHA = 1.702
EPS = 1e-5
MOE_BLOCK = 128
D_IN = 3 * D_A + 2 * D_B + 2 * D_MODEL
SPLITS = (D_A, 2 * D_A, 3 * D_A, 3 * D_A + D_B, 3 * D_A + 2 * D_B, 3 * D_A + 2 * D_B + D_MODEL)

kernel_name = "hybrid_conv_moe_stream_step"


def rmsnorm(x, g):
    xf = x.astype(jnp.float32)
    y = xf * lax.rsqrt(jnp.mean(xf * xf, axis=-1, keepdims=True) + EPS)
    return (y * g.astype(jnp.float32)).astype(x.dtype)


def layernorm(x, g, b):
    xf = x.astype(jnp.float32)
    mu = jnp.mean(xf, axis=-1, keepdims=True)
    xc = xf - mu
    var = jnp.mean(xc * xc, axis=-1, keepdims=True)
    y = xc * lax.rsqrt(var + EPS) * g.astype(jnp.float32) + b.astype(jnp.float32)
    return y.astype(x.dtype)


def causal_depthwise_conv(u, state, w):
    width = w.shape[0]
    full = jnp.concatenate([state.astype(u.dtype), u], axis=1)
    y = lax.conv_general_dilated(
        full, w[:, None, :].astype(u.dtype), window_strides=(1,), padding="VALID",
        dimension_numbers=("NWC", "WIO", "NWC"), feature_group_count=u.shape[-1])
    return y, full[:, full.shape[1] - (width - 1):]


def moe(x, w_router, b_router, w_gate_up, b_gate_up, w_down, b_down):
    bsz, seq, d = x.shape
    t = bsz * seq
    xf = x.reshape(t, d)
    logits = (xf @ w_router + b_router).astype(jnp.float32)
    top_val, top_idx = lax.top_k(logits, TOP_K)
    gate_w = jax.nn.softmax(top_val, axis=-1)
    m = t * TOP_K
    e_flat = top_idx.reshape(m).astype(jnp.int32)
    tok_flat = jnp.arange(m, dtype=jnp.int32) // TOP_K
    w_flat = gate_w.reshape(m)
    order = jnp.argsort(e_flat)
    e_s, tok_s, w_s = e_flat[order], tok_flat[order], w_flat[order]
    counts = jnp.zeros((N_EXPERTS,), jnp.int32).at[e_flat].add(1)
    padded = (counts + MOE_BLOCK - 1) // MOE_BLOCK * MOE_BLOCK
    start = jnp.cumsum(counts) - counts
    pend = jnp.cumsum(padded)
    pstart = pend - padded
    dest = pstart[e_s] + (jnp.arange(m, dtype=jnp.int32) - start[e_s])
    n_blocks = -(-m // MOE_BLOCK) + N_EXPERTS
    p = n_blocks * MOE_BLOCK
    x_buf = jnp.zeros((p, d), x.dtype).at[dest].set(xf[tok_s])
    tok_buf = jnp.full((p,), t, jnp.int32).at[dest].set(tok_s)
    w_buf = jnp.zeros((p,), jnp.float32).at[dest].set(w_s)
    block_start = jnp.arange(n_blocks, dtype=jnp.int32) * MOE_BLOCK
    block_e = jnp.minimum(jnp.searchsorted(pend, block_start, side="right"), N_EXPERTS - 1)

    def expert_block(args):
        xb, e = args
        gu = xb @ w_gate_up[e] + b_gate_up[e]
        gate = jnp.minimum(gu[:, :D_FF], SWIGLU_LIMIT)
        up = jnp.clip(gu[:, D_FF:], -SWIGLU_LIMIT, SWIGLU_LIMIT)
        h = (up + 1.0) * gate * jax.nn.sigmoid(SWIGLU_ALPHA * gate)
        return h @ w_down[e] + b_down[e]

    out = lax.map(expert_block, (x_buf.reshape(n_blocks, MOE_BLOCK, d), block_e)).reshape(p, d)
    y = jnp.zeros((t + 1, d), jnp.float32).at[tok_buf].add(out.astype(jnp.float32) * w_buf[:, None])[:t]
    return y.astype(x.dtype).reshape(bsz, seq, d)


def layer(x, st_a, st_b, g_mix, w_in, b_in, w_conv_a, w_out_a, w_conv_b, b_conv_b, ln_g, ln_b,
          w_out_b, b_out_b, w_o, g_ffn, w_router, b_router, w_gate_up, b_gate_up, w_down, b_down):
    n = rmsnorm(x, g_mix)
    proj = n @ w_in + b_in
    g_b, g_c, u, glu_a, glu_b, r_a, r_b = jnp.split(proj, SPLITS, axis=-1)
    ya, new_a = causal_depthwise_conv(g_c * u, st_a, w_conv_a)
    out_a = (g_b * ya) @ w_out_a
    v = glu_a * jax.nn.sigmoid(glu_b)
    yb, new_b = causal_depthwise_conv(v, st_b, w_conv_b)
    yb = jax.nn.silu(layernorm(yb + b_conv_b, ln_g, ln_b))
    out_b = yb @ w_out_b + b_out_b
    mix = jax.nn.sigmoid(r_a) * out_a + jax.nn.sigmoid(r_b) * out_b
    h = x + mix @ w_o
    h = h + moe(rmsnorm(h, g_ffn), w_router, b_router, w_gate_up, b_gate_up, w_down, b_down)
    return h, new_a, new_b


def setup_inputs(seed: int = 0) -> dict:
    key = jax.random.key(seed)
    ks = jax.random.split(key, 24)
    f32 = jnp.float32

    def nrm(k, shape, scale):
        return jax.random.normal(k, shape, f32) * scale

    L = DEPTH
    return {
        "x_prompt": nrm(ks[0], (BATCH, SEQ, D_MODEL), 1.0),
        "x_sample": nrm(ks[1], (DEC_BATCH, DEC_SEQ, D_MODEL), 1.0),
        "state_conv_a": nrm(ks[2], (L, DEC_BATCH, CONV_A - 1, D_A), 1.0),
        "state_conv_b": nrm(ks[3], (L, DEC_BATCH, CONV_B - 1, D_B), 1.0),
        "g_mix": 1.0 + nrm(ks[4], (L, D_MODEL), 0.02),
        "w_in": nrm(ks[5], (L, D_MODEL, D_IN), D_MODEL ** -0.5),
        "b_in": nrm(ks[6], (L, D_IN), 0.02),
        "w_conv_a": nrm(ks[7], (L, CONV_A, D_A), CONV_A ** -0.5),
        "w_out_a": nrm(ks[8], (L, D_A, D_MODEL), D_A ** -0.5),
        "w_conv_b": nrm(ks[9], (L, CONV_B, D_B), CONV_B ** -0.5),
        "b_conv_b": nrm(ks[10], (L, D_B), 0.02),
        "ln_g": 1.0 + nrm(ks[11], (L, D_B), 0.02),
        "ln_b": nrm(ks[12], (L, D_B), 0.02),
        "w_out_b": nrm(ks[13], (L, D_B, D_MODEL), D_B ** -0.5),
        "b_out_b": nrm(ks[14], (L, D_MODEL), 0.02),
        "w_o": nrm(ks[15], (L, D_MODEL, D_MODEL), D_MODEL ** -0.5),
        "g_ffn": 1.0 + nrm(ks[16], (L, D_MODEL), 0.02),
        "w_router": nrm(ks[17], (L, D_MODEL, N_EXPERTS), D_MODEL ** -0.5),
        "b_router": nrm(ks[18], (L, N_EXPERTS), 0.01),
        "w_gate_up": nrm(ks[19], (L, N_EXPERTS, D_MODEL, 2 * D_FF), D_MODEL ** -0.5),
        "b_gate_up": nrm(ks[20], (L, N_EXPERTS, 2 * D_FF), 0.02),
        "w_down": nrm(ks[21], (L, N_EXPERTS, D_FF, D_MODEL), D_FF ** -0.5),
        "b_down": nrm(ks[22], (L, N_EXPERTS, D_MODEL), 0.02),
        "g_final": 1.0 + nrm(ks[23], (D_MODEL,), 0.02),
    }


def reference(x_prompt, x_sample, state_conv_a, state_conv_b, g_mix, w_in, b_in, w_conv_a, w_out_a,
              w_conv_b, b_conv_b, ln_g, ln_b, w_out_b, b_out_b, w_o, g_ffn, w_router, b_router,
              w_gate_up, b_gate_up, w_down, b_down, g_final):
    hp, hs = x_prompt, x_sample
    bp = x_prompt.shape[0]
    na_p, nb_p, na_s, nb_s = [], [], [], []
    for l in range(DEPTH):
        params = (g_mix[l], w_in[l], b_in[l], w_conv_a[l], w_out_a[l], w_conv_b[l], b_conv_b[l],
                  ln_g[l], ln_b[l], w_out_b[l], b_out_b[l], w_o[l], g_ffn[l], w_router[l],
                  b_router[l], w_gate_up[l], b_gate_up[l], w_down[l], b_down[l])
        zero_a = jnp.zeros((bp, CONV_A - 1, D_A), x_prompt.dtype)
        zero_b = jnp.zeros((bp, CONV_B - 1, D_B), x_prompt.dtype)
        hp, ap, bpst = layer(hp, zero_a, zero_b, *params)
        hs, as_, bs = layer(hs, state_conv_a[l], state_conv_b[l], *params)
        na_p.append(ap)
        nb_p.append(bpst)
        na_s.append(as_)
        nb_s.append(bs)
    y_prompt = rmsnorm(hp, g_final)
    y_sample = rmsnorm(hs, g_final)
    new_conv_a_prompt = jnp.stack(na_p)
    new_conv_b_prompt = jnp.stack(nb_p)
    new_conv_a_sample = jnp.stack(na_s)
    new_conv_b_sample = jnp.stack(nb_s)
    return (y_prompt, y_sample, new_conv_a_prompt, new_conv_b_prompt, new_conv_a_sample, new_conv_b_sample)
```

```python
import functools

import jax
import jax.numpy as jnp
from jax import lax
from jax.experimental import pallas as pl
from jax.experimental.pallas import tpu as pltpu

F32 = jnp.float32
BF16 = jnp.bfloat16
I32 = jnp.int32

D = 1024
LANES = 128
SUBLANES = 8
NCHUNK = D // LANES
N_EXPERTS = 32
TOP_K = 4
TAPS_A = 3
TAPS_B = 31
EPS = 1e-5
SWIGLU_LIMIT = 7.0
SWIGLU_ALPHA = 1.702

TL = 256
HIST = 32
HIST_A = 8
CONV_UNROLL = 4
ROWS = 256
ROUTE_CHUNK = 512
VMEM_LIMIT = 56 * 1024 * 1024


def _rows_to_slab(dst_ref, slot0, val):
    n = val.shape[0]
    for c in range(NCHUNK):
        dst_ref[pl.ds(SUBLANES * slot0 + c, n, stride=SUBLANES), :] = val[:, c * LANES:(c + 1) * LANES]


def _slab_to_rows(src_ref, slot0, n):
    return jnp.concatenate(
        [src_ref[pl.ds(SUBLANES * slot0 + c, n, stride=SUBLANES), :] for c in range(NCHUNK)], axis=1)


def _conv_slab(src_ref, w_ref, ntaps, dst_ref, n_out, seg_len, seg_stride, first_tap_slot):
    w = [w_ref[j] for j in range(ntaps)]
    seg_shift = seg_len.bit_length() - 1
    assert (1 << seg_shift) == seg_len

    def body(q, carry):
        for u in range(CONV_UNROLL):
            o = q * CONV_UNROLL + u
            seg = o >> seg_shift
            step = o & (seg_len - 1)
            slot = seg * seg_stride + first_tap_slot + step
            base = pl.multiple_of(slot * SUBLANES, SUBLANES)
            acc = w[0] * src_ref[pl.ds(base, SUBLANES), :]
            for j in range(1, ntaps):
                acc = acc + w[j] * src_ref[pl.ds(base + SUBLANES * j, SUBLANES), :]
            dst_ref[pl.ds(pl.multiple_of(o * SUBLANES, SUBLANES), SUBLANES), :] = acc
        return carry

    lax.fori_loop(0, n_out // CONV_UNROLL, body, 0)


def _rmsnorm(x, g):
    return x * lax.rsqrt(jnp.mean(x * x, axis=-1, keepdims=True) + EPS) * g


def _mixer_kernel(n_prompt_tiles, tiles_per_seq, seq_per_tile, seq_len_s,
                  xp_ref, xs_ref, sta_ref, stb_ref, gmix_ref, win_ref, bin_ref, wca_ref, woa_ref,
                  wcb_ref, bcb_ref, lng_ref, lnb_ref, wob_ref, bob_ref, wo_ref, gffn_ref, wrt_ref, brt_ref,
                  h_ref, hn_ref, idx_ref, gw_ref, nap_ref, nbp_ref, nas_ref, nbs_ref,
                  cu_scr, v_scr, cuslab, vslab, yaslab, ybslab):
    i = pl.program_id(0)
    is_prompt = i < n_prompt_tiles

    x = jnp.where(is_prompt, xp_ref[...], xs_ref[...])
    n_bf = _rmsnorm(x, gmix_ref[...]).astype(BF16)

    def proj(g):
        cols = slice(g * D, (g + 1) * D)
        return jnp.dot(n_bf, win_ref[:, cols], preferred_element_type=F32) + bin_ref[:, cols]

    cu_scr[...] = proj(1) * proj(2)
    v_scr[...] = proj(3) * jax.nn.sigmoid(proj(4))

    @pl.when(is_prompt)
    def _():
        j = lax.rem(i, tiles_per_seq)

        @pl.when(j == 0)
        def _():
            cuslab[0:HIST_A * SUBLANES, :] = jnp.zeros((HIST_A * SUBLANES, LANES), F32)
            vslab[0:HIST * SUBLANES, :] = jnp.zeros((HIST * SUBLANES, LANES), F32)

        _rows_to_slab(cuslab, HIST_A, cu_scr[...])
        _rows_to_slab(vslab, HIST, v_scr[...])
        _conv_slab(cuslab, wca_ref, TAPS_A, yaslab, TL, TL, 0, HIST_A - (TAPS_A - 1))
        _conv_slab(vslab, wcb_ref, TAPS_B, ybslab, TL, TL, 0, HIST - (TAPS_B - 1))
        cuslab[0:HIST_A * SUBLANES, :] = cuslab[TL * SUBLANES:(TL + HIST_A) * SUBLANES, :]
        vslab[0:HIST * SUBLANES, :] = vslab[TL * SUBLANES:(TL + HIST) * SUBLANES, :]
        nap_ref[0] = cu_scr[TL - (TAPS_A - 1):TL, :]
        nbp_ref[0] = v_scr[TL - (TAPS_B - 1):TL, :]

    @pl.when(jnp.logical_not(is_prompt))
    def _():
        seg_a = HIST_A + seq_len_s
        seg_b = HIST + seq_len_s
        for s in range(seq_per_tile):
            r0 = s * seq_len_s
            for c in range(NCHUNK):
                cols = slice(c * LANES, (c + 1) * LANES)
                cuslab[pl.ds(SUBLANES * (seg_a * s + HIST_A - (TAPS_A - 1)) + c, TAPS_A - 1, stride=SUBLANES), :] = (
                    sta_ref[s, :, cols])
                vslab[pl.ds(SUBLANES * (seg_b * s + HIST - (TAPS_B - 1)) + c, TAPS_B - 1, stride=SUBLANES), :] = (
                    stb_ref[s, :, cols])
            _rows_to_slab(cuslab, seg_a * s + HIST_A, cu_scr[r0:r0 + seq_len_s, :])
            _rows_to_slab(vslab, seg_b * s + HIST, v_scr[r0:r0 + seq_len_s, :])
            nas_ref[s] = cu_scr[r0 + seq_len_s - (TAPS_A - 1):r0 + seq_len_s, :]
            nbs_ref[s] = v_scr[r0 + seq_len_s - (TAPS_B - 1):r0 + seq_len_s, :]
        _conv_slab(cuslab, wca_ref, TAPS_A, yaslab, TL, seq_len_s, seg_a, HIST_A - (TAPS_A - 1))
        _conv_slab(vslab, wcb_ref, TAPS_B, ybslab, TL, seq_len_s, seg_b, HIST - (TAPS_B - 1))

    ya = _slab_to_rows(yaslab, 0, TL)
    yb = _slab_to_rows(ybslab, 0, TL) + bcb_ref[...]

    out_a = jnp.dot((proj(0) * ya).astype(BF16), woa_ref[...], preferred_element_type=F32)
    mu = jnp.mean(yb, axis=-1, keepdims=True)
    yc = yb - mu
    var = jnp.mean(yc * yc, axis=-1, keepdims=True)
    ln = yc * lax.rsqrt(var + EPS) * lng_ref[...] + lnb_ref[...]
    act = ln * jax.nn.sigmoid(ln)
    out_b = jnp.dot(act.astype(BF16), wob_ref[...], preferred_element_type=F32) + bob_ref[...]
    mix = jax.nn.sigmoid(proj(5)) * out_a + jax.nn.sigmoid(proj(6)) * out_b
    h = x + jnp.dot(mix.astype(BF16), wo_ref[...], preferred_element_type=F32)
    h_ref[...] = h
    hn = _rmsnorm(h, gffn_ref[...])
    _rows_to_slab(hn_ref, 0, hn)

    logits = lax.dot_general(wrt_ref[...], hn, (((1,), (1,)), ((), ())),
                             precision=lax.Precision.HIGHEST, preferred_element_type=F32) + brt_ref[...]
    iota_e = lax.broadcasted_iota(I32, (N_EXPERTS, TL), 0).astype(F32)
    vals, idxs = [], []
    for _ in range(TOP_K):
        m = jnp.max(logits, axis=0, keepdims=True)
        sel = jnp.min(jnp.where(logits == m, iota_e, float(N_EXPERTS)), axis=0, keepdims=True)
        vals.append(m)
        idxs.append(sel)
        logits = jnp.where(iota_e == sel, -jnp.inf, logits)
    ex = [jnp.exp(v - vals[0]) for v in vals]
    denom = ex[0] + ex[1] + ex[2] + ex[3]
    for k in range(TOP_K):
        idx_ref[k:k + 1, :] = idxs[k].astype(I32)
        gw_ref[k:k + 1, :] = ex[k] / denom


def _route_kernel(n_tokens, n_blocks_pad, idx_ref, dest_ref, be_ref, nu_ref, tri_scr):
    n_chunks = n_tokens // ROUTE_CHUNK
    iota_e = lax.broadcasted_iota(I32, (N_EXPERTS, ROUTE_CHUNK), 0)

    def masks(c):
        sl = pl.ds(pl.multiple_of(c * ROUTE_CHUNK, ROUTE_CHUNK), ROUTE_CHUNK)
        return sl, [idx_ref[k:k + 1, sl] == iota_e for k in range(TOP_K)]

    def onehot_sum(ms):
        tot = ms[0].astype(F32)
        for k in range(1, TOP_K):
            tot = tot + ms[k].astype(F32)
        return tot

    def count_body(c, acc):
        _, ms = masks(c)
        return acc + onehot_sum(ms)

    acc = lax.fori_loop(0, n_chunks, count_body, jnp.zeros((N_EXPERTS, ROUTE_CHUNK), F32))
    counts = jnp.sum(acc, axis=1, keepdims=True)
    nblk = jnp.floor((counts + (ROWS - 1)) * (1.0 / ROWS))
    r = lax.broadcasted_iota(I32, (N_EXPERTS, N_EXPERTS), 0)
    cidx = lax.broadcasted_iota(I32, (N_EXPERTS, N_EXPERTS), 1)
    lower = (cidx <= r).astype(F32)
    pend = jnp.dot(lower, jnp.broadcast_to(nblk, (N_EXPERTS, LANES)),
                   precision=lax.Precision.HIGHEST, preferred_element_type=F32)
    pend1 = pend[:, 0:1]
    pstart_rows = (pend1 - nblk) * ROWS

    bvec = lax.broadcasted_iota(I32, (N_EXPERTS, n_blocks_pad), 1).astype(F32)
    be = jnp.sum((pend1 <= bvec).astype(F32), axis=0, keepdims=True)
    be_ref[...] = jnp.minimum(be, N_EXPERTS - 1).astype(I32)
    last = lax.broadcasted_iota(I32, (N_EXPERTS, LANES), 0) == N_EXPERTS - 1
    nu_ref[...] = jnp.sum(jnp.where(last, pend, 0.0), axis=0, keepdims=True).astype(I32)

    tr = lax.broadcasted_iota(I32, (ROUTE_CHUNK, ROUTE_CHUNK), 0)
    tc = lax.broadcasted_iota(I32, (ROUTE_CHUNK, ROUTE_CHUNK), 1)
    tri_scr[...] = (tr < tc).astype(BF16)

    def dest_body(c, carry):
        sl, ms = masks(c)
        tot = onehot_sum(ms)
        before = jnp.dot(tot.astype(BF16), tri_scr[...], preferred_element_type=F32)
        pos = pstart_rows + carry + before
        for k in range(TOP_K):
            dest_ref[k:k + 1, sl] = jnp.sum(jnp.where(ms[k], pos, 0.0), axis=0, keepdims=True).astype(I32)
        return carry + jnp.sum(tot, axis=1, keepdims=True)

    lax.fori_loop(0, n_chunks, dest_body, jnp.zeros((N_EXPERTS, 1), F32))


def _dispatch_kernel(dest_ref, hn_hbm, xz_hbm, xb_hbm, sem):
    del xz_hbm
    i = pl.program_id(0)

    def row_copy(tt, k):
        src = pl.multiple_of((i * TL + tt) * SUBLANES, SUBLANES)
        dst = pl.multiple_of(dest_ref[0, 0, k * TL + tt] * SUBLANES, SUBLANES)
        return pltpu.make_async_copy(hn_hbm.at[pl.ds(src, SUBLANES), :], xb_hbm.at[pl.ds(dst, SUBLANES), :], sem)

    def body(tt, carry):
        for k in range(TOP_K):
            row_copy(tt, k).start()
        return carry

    lax.fori_loop(0, TL, body, 0)
    n = TOP_K * TL * SUBLANES
    pltpu.make_async_copy(hn_hbm.at[pl.ds(0, n), :], xb_hbm.at[pl.ds(0, n), :], sem).wait()


def _expert_kernel(be_ref, nu_ref, xs_ref, wgu_ref, bgu_ref, wd_ref, bd_ref, o_ref, wgu_bf, wd_bf):
    b = pl.program_id(0)
    prev = be_ref[jnp.maximum(b - 1, 0)]
    active = b < nu_ref[0]

    @pl.when(active & ((b == 0) | (be_ref[b] != prev)))
    def _():
        wgu_bf[...] = wgu_ref[0].astype(BF16)
        wd_bf[...] = wd_ref[0].astype(BF16)

    @pl.when(active)
    def _():
        xb = _slab_to_rows(xs_ref, 0, ROWS).astype(BF16)
        gu = jnp.dot(xb, wgu_bf[...], preferred_element_type=F32) + bgu_ref[0]
        gate = jnp.minimum(gu[:, :D], SWIGLU_LIMIT)
        up = jnp.clip(gu[:, D:], -SWIGLU_LIMIT, SWIGLU_LIMIT)
        hidden = (up + 1.0) * gate * jax.nn.sigmoid(SWIGLU_ALPHA * gate)
        out = jnp.dot(hidden.astype(BF16), wd_bf[...], preferred_element_type=F32) + bd_ref[0]
        _rows_to_slab(o_ref, 0, out)

    @pl.when(jnp.logical_not(active))
    def _():
        o_ref[...] = jnp.zeros(o_ref.shape, F32)


def _combine_kernel(n_prompt_tiles, dest_ref, h_ref, gw_ref, gfin_ref, ob_hbm, yp_ref, ys_ref, gath, sem):
    def row_copy(tt, k):
        src = pl.multiple_of(dest_ref[0, 0, k * TL + tt] * SUBLANES, SUBLANES)
        dst = pl.multiple_of((k * TL + tt) * SUBLANES, SUBLANES)
        return pltpu.make_async_copy(ob_hbm.at[pl.ds(src, SUBLANES), :], gath.at[pl.ds(dst, SUBLANES), :], sem)

    def body(tt, carry):
        for k in range(TOP_K):
            row_copy(tt, k).start()
        return carry

    lax.fori_loop(0, TL, body, 0)
    n = TOP_K * TL * SUBLANES
    pltpu.make_async_copy(ob_hbm.at[pl.ds(0, n), :], gath, sem).wait()

    acc = h_ref[...]
    gw = gw_ref[...]
    for k in range(TOP_K):
        acc = acc + gw[:, k:k + 1] * _slab_to_rows(gath, k * TL, TL)
    y = _rmsnorm(acc, gfin_ref[...])
    is_prompt = pl.program_id(0) < n_prompt_tiles

    @pl.when(is_prompt)
    def _():
        yp_ref[...] = y

    @pl.when(jnp.logical_not(is_prompt))
    def _():
        ys_ref[...] = y


def _const_spec(shape):
    nd = len(shape)
    return pl.BlockSpec(shape, lambda *_: (0,) * nd, pipeline_mode=pl.Buffered(1))


def kernel(x_prompt, x_sample, state_conv_a, state_conv_b, g_mix, w_in, b_in, w_conv_a, w_out_a, w_conv_b, b_conv_b, ln_g, ln_b, w_out_b, b_out_b, w_o, g_ffn, w_router, b_router, w_gate_up, b_gate_up, w_down, b_down, g_final):
    depth = g_mix.shape[0]
    assert depth == 1
    bp, lp, d = x_prompt.shape
    bs, ls, _ = x_sample.shape
    assert d == D and w_in.shape[2] == 7 * D
    tp, ts = bp * lp, bs * ls
    t = tp + ts
    assert lp % TL == 0 and TL % ls == 0 and ts % TL == 0 and t % ROUTE_CHUNK == 0
    npt, nst = tp // TL, ts // TL
    nt = npt + nst
    tiles_per_seq = lp // TL
    seq_per_tile = TL // ls
    n_blocks = (t * TOP_K) // ROWS + N_EXPERTS
    n_blocks_pad = -(-n_blocks // LANES) * LANES
    p_rows = n_blocks * ROWS

    row = lambda a: a.reshape(1, -1)
    f32_spec = lambda: _const_spec((1, D))

    mixer = pl.pallas_call(
        functools.partial(_mixer_kernel, npt, tiles_per_seq, seq_per_tile, ls),
        grid=(nt,),
        in_specs=[
            pl.BlockSpec((TL, D), lambda i: (jnp.minimum(i, npt - 1), 0)),
            pl.BlockSpec((TL, D), lambda i: (jnp.maximum(i - npt, 0), 0)),
            pl.BlockSpec((seq_per_tile, TAPS_A - 1, D), lambda i: (jnp.maximum(i - npt, 0), 0, 0)),
            pl.BlockSpec((seq_per_tile, TAPS_B - 1, D), lambda i: (jnp.maximum(i - npt, 0), 0, 0)),
            f32_spec(),
            _const_spec((D, 7 * D)),
            _const_spec((1, 7 * D)),
            _const_spec((TAPS_A, SUBLANES, LANES)),
            _const_spec((D, D)),
            _const_spec((TAPS_B, SUBLANES, LANES)),
            f32_spec(), f32_spec(), f32_spec(),
            _const_spec((D, D)),
            f32_spec(),
            _const_spec((D, D)),
            f32_spec(),
            _const_spec((N_EXPERTS, D)),
            _const_spec((N_EXPERTS, 1)),
        ],
        out_specs=[
            pl.BlockSpec((TL, D), lambda i: (i, 0)),
            pl.BlockSpec((TL * SUBLANES, LANES), lambda i: (i, 0)),
            pl.BlockSpec((TOP_K, TL), lambda i: (0, i)),
            pl.BlockSpec((TOP_K, TL), lambda i: (0, i)),
            pl.BlockSpec((1, TAPS_A - 1, D), lambda i: (jnp.minimum(i // tiles_per_seq, bp - 1), 0, 0)),
            pl.BlockSpec((1, TAPS_B - 1, D), lambda i: (jnp.minimum(i // tiles_per_seq, bp - 1), 0, 0)),
            pl.BlockSpec((seq_per_tile, TAPS_A - 1, D), lambda i: (jnp.maximum(i - npt, 0), 0, 0)),
            pl.BlockSpec((seq_per_tile, TAPS_B - 1, D), lambda i: (jnp.maximum(i - npt, 0), 0, 0)),
        ],
        out_shape=[
            jax.ShapeDtypeStruct((t, D), F32),
            jax.ShapeDtypeStruct((t * SUBLANES, LANES), F32),
            jax.ShapeDtypeStruct((TOP_K, t), I32),
            jax.ShapeDtypeStruct((TOP_K, t), F32),
            jax.ShapeDtypeStruct((bp, TAPS_A - 1, D), F32),
            jax.ShapeDtypeStruct((bp, TAPS_B - 1, D), F32),
            jax.ShapeDtypeStruct((bs, TAPS_A - 1, D), F32),
            jax.ShapeDtypeStruct((bs, TAPS_B - 1, D), F32),
        ],
        scratch_shapes=[
            pltpu.VMEM((TL, D), F32),
            pltpu.VMEM((TL, D), F32),
            pltpu.VMEM((max(HIST_A + TL, seq_per_tile * (HIST_A + ls)) * SUBLANES, LANES), F32),
            pltpu.VMEM((max(HIST + TL, seq_per_tile * (HIST + ls)) * SUBLANES, LANES), F32),
            pltpu.VMEM((TL * SUBLANES, LANES), F32),
            pltpu.VMEM((TL * SUBLANES, LANES), F32),
        ],
        compiler_params=pltpu.CompilerParams(dimension_semantics=("arbitrary",), vmem_limit_bytes=VMEM_LIMIT),
        name="mixer",
    )
    h, hn_slab, idx_t, gw_t, na_p, nb_p, na_s, nb_s = mixer(
        x_prompt.reshape(tp, D), x_sample.reshape(ts, D), state_conv_a[0], state_conv_b[0],
        row(g_mix), w_in[0].astype(BF16), row(b_in), w_conv_a[0].reshape(TAPS_A, SUBLANES, LANES),
        w_out_a[0].astype(BF16), w_conv_b[0].reshape(TAPS_B, SUBLANES, LANES), row(b_conv_b), row(ln_g), row(ln_b),
        w_out_b[0].astype(BF16), row(b_out_b), w_o[0].astype(BF16), row(g_ffn),
        w_router[0].T, b_router[0].reshape(N_EXPERTS, 1))

    dest_t, block_e, n_used = pl.pallas_call(
        functools.partial(_route_kernel, t, n_blocks_pad),
        out_shape=[
            jax.ShapeDtypeStruct((TOP_K, t), I32),
            jax.ShapeDtypeStruct((1, n_blocks_pad), I32),
            jax.ShapeDtypeStruct((1, LANES), I32),
        ],
        scratch_shapes=[pltpu.VMEM((ROUTE_CHUNK, ROUTE_CHUNK), BF16)],
        compiler_params=pltpu.CompilerParams(vmem_limit_bytes=VMEM_LIMIT),
        name="route",
    )(idx_t)
    dest_sm = dest_t.reshape(TOP_K, nt, TL).transpose(1, 0, 2).reshape(nt, 1, TOP_K * TL)
    block_e = block_e[0, :n_blocks]
    n_used = n_used[0, :1]

    smem_tile_spec = pl.BlockSpec((1, 1, TOP_K * TL), lambda i: (i, 0, 0), memory_space=pltpu.SMEM)
    any_spec = pl.BlockSpec(memory_space=pl.ANY)
    x_buf = pl.pallas_call(
        _dispatch_kernel,
        grid=(nt,),
        in_specs=[smem_tile_spec, any_spec, any_spec],
        out_specs=any_spec,
        out_shape=jax.ShapeDtypeStruct((p_rows * SUBLANES, LANES), F32),
        scratch_shapes=[pltpu.SemaphoreType.DMA],
        input_output_aliases={2: 0},
        compiler_params=pltpu.CompilerParams(dimension_semantics=("arbitrary",)),
        name="dispatch",
    )(dest_sm, hn_slab, jnp.zeros((p_rows * SUBLANES, LANES), F32))

    def blk(b, be, nu):
        return jnp.minimum(b, nu[0] - 1)

    out_buf = pl.pallas_call(
        _expert_kernel,
        grid_spec=pltpu.PrefetchScalarGridSpec(
            num_scalar_prefetch=2,
            grid=(n_blocks,),
            in_specs=[
                pl.BlockSpec((ROWS * SUBLANES, LANES), lambda b, be, nu: (blk(b, be, nu), 0)),
                pl.BlockSpec((1, D, 2 * D), lambda b, be, nu: (be[blk(b, be, nu)], 0, 0)),
                pl.BlockSpec((1, 1, 2 * D), lambda b, be, nu: (be[blk(b, be, nu)], 0, 0)),
                pl.BlockSpec((1, D, D), lambda b, be, nu: (be[blk(b, be, nu)], 0, 0)),
                pl.BlockSpec((1, 1, D), lambda b, be, nu: (be[blk(b, be, nu)], 0, 0)),
            ],
            out_specs=pl.BlockSpec((ROWS * SUBLANES, LANES), lambda b, be, nu: (b, 0)),
            scratch_shapes=[pltpu.VMEM((D, 2 * D), BF16), pltpu.VMEM((D, D), BF16)],
        ),
        out_shape=jax.ShapeDtypeStruct((p_rows * SUBLANES, LANES), F32),
        compiler_params=pltpu.CompilerParams(dimension_semantics=("arbitrary",), vmem_limit_bytes=VMEM_LIMIT),
        name="experts",
    )(block_e, n_used, x_buf, w_gate_up[0], b_gate_up[0].reshape(N_EXPERTS, 1, 2 * D), w_down[0],
      b_down[0].reshape(N_EXPERTS, 1, D))

    y_p, y_s = pl.pallas_call(
        functools.partial(_combine_kernel, npt),
        grid=(nt,),
        in_specs=[
            smem_tile_spec,
            pl.BlockSpec((TL, D), lambda i: (i, 0)),
            pl.BlockSpec((TL, TOP_K), lambda i: (i, 0)),
            pl.BlockSpec((1, D), lambda i: (0, 0)),
            any_spec,
        ],
        out_specs=[
            pl.BlockSpec((TL, D), lambda i: (jnp.minimum(i, npt - 1), 0)),
            pl.BlockSpec((TL, D), lambda i: (jnp.maximum(i - npt, 0), 0)),
        ],
        out_shape=[jax.ShapeDtypeStruct((tp, D), F32), jax.ShapeDtypeStruct((ts, D), F32)],
        scratch_shapes=[pltpu.VMEM((TOP_K * TL * SUBLANES, LANES), F32), pltpu.SemaphoreType.DMA],
        compiler_params=pltpu.CompilerParams(dimension_semantics=("arbitrary",), vmem_limit_bytes=VMEM_LIMIT),
        name="combine",
    )(dest_sm, h, gw_t.T, row(g_final), out_buf)

    return (y_p.reshape(bp, lp, D), y_s.reshape(bs, ls, D), na_p[None], nb_p[None], na_s[None], nb_s[None])
```

```python
import functools

import jax
import jax.numpy as jnp
from jax import lax
from jax.experimental import pallas as pl
from jax.experimental.pallas import tpu as pltpu

F32 = jnp.float32
BF16 = jnp.bfloat16
I32 = jnp.int32

D = 1024
LANES = 128
SUBLANES = 8
NCHUNK = D // LANES
N_EXPERTS = 32
TOP_K = 4
TAPS_A = 3
TAPS_B = 31
EPS = 1e-5
SWIGLU_LIMIT = 7.0
SWIGLU_ALPHA = 1.702

TL = 256
HIST = 32
HIST_A = 8
CONV_UNROLL = 4
ROWS = 256
ROUTE_CHUNK = 512
VMEM_LIMIT = 56 * 1024 * 1024


def _rows_to_slab(dst_ref, slot0, val):
    n = val.shape[0]
    for c in range(NCHUNK):
        dst_ref[pl.ds(SUBLANES * slot0 + c, n, stride=SUBLANES), :] = val[:, c * LANES:(c + 1) * LANES]


def _slab_to_rows(src_ref, slot0, n):
    return jnp.concatenate(
        [src_ref[pl.ds(SUBLANES * slot0 + c, n, stride=SUBLANES), :] for c in range(NCHUNK)], axis=1)


def _conv_slab(src_ref, w_ref, ntaps, dst_ref, n_out, seg_len, seg_stride, first_tap_slot):
    w = [w_ref[j] for j in range(ntaps)]
    seg_shift = seg_len.bit_length() - 1
    assert (1 << seg_shift) == seg_len

    def body(q, carry):
        for u in range(CONV_UNROLL):
            o = q * CONV_UNROLL + u
            seg = o >> seg_shift
            step = o & (seg_len - 1)
            slot = seg * seg_stride + first_tap_slot + step
            base = pl.multiple_of(slot * SUBLANES, SUBLANES)
            acc = w[0] * src_ref[pl.ds(base, SUBLANES), :]
            for j in range(1, ntaps):
                acc = acc + w[j] * src_ref[pl.ds(base + SUBLANES * j, SUBLANES), :]
            dst_ref[pl.ds(pl.multiple_of(o * SUBLANES, SUBLANES), SUBLANES), :] = acc
        return carry

    lax.fori_loop(0, n_out // CONV_UNROLL, body, 0)


def _rmsnorm(x, g):
    return x * lax.rsqrt(jnp.mean(x * x, axis=-1, keepdims=True) + EPS) * g


def _mixer_kernel(n_prompt_tiles, tiles_per_seq, seq_per_tile, seq_len_s,
                  xp_ref, xs_ref, sta_ref, stb_ref, gmix_ref, win_ref, bin_ref, wca_ref, woa_ref,
                  wcb_ref, bcb_ref, lng_ref, lnb_ref, wob_ref, bob_ref, wo_ref, gffn_ref, wrt_ref, brt_ref,
                  h_ref, hn_ref, idx_ref, gw_ref, nap_ref, nbp_ref, nas_ref, nbs_ref,
                  cu_scr, v_scr, cuslab, vslab, yaslab, ybslab):
    i = pl.program_id(0)
    is_prompt = i < n_prompt_tiles

    x = jnp.where(is_prompt, xp_ref[...], xs_ref[...])
    n_bf = _rmsnorm(x, gmix_ref[...]).astype(BF16)

    def proj(g):
        cols = slice(g * D, (g + 1) * D)
        return jnp.dot(n_bf, win_ref[:, cols], preferred_element_type=F32) + bin_ref[:, cols]

    cu_scr[...] = proj(1) * proj(2)
    v_scr[...] = proj(3) * jax.nn.sigmoid(proj(4))

    @pl.when(is_prompt)
    def _():
        j = lax.rem(i, tiles_per_seq)

        @pl.when(j == 0)
        def _():
            cuslab[0:HIST_A * SUBLANES, :] = jnp.zeros((HIST_A * SUBLANES, LANES), F32)
            vslab[0:HIST * SUBLANES, :] = jnp.zeros((HIST * SUBLANES, LANES), F32)

        _rows_to_slab(cuslab, HIST_A, cu_scr[...])
        _rows_to_slab(vslab, HIST, v_scr[...])
        _conv_slab(cuslab, wca_ref, TAPS_A, yaslab, TL, TL, 0, HIST_A - (TAPS_A - 1))
        _conv_slab(vslab, wcb_ref, TAPS_B, ybslab, TL, TL, 0, HIST - (TAPS_B - 1))
        cuslab[0:HIST_A * SUBLANES, :] = cuslab[TL * SUBLANES:(TL + HIST_A) * SUBLANES, :]
        vslab[0:HIST * SUBLANES, :] = vslab[TL * SUBLANES:(TL + HIST) * SUBLANES, :]
        nap_ref[0] = cu_scr[TL - (TAPS_A - 1):TL, :]
        nbp_ref[0] = v_scr[TL - (TAPS_B - 1):TL, :]

    @pl.when(jnp.logical_not(is_prompt))
    def _():
        seg_a = HIST_A + seq_len_s
        seg_b = HIST + seq_len_s
        for s in range(seq_per_tile):
            r0 = s * seq_len_s
            for c in range(NCHUNK):
                cols = slice(c * LANES, (c + 1) * LANES)
                cuslab[pl.ds(SUBLANES * (seg_a * s + HIST_A - (TAPS_A - 1)) + c, TAPS_A - 1, stride=SUBLANES), :] = (
                    sta_ref[s, :, cols])
                vslab[pl.ds(SUBLANES * (seg_b * s + HIST - (TAPS_B - 1)) + c, TAPS_B - 1, stride=SUBLANES), :] = (
                    stb_ref[s, :, cols])
            _rows_to_slab(cuslab, seg_a * s + HIST_A, cu_scr[r0:r0 + seq_len_s, :])
            _rows_to_slab(vslab, seg_b * s + HIST, v_scr[r0:r0 + seq_len_s, :])
            nas_ref[s] = cu_scr[r0 + seq_len_s - (TAPS_A - 1):r0 + seq_len_s, :]
            nbs_ref[s] = v_scr[r0 + seq_len_s - (TAPS_B - 1):r0 + seq_len_s, :]
        _conv_slab(cuslab, wca_ref, TAPS_A, yaslab, TL, seq_len_s, seg_a, HIST_A - (TAPS_A - 1))
        _conv_slab(vslab, wcb_ref, TAPS_B, ybslab, TL, seq_len_s, seg_b, HIST - (TAPS_B - 1))

    ya = _slab_to_rows(yaslab, 0, TL)
    yb = _slab_to_rows(ybslab, 0, TL) + bcb_ref[...]

    out_a = jnp.dot((proj(0) * ya).astype(BF16), woa_ref[...], preferred_element_type=F32)
    mu = jnp.mean(yb, axis=-1, keepdims=True)
    yc = yb - mu
    var = jnp.mean(yc * yc, axis=-1, keepdims=True)
    ln = yc * lax.rsqrt(var + EPS) * lng_ref[...] + lnb_ref[...]
    act = ln * jax.nn.sigmoid(ln)
    out_b = jnp.dot(act.astype(BF16), wob_ref[...], preferred_element_type=F32) + bob_ref[...]
    mix = jax.nn.sigmoid(proj(5)) * out_a + jax.nn.sigmoid(proj(6)) * out_b
    h = x + jnp.dot(mix.astype(BF16), wo_ref[...], preferred_element_type=F32)
    h_ref[...] = h
    hn = _rmsnorm(h, gffn_ref[...])
    _rows_to_slab(hn_ref, 0, hn)

    logits = lax.dot_general(wrt_ref[...], hn, (((1,), (1,)), ((), ())),
                             precision=lax.Precision.HIGHEST, preferred_element_type=F32) + brt_ref[...]
    iota_e = lax.broadcasted_iota(I32, (N_EXPERTS, TL), 0).astype(F32)
    vals, idxs = [], []
    for _ in range(TOP_K):
        m = jnp.max(logits, axis=0, keepdims=True)
        sel = jnp.min(jnp.where(logits == m, iota_e, float(N_EXPERTS)), axis=0, keepdims=True)
        vals.append(m)
        idxs.append(sel)
        logits = jnp.where(iota_e == sel, -jnp.inf, logits)
    ex = [jnp.exp(v - vals[0]) for v in vals]
    denom = ex[0] + ex[1] + ex[2] + ex[3]
    for k in range(TOP_K):
        idx_ref[k:k + 1, :] = idxs[k].astype(I32)
        gw_ref[k:k + 1, :] = ex[k] / denom


def _route_kernel(n_tokens, n_blocks_pad, idx_ref, dest_ref, be_ref, nu_ref, tri_scr):
    n_chunks = n_tokens // ROUTE_CHUNK
    iota_e = lax.broadcasted_iota(I32, (N_EXPERTS, ROUTE_CHUNK), 0)

    def masks(c):
        sl = pl.ds(pl.multiple_of(c * ROUTE_CHUNK, ROUTE_CHUNK), ROUTE_CHUNK)
        return sl, [idx_ref[k:k + 1, sl] == iota_e for k in range(TOP_K)]

    def onehot_sum(ms):
        tot = ms[0].astype(F32)
        for k in range(1, TOP_K):
            tot = tot + ms[k].astype(F32)
        return tot

    def count_body(c, acc):
        _, ms = masks(c)
        return acc + onehot_sum(ms)

    acc = lax.fori_loop(0, n_chunks, count_body, jnp.zeros((N_EXPERTS, ROUTE_CHUNK), F32))
    counts = jnp.sum(acc, axis=1, keepdims=True)
    nblk = jnp.floor((counts + (ROWS - 1)) * (1.0 / ROWS))
    r = lax.broadcasted_iota(I32, (N_EXPERTS, N_EXPERTS), 0)
    cidx = lax.broadcasted_iota(I32, (N_EXPERTS, N_EXPERTS), 1)
    lower = (cidx <= r).astype(F32)
    pend = jnp.dot(lower, jnp.broadcast_to(nblk, (N_EXPERTS, LANES)),
                   precision=lax.Precision.HIGHEST, preferred_element_type=F32)
    pend1 = pend[:, 0:1]
    pstart_rows = (pend1 - nblk) * ROWS

    bvec = lax.broadcasted_iota(I32, (N_EXPERTS, n_blocks_pad), 1).astype(F32)
    be = jnp.sum((pend1 <= bvec).astype(F32), axis=0, keepdims=True)
    be_ref[...] = jnp.minimum(be, N_EXPERTS - 1).astype(I32)
    last = lax.broadcasted_iota(I32, (N_EXPERTS, LANES), 0) == N_EXPERTS - 1
    nu_ref[...] = jnp.sum(jnp.where(last, pend, 0.0), axis=0, keepdims=True).astype(I32)

    tr = lax.broadcasted_iota(I32, (ROUTE_CHUNK, ROUTE_CHUNK), 0)
    tc = lax.broadcasted_iota(I32, (ROUTE_CHUNK, ROUTE_CHUNK), 1)
    tri_scr[...] = (tr < tc).astype(BF16)

    def dest_body(c, carry):
        sl, ms = masks(c)
        tot = onehot_sum(ms)
        before = jnp.dot(tot.astype(BF16), tri_scr[...], preferred_element_type=F32)
        pos = pstart_rows + carry + before
        for k in range(TOP_K):
            dest_ref[k:k + 1, sl] = jnp.sum(jnp.where(ms[k], pos, 0.0), axis=0, keepdims=True).astype(I32)
        return carry + jnp.sum(tot, axis=1, keepdims=True)

    lax.fori_loop(0, n_chunks, dest_body, jnp.zeros((N_EXPERTS, 1), F32))


def _dispatch_kernel(dest_ref, hn_ref, xz_hbm, xb_hbm, sem):
    del xz_hbm

    def row_copy(tt, k):
        src = pl.multiple_of(tt * SUBLANES, SUBLANES)
        dst = pl.multiple_of(dest_ref[0, 0, k * TL + tt] * SUBLANES, SUBLANES)
        return pltpu.make_async_copy(hn_ref.at[pl.ds(src, SUBLANES), :], xb_hbm.at[pl.ds(dst, SUBLANES), :], sem)

    def body(tt, carry):
        for k in range(TOP_K):
            row_copy(tt, k).start()
        return carry

    lax.fori_loop(0, TL, body, 0)
    n = TL * SUBLANES
    for k in range(TOP_K):
        pltpu.make_async_copy(hn_ref, xb_hbm.at[pl.ds(0, n), :], sem).wait()


def _expert_kernel(be_ref, nu_ref, xs_ref, wgu_ref, bgu_ref, wd_ref, bd_ref, o_ref, wgu_bf, wd_bf):
    b = pl.program_id(0)
    prev = be_ref[jnp.maximum(b - 1, 0)]
    active = b < nu_ref[0]

    @pl.when(active & ((b == 0) | (be_ref[b] != prev)))
    def _():
        wgu_bf[...] = wgu_ref[0].astype(BF16)
        wd_bf[...] = wd_ref[0].astype(BF16)

    @pl.when(active)
    def _():
        xb = _slab_to_rows(xs_ref, 0, ROWS).astype(BF16)
        gu = jnp.dot(xb, wgu_bf[...], preferred_element_type=F32) + bgu_ref[0]
        gate = jnp.minimum(gu[:, :D], SWIGLU_LIMIT)
        up = jnp.clip(gu[:, D:], -SWIGLU_LIMIT, SWIGLU_LIMIT)
        hidden = (up + 1.0) * gate * jax.nn.sigmoid(SWIGLU_ALPHA * gate)
        out = jnp.dot(hidden.astype(BF16), wd_bf[...], preferred_element_type=F32) + bd_ref[0]
        _rows_to_slab(o_ref, 0, out)

    @pl.when(jnp.logical_not(active))
    def _():
        o_ref[...] = jnp.zeros(o_ref.shape, F32)


def _combine_kernel(n_prompt_tiles, dest_ref, h_ref, gw_ref, gfin_ref, ob_hbm, yp_ref, ys_ref, gath, sem):
    def row_copy(tt, k):
        src = pl.multiple_of(dest_ref[0, 0, k * TL + tt] * SUBLANES, SUBLANES)
        dst = pl.multiple_of((k * TL + tt) * SUBLANES, SUBLANES)
        return pltpu.make_async_copy(ob_hbm.at[pl.ds(src, SUBLANES), :], gath.at[pl.ds(dst, SUBLANES), :], sem)

    def body(tt, carry):
        for k in range(TOP_K):
            row_copy(tt, k).start()
        return carry

    lax.fori_loop(0, TL, body, 0)
    n = TOP_K * TL * SUBLANES
    pltpu.make_async_copy(ob_hbm.at[pl.ds(0, n), :], gath, sem).wait()

    acc = h_ref[...]
    gw = gw_ref[...]
    for k in range(TOP_K):
        acc = acc + gw[:, k:k + 1] * _slab_to_rows(gath, k * TL, TL)
    y = _rmsnorm(acc, gfin_ref[...])
    is_prompt = pl.program_id(0) < n_prompt_tiles

    @pl.when(is_prompt)
    def _():
        yp_ref[...] = y

    @pl.when(jnp.logical_not(is_prompt))
    def _():
        ys_ref[...] = y


def _const_spec(shape):
    nd = len(shape)
    return pl.BlockSpec(shape, lambda *_: (0,) * nd, pipeline_mode=pl.Buffered(1))


def kernel(x_prompt, x_sample, state_conv_a, state_conv_b, g_mix, w_in, b_in, w_conv_a, w_out_a, w_conv_b, b_conv_b, ln_g, ln_b, w_out_b, b_out_b, w_o, g_ffn, w_router, b_router, w_gate_up, b_gate_up, w_down, b_down, g_final):
    depth = g_mix.shape[0]
    assert depth == 1
    bp, lp, d = x_prompt.shape
    bs, ls, _ = x_sample.shape
    assert d == D and w_in.shape[2] == 7 * D
    tp, ts = bp * lp, bs * ls
    t = tp + ts
    assert lp % TL == 0 and TL % ls == 0 and ts % TL == 0 and t % ROUTE_CHUNK == 0
    npt, nst = tp // TL, ts // TL
    nt = npt + nst
    tiles_per_seq = lp // TL
    seq_per_tile = TL // ls
    n_blocks = (t * TOP_K) // ROWS + N_EXPERTS
    n_blocks_pad = -(-n_blocks // LANES) * LANES
    p_rows = n_blocks * ROWS

    row = lambda a: a.reshape(1, -1)
    f32_spec = lambda: _const_spec((1, D))

    mixer = pl.pallas_call(
        functools.partial(_mixer_kernel, npt, tiles_per_seq, seq_per_tile, ls),
        grid=(nt,),
        in_specs=[
            pl.BlockSpec((TL, D), lambda i: (jnp.minimum(i, npt - 1), 0)),
            pl.BlockSpec((TL, D), lambda i: (jnp.maximum(i - npt, 0), 0)),
            pl.BlockSpec((seq_per_tile, TAPS_A - 1, D), lambda i: (jnp.maximum(i - npt, 0), 0, 0)),
            pl.BlockSpec((seq_per_tile, TAPS_B - 1, D), lambda i: (jnp.maximum(i - npt, 0), 0, 0)),
            f32_spec(),
            _const_spec((D, 7 * D)),
            _const_spec((1, 7 * D)),
            _const_spec((TAPS_A, SUBLANES, LANES)),
            _const_spec((D, D)),
            _const_spec((TAPS_B, SUBLANES, LANES)),
            f32_spec(), f32_spec(), f32_spec(),
            _const_spec((D, D)),
            f32_spec(),
            _const_spec((D, D)),
            f32_spec(),
            _const_spec((N_EXPERTS, D)),
            _const_spec((N_EXPERTS, 1)),
        ],
        out_specs=[
            pl.BlockSpec((TL, D), lambda i: (i, 0)),
            pl.BlockSpec((TL * SUBLANES, LANES), lambda i: (i, 0)),
            pl.BlockSpec((TOP_K, TL), lambda i: (0, i)),
            pl.BlockSpec((TOP_K, TL), lambda i: (0, i)),
            pl.BlockSpec((1, TAPS_A - 1, D), lambda i: (jnp.minimum(i // tiles_per_seq, bp - 1), 0, 0)),
            pl.BlockSpec((1, TAPS_B - 1, D), lambda i: (jnp.minimum(i // tiles_per_seq, bp - 1), 0, 0)),
            pl.BlockSpec((seq_per_tile, TAPS_A - 1, D), lambda i: (jnp.maximum(i - npt, 0), 0, 0)),
            pl.BlockSpec((seq_per_tile, TAPS_B - 1, D), lambda i: (jnp.maximum(i - npt, 0), 0, 0)),
        ],
        out_shape=[
            jax.ShapeDtypeStruct((t, D), F32),
            jax.ShapeDtypeStruct((t * SUBLANES, LANES), F32),
            jax.ShapeDtypeStruct((TOP_K, t), I32),
            jax.ShapeDtypeStruct((TOP_K, t), F32),
            jax.ShapeDtypeStruct((bp, TAPS_A - 1, D), F32),
            jax.ShapeDtypeStruct((bp, TAPS_B - 1, D), F32),
            jax.ShapeDtypeStruct((bs, TAPS_A - 1, D), F32),
            jax.ShapeDtypeStruct((bs, TAPS_B - 1, D), F32),
        ],
        scratch_shapes=[
            pltpu.VMEM((TL, D), F32),
            pltpu.VMEM((TL, D), F32),
            pltpu.VMEM((max(HIST_A + TL, seq_per_tile * (HIST_A + ls)) * SUBLANES, LANES), F32),
            pltpu.VMEM((max(HIST + TL, seq_per_tile * (HIST + ls)) * SUBLANES, LANES), F32),
            pltpu.VMEM((TL * SUBLANES, LANES), F32),
            pltpu.VMEM((TL * SUBLANES, LANES), F32),
        ],
        compiler_params=pltpu.CompilerParams(dimension_semantics=("arbitrary",), vmem_limit_bytes=VMEM_LIMIT),
        name="mixer",
    )
    h, hn_slab, idx_t, gw_t, na_p, nb_p, na_s, nb_s = mixer(
        x_prompt.reshape(tp, D), x_sample.reshape(ts, D), state_conv_a[0], state_conv_b[0],
        row(g_mix), w_in[0].astype(BF16), row(b_in), w_conv_a[0].reshape(TAPS_A, SUBLANES, LANES),
        w_out_a[0].astype(BF16), w_conv_b[0].reshape(TAPS_B, SUBLANES, LANES), row(b_conv_b), row(ln_g), row(ln_b),
        w_out_b[0].astype(BF16), row(b_out_b), w_o[0].astype(BF16), row(g_ffn),
        w_router[0].T, b_router[0].reshape(N_EXPERTS, 1))

    dest_t, block_e, n_used = pl.pallas_call(
        functools.partial(_route_kernel, t, n_blocks_pad),
        out_shape=[
            jax.ShapeDtypeStruct((TOP_K, t), I32),
            jax.ShapeDtypeStruct((1, n_blocks_pad), I32),
            jax.ShapeDtypeStruct((1, LANES), I32),
        ],
        scratch_shapes=[pltpu.VMEM((ROUTE_CHUNK, ROUTE_CHUNK), BF16)],
        compiler_params=pltpu.CompilerParams(vmem_limit_bytes=VMEM_LIMIT),
        name="route",
    )(idx_t)
    dest_sm = dest_t.reshape(TOP_K, nt, TL).transpose(1, 0, 2).reshape(nt, 1, TOP_K * TL)
    block_e = block_e[0, :n_blocks]
    n_used = n_used[0, :1]

    smem_tile_spec = pl.BlockSpec((1, 1, TOP_K * TL), lambda i: (i, 0, 0), memory_space=pltpu.SMEM)
    any_spec = pl.BlockSpec(memory_space=pl.ANY)
    x_buf = pl.pallas_call(
        _dispatch_kernel,
        grid=(nt,),
        in_specs=[smem_tile_spec, pl.BlockSpec((TL * SUBLANES, LANES), lambda i: (i, 0)), any_spec],
        out_specs=any_spec,
        out_shape=jax.ShapeDtypeStruct((p_rows * SUBLANES, LANES), F32),
        scratch_shapes=[pltpu.SemaphoreType.DMA],
        input_output_aliases={2: 0},
        compiler_params=pltpu.CompilerParams(dimension_semantics=("arbitrary",)),
        name="dispatch",
    )(dest_sm, hn_slab, jnp.zeros((p_rows * SUBLANES, LANES), F32))

    def blk(b, be, nu):
        return jnp.minimum(b, nu[0] - 1)

    out_buf = pl.pallas_call(
        _expert_kernel,
        grid_spec=pltpu.PrefetchScalarGridSpec(
            num_scalar_prefetch=2,
            grid=(n_blocks,),
            in_specs=[
                pl.BlockSpec((ROWS * SUBLANES, LANES), lambda b, be, nu: (blk(b, be, nu), 0)),
                pl.BlockSpec((1, D, 2 * D), lambda b, be, nu: (be[blk(b, be, nu)], 0, 0)),
                pl.BlockSpec((1, 1, 2 * D), lambda b, be, nu: (be[blk(b, be, nu)], 0, 0)),
                pl.BlockSpec((1, D, D), lambda b, be, nu: (be[blk(b, be, nu)], 0, 0)),
                pl.BlockSpec((1, 1, D), lambda b, be, nu: (be[blk(b, be, nu)], 0, 0)),
            ],
            out_specs=pl.BlockSpec((ROWS * SUBLANES, LANES), lambda b, be, nu: (b, 0)),
            scratch_shapes=[pltpu.VMEM((D, 2 * D), BF16), pltpu.VMEM((D, D), BF16)],
        ),
        out_shape=jax.ShapeDtypeStruct((p_rows * SUBLANES, LANES), F32),
        compiler_params=pltpu.CompilerParams(dimension_semantics=("arbitrary",), vmem_limit_bytes=VMEM_LIMIT),
        name="experts",
    )(block_e, n_used, x_buf, w_gate_up[0], b_gate_up[0].reshape(N_EXPERTS, 1, 2 * D), w_down[0],
      b_down[0].reshape(N_EXPERTS, 1, D))

    y_p, y_s = pl.pallas_call(
        functools.partial(_combine_kernel, npt),
        grid=(nt,),
        in_specs=[
            smem_tile_spec,
            pl.BlockSpec((TL, D), lambda i: (i, 0)),
            pl.BlockSpec((TL, TOP_K), lambda i: (i, 0)),
            pl.BlockSpec((1, D), lambda i: (0, 0)),
            any_spec,
        ],
        out_specs=[
            pl.BlockSpec((TL, D), lambda i: (jnp.minimum(i, npt - 1), 0)),
            pl.BlockSpec((TL, D), lambda i: (jnp.maximum(i - npt, 0), 0)),
        ],
        out_shape=[jax.ShapeDtypeStruct((tp, D), F32), jax.ShapeDtypeStruct((ts, D), F32)],
        scratch_shapes=[pltpu.VMEM((TOP_K * TL * SUBLANES, LANES), F32), pltpu.SemaphoreType.DMA],
        compiler_params=pltpu.CompilerParams(dimension_semantics=("arbitrary",), vmem_limit_bytes=VMEM_LIMIT),
        name="combine",
    )(dest_sm, h, gw_t.T, row(g_final), out_buf)

    return (y_p.reshape(bp, lp, D), y_s.reshape(bs, ls, D), na_p[None], nb_p[None], na_s[None], nb_s[None])
```

```python
import functools

import jax
import jax.numpy as jnp
from jax import lax
from jax.experimental import pallas as pl
from jax.experimental.pallas import tpu as pltpu

F32 = jnp.float32
BF16 = jnp.bfloat16
I32 = jnp.int32

D = 1024
LANES = 128
SUBLANES = 8
NCHUNK = D // LANES
N_EXPERTS = 32
TOP_K = 4
TAPS_A = 3
TAPS_B = 31
EPS = 1e-5
SWIGLU_LIMIT = 7.0
SWIGLU_ALPHA = 1.702

TL = 256
HIST = 32
HIST_A = 8
CONV_UNROLL = 4
ROWS = 256
ROUTE_CHUNK = 512
VMEM_LIMIT = 56 * 1024 * 1024


def _rows_to_slab(dst_ref, slot0, val):
    n = val.shape[0]
    for c in range(NCHUNK):
        dst_ref[pl.ds(SUBLANES * slot0 + c, n, stride=SUBLANES), :] = val[:, c * LANES:(c + 1) * LANES]


def _slab_to_rows(src_ref, slot0, n):
    return jnp.concatenate(
        [src_ref[pl.ds(SUBLANES * slot0 + c, n, stride=SUBLANES), :] for c in range(NCHUNK)], axis=1)


def _conv_slab(src_ref, w_ref, ntaps, dst_ref, n_out, seg_len, seg_stride, first_tap_slot):
    w = [w_ref[j] for j in range(ntaps)]
    seg_shift = seg_len.bit_length() - 1
    assert (1 << seg_shift) == seg_len

    def body(q, carry):
        for u in range(CONV_UNROLL):
            o = q * CONV_UNROLL + u
            seg = o >> seg_shift
            step = o & (seg_len - 1)
            slot = seg * seg_stride + first_tap_slot + step
            base = pl.multiple_of(slot * SUBLANES, SUBLANES)
            acc = w[0] * src_ref[pl.ds(base, SUBLANES), :]
            for j in range(1, ntaps):
                acc = acc + w[j] * src_ref[pl.ds(base + SUBLANES * j, SUBLANES), :]
            dst_ref[pl.ds(pl.multiple_of(o * SUBLANES, SUBLANES), SUBLANES), :] = acc
        return carry

    lax.fori_loop(0, n_out // CONV_UNROLL, body, 0)


def _rmsnorm(x, g):
    return x * lax.rsqrt(jnp.mean(x * x, axis=-1, keepdims=True) + EPS) * g


def _mixer_kernel(n_prompt_tiles, tiles_per_seq, seq_per_tile, seq_len_s,
                  xp_ref, xs_ref, sta_ref, stb_ref, gmix_ref, win_ref, bin_ref, wca_ref, woa_ref,
                  wcb_ref, bcb_ref, lng_ref, lnb_ref, wob_ref, bob_ref, wo_ref, gffn_ref, wrt_ref, brt_ref,
                  h_ref, hn_ref, idx_ref, gw_ref, nap_ref, nbp_ref, nas_ref, nbs_ref,
                  cu_scr, v_scr, cuslab, vslab, yaslab, ybslab):
    i = pl.program_id(0)
    is_prompt = i < n_prompt_tiles

    x = jnp.where(is_prompt, xp_ref[...], xs_ref[...])
    n_bf = _rmsnorm(x, gmix_ref[...]).astype(BF16)

    def proj(g):
        cols = slice(g * D, (g + 1) * D)
        return jnp.dot(n_bf, win_ref[:, cols], preferred_element_type=F32) + bin_ref[:, cols]

    cu_scr[...] = proj(1) * proj(2)
    v_scr[...] = proj(3) * jax.nn.sigmoid(proj(4))

    @pl.when(is_prompt)
    def _():
        j = lax.rem(i, tiles_per_seq)

        @pl.when(j == 0)
        def _():
            cuslab[0:HIST_A * SUBLANES, :] = jnp.zeros((HIST_A * SUBLANES, LANES), F32)
            vslab[0:HIST * SUBLANES, :] = jnp.zeros((HIST * SUBLANES, LANES), F32)

        _rows_to_slab(cuslab, HIST_A, cu_scr[...])
        _rows_to_slab(vslab, HIST, v_scr[...])
        _conv_slab(cuslab, wca_ref, TAPS_A, yaslab, TL, TL, 0, HIST_A - (TAPS_A - 1))
        _conv_slab(vslab, wcb_ref, TAPS_B, ybslab, TL, TL, 0, HIST - (TAPS_B - 1))
        cuslab[0:HIST_A * SUBLANES, :] = cuslab[TL * SUBLANES:(TL + HIST_A) * SUBLANES, :]
        vslab[0:HIST * SUBLANES, :] = vslab[TL * SUBLANES:(TL + HIST) * SUBLANES, :]
        nap_ref[0] = cu_scr[TL - (TAPS_A - 1):TL, :]
        nbp_ref[0] = v_scr[TL - (TAPS_B - 1):TL, :]

    @pl.when(jnp.logical_not(is_prompt))
    def _():
        seg_a = HIST_A + seq_len_s
        seg_b = HIST + seq_len_s
        for s in range(seq_per_tile):
            r0 = s * seq_len_s
            for c in range(NCHUNK):
                cols = slice(c * LANES, (c + 1) * LANES)
                cuslab[pl.ds(SUBLANES * (seg_a * s + HIST_A - (TAPS_A - 1)) + c, TAPS_A - 1, stride=SUBLANES), :] = (
                    sta_ref[s, :, cols])
                vslab[pl.ds(SUBLANES * (seg_b * s + HIST - (TAPS_B - 1)) + c, TAPS_B - 1, stride=SUBLANES), :] = (
                    stb_ref[s, :, cols])
            _rows_to_slab(cuslab, seg_a * s + HIST_A, cu_scr[r0:r0 + seq_len_s, :])
            _rows_to_slab(vslab, seg_b * s + HIST, v_scr[r0:r0 + seq_len_s, :])
            nas_ref[s] = cu_scr[r0 + seq_len_s - (TAPS_A - 1):r0 + seq_len_s, :]
            nbs_ref[s] = v_scr[r0 + seq_len_s - (TAPS_B - 1):r0 + seq_len_s, :]
        _conv_slab(cuslab, wca_ref, TAPS_A, yaslab, TL, seq_len_s, seg_a, HIST_A - (TAPS_A - 1))
        _conv_slab(vslab, wcb_ref, TAPS_B, ybslab, TL, seq_len_s, seg_b, HIST - (TAPS_B - 1))

    ya = _slab_to_rows(yaslab, 0, TL)
    yb = _slab_to_rows(ybslab, 0, TL) + bcb_ref[...]

    out_a = jnp.dot((proj(0) * ya).astype(BF16), woa_ref[...], preferred_element_type=F32)
    mu = jnp.mean(yb, axis=-1, keepdims=True)
    yc = yb - mu
    var = jnp.mean(yc * yc, axis=-1, keepdims=True)
    ln = yc * lax.rsqrt(var + EPS) * lng_ref[...] + lnb_ref[...]
    act = ln * jax.nn.sigmoid(ln)
    out_b = jnp.dot(act.astype(BF16), wob_ref[...], preferred_element_type=F32) + bob_ref[...]
    mix = jax.nn.sigmoid(proj(5)) * out_a + jax.nn.sigmoid(proj(6)) * out_b
    h = x + jnp.dot(mix.astype(BF16), wo_ref[...], preferred_element_type=F32)
    h_ref[...] = h
    hn = _rmsnorm(h, gffn_ref[...])
    _rows_to_slab(hn_ref, 0, hn)

    logits = lax.dot_general(wrt_ref[...], hn, (((1,), (1,)), ((), ())),
                             precision=lax.Precision.HIGHEST, preferred_element_type=F32) + brt_ref[...]
    iota_e = lax.broadcasted_iota(I32, (N_EXPERTS, TL), 0).astype(F32)
    vals, idxs = [], []
    for _ in range(TOP_K):
        m = jnp.max(logits, axis=0, keepdims=True)
        sel = jnp.min(jnp.where(logits == m, iota_e, float(N_EXPERTS)), axis=0, keepdims=True)
        vals.append(m)
        idxs.append(sel)
        logits = jnp.where(iota_e == sel, -jnp.inf, logits)
    ex = [jnp.exp(v - vals[0]) for v in vals]
    denom = ex[0] + ex[1] + ex[2] + ex[3]
    for k in range(TOP_K):
        idx_ref[k:k + 1, :] = idxs[k].astype(I32)
        gw_ref[k:k + 1, :] = ex[k] / denom


def _route_kernel(n_tokens, n_blocks_pad, idx_ref, dest_ref, be_ref, nu_ref, pad_start_ref, pad_len_ref, tri_scr):
    n_chunks = n_tokens // ROUTE_CHUNK
    iota_e = lax.broadcasted_iota(I32, (N_EXPERTS, ROUTE_CHUNK), 0)

    def masks(c):
        sl = pl.ds(pl.multiple_of(c * ROUTE_CHUNK, ROUTE_CHUNK), ROUTE_CHUNK)
        return sl, [idx_ref[k:k + 1, sl] == iota_e for k in range(TOP_K)]

    def onehot_sum(ms):
        tot = ms[0].astype(F32)
        for k in range(1, TOP_K):
            tot = tot + ms[k].astype(F32)
        return tot

    def count_body(c, acc):
        _, ms = masks(c)
        return acc + onehot_sum(ms)

    acc = lax.fori_loop(0, n_chunks, count_body, jnp.zeros((N_EXPERTS, ROUTE_CHUNK), F32))
    counts = jnp.sum(acc, axis=1, keepdims=True)
    nblk = jnp.floor((counts + (ROWS - 1)) * (1.0 / ROWS))
    r = lax.broadcasted_iota(I32, (N_EXPERTS, N_EXPERTS), 0)
    cidx = lax.broadcasted_iota(I32, (N_EXPERTS, N_EXPERTS), 1)
    lower = (cidx <= r).astype(F32)
    pend = jnp.dot(lower, jnp.broadcast_to(nblk, (N_EXPERTS, LANES)),
                   precision=lax.Precision.HIGHEST, preferred_element_type=F32)
    pend1 = pend[:, 0:1]
    pstart_rows = (pend1 - nblk) * ROWS
    pad_start_ref[...] = jnp.broadcast_to(pstart_rows + counts, (N_EXPERTS, LANES)).astype(I32)
    pad_len_ref[...] = jnp.broadcast_to(nblk * ROWS - counts, (N_EXPERTS, LANES)).astype(I32)

    bvec = lax.broadcasted_iota(I32, (N_EXPERTS, n_blocks_pad), 1).astype(F32)
    be = jnp.sum((pend1 <= bvec).astype(F32), axis=0, keepdims=True)
    be_ref[...] = jnp.minimum(be, N_EXPERTS - 1).astype(I32)
    last = lax.broadcasted_iota(I32, (N_EXPERTS, LANES), 0) == N_EXPERTS - 1
    nu_ref[...] = jnp.sum(jnp.where(last, pend, 0.0), axis=0, keepdims=True).astype(I32)

    tr = lax.broadcasted_iota(I32, (ROUTE_CHUNK, ROUTE_CHUNK), 0)
    tc = lax.broadcasted_iota(I32, (ROUTE_CHUNK, ROUTE_CHUNK), 1)
    tri_scr[...] = (tr < tc).astype(BF16)

    def dest_body(c, carry):
        sl, ms = masks(c)
        tot = onehot_sum(ms)
        before = jnp.dot(tot.astype(BF16), tri_scr[...], preferred_element_type=F32)
        pos = pstart_rows + carry + before
        for k in range(TOP_K):
            dest_ref[k:k + 1, sl] = jnp.sum(jnp.where(ms[k], pos, 0.0), axis=0, keepdims=True).astype(I32)
        return carry + jnp.sum(tot, axis=1, keepdims=True)

    lax.fori_loop(0, n_chunks, dest_body, jnp.zeros((N_EXPERTS, 1), F32))


def _dispatch_kernel(n_blocks, pad_start_ref, pad_len_ref, nu_ref, dest_ref, hn_ref, xb_hbm, ztile, sem, pad_sem):
    def row_copy(tt, k):
        src = pl.multiple_of(tt * SUBLANES, SUBLANES)
        dst = pl.multiple_of(dest_ref[0, 0, k * TL + tt] * SUBLANES, SUBLANES)
        return pltpu.make_async_copy(hn_ref.at[pl.ds(src, SUBLANES), :], xb_hbm.at[pl.ds(dst, SUBLANES), :], sem)

    def body(tt, carry):
        for k in range(TOP_K):
            row_copy(tt, k).start(priority=k % 2)
        return carry

    lax.fori_loop(0, TL, body, 0)

    @pl.when(pl.program_id(0) == 0)
    def _():
        ztile[...] = jnp.zeros(ztile.shape, F32)

        def pad_copy(row):
            dst = pl.multiple_of(row * SUBLANES, SUBLANES)
            return pltpu.make_async_copy(ztile.at[pl.ds(0, SUBLANES), :], xb_hbm.at[pl.ds(dst, SUBLANES), :], pad_sem)

        def tail_copy(blk):
            dst = pl.multiple_of(blk * (ROWS * SUBLANES), ROWS * SUBLANES)
            return pltpu.make_async_copy(ztile, xb_hbm.at[pl.ds(dst, ROWS * SUBLANES), :], pad_sem)

        def run(copy_at, lo, hi):
            def issue(r, carry):
                copy_at(r).start()
                return carry

            def drain(r, carry):
                copy_at(r).wait()
                return carry

            lax.fori_loop(lo, hi, issue, 0)
            lax.fori_loop(lo, hi, drain, 0)

        for e in range(N_EXPERTS):
            start = pad_start_ref[e]
            run(pad_copy, start, start + pad_len_ref[e])
        run(tail_copy, nu_ref[0], n_blocks)

    n = TL * SUBLANES
    for k in range(TOP_K):
        pltpu.make_async_copy(hn_ref, xb_hbm.at[pl.ds(0, n), :], sem).wait()


def _expert_kernel(be_ref, nu_ref, xs_ref, wgu_hbm, bgu_ref, wd_hbm, bd_ref, o_ref,
                   wgu_f32, wd_f32, wgu_bf, wd_bf, slot_ref, sems):
    b = pl.program_id(0)
    n_used = nu_ref[0]
    e = be_ref[b]
    prev = be_ref[jnp.maximum(b - 1, 0)]
    active = b < n_used

    def weight_copies(expert, slot):
        return (pltpu.make_async_copy(wgu_hbm.at[expert], wgu_f32.at[slot], sems.at[0, slot]),
                pltpu.make_async_copy(wd_hbm.at[expert], wd_f32.at[slot], sems.at[1, slot]))

    @pl.when(active & (b == 0))
    def _():
        slot_ref[0] = 0
        for c in weight_copies(e, 0):
            c.start()

    @pl.when(active & ((b == 0) | (e != prev)))
    def _():
        slot = slot_ref[0]
        for c in weight_copies(e, slot):
            c.wait()
        last = n_used - 1
        nxt = lax.while_loop(lambda j: (j <= last) & (be_ref[jnp.minimum(j, last)] == e), lambda j: j + 1, b + 1)

        @pl.when(nxt <= last)
        def _():
            for c in weight_copies(be_ref[jnp.minimum(nxt, last)], 1 - slot):
                c.start()

        wgu_bf[...] = wgu_f32[slot].astype(BF16)
        wd_bf[...] = wd_f32[slot].astype(BF16)
        slot_ref[0] = 1 - slot

    @pl.when(active)
    def _():
        xb = _slab_to_rows(xs_ref, 0, ROWS).astype(BF16)
        gu = jnp.dot(xb, wgu_bf[...], preferred_element_type=F32) + bgu_ref[0]
        gate = jnp.minimum(gu[:, :D], SWIGLU_LIMIT)
        up = jnp.clip(gu[:, D:], -SWIGLU_LIMIT, SWIGLU_LIMIT)
        hidden = (up + 1.0) * gate * jax.nn.sigmoid(SWIGLU_ALPHA * gate)
        out = jnp.dot(hidden.astype(BF16), wd_bf[...], preferred_element_type=F32) + bd_ref[0]
        _rows_to_slab(o_ref, 0, out)

    @pl.when(jnp.logical_not(active))
    def _():
        o_ref[...] = jnp.zeros(o_ref.shape, F32)


def _combine_kernel(n_prompt_tiles, dest_ref, h_ref, gw_ref, gfin_ref, ob_hbm, yp_ref, ys_ref, gath, sem):
    def row_copy(tt, k):
        src = pl.multiple_of(dest_ref[0, 0, k * TL + tt] * SUBLANES, SUBLANES)
        dst = pl.multiple_of((k * TL + tt) * SUBLANES, SUBLANES)
        return pltpu.make_async_copy(ob_hbm.at[pl.ds(src, SUBLANES), :], gath.at[pl.ds(dst, SUBLANES), :], sem)

    def body(tt, carry):
        for k in range(TOP_K):
            row_copy(tt, k).start(priority=k % 2)
        return carry

    lax.fori_loop(0, TL, body, 0)
    n = TOP_K * TL * SUBLANES
    pltpu.make_async_copy(ob_hbm.at[pl.ds(0, n), :], gath, sem).wait()

    acc = h_ref[...]
    gw = gw_ref[...]
    for k in range(TOP_K):
        acc = acc + gw[:, k:k + 1] * _slab_to_rows(gath, k * TL, TL)
    y = _rmsnorm(acc, gfin_ref[...])
    is_prompt = pl.program_id(0) < n_prompt_tiles

    @pl.when(is_prompt)
    def _():
        yp_ref[...] = y

    @pl.when(jnp.logical_not(is_prompt))
    def _():
        ys_ref[...] = y


def _const_spec(shape):
    nd = len(shape)
    return pl.BlockSpec(shape, lambda *_: (0,) * nd, pipeline_mode=pl.Buffered(1))


def kernel(x_prompt, x_sample, state_conv_a, state_conv_b, g_mix, w_in, b_in, w_conv_a, w_out_a, w_conv_b, b_conv_b, ln_g, ln_b, w_out_b, b_out_b, w_o, g_ffn, w_router, b_router, w_gate_up, b_gate_up, w_down, b_down, g_final):
    depth = g_mix.shape[0]
    assert depth == 1
    bp, lp, d = x_prompt.shape
    bs, ls, _ = x_sample.shape
    assert d == D and w_in.shape[2] == 7 * D
    tp, ts = bp * lp, bs * ls
    t = tp + ts
    assert lp % TL == 0 and TL % ls == 0 and ts % TL == 0 and t % ROUTE_CHUNK == 0
    npt, nst = tp // TL, ts // TL
    nt = npt + nst
    tiles_per_seq = lp // TL
    seq_per_tile = TL // ls
    n_blocks = (t * TOP_K) // ROWS + N_EXPERTS
    n_blocks_pad = -(-n_blocks // LANES) * LANES
    p_rows = n_blocks * ROWS

    row = lambda a: a.reshape(1, -1)
    f32_spec = lambda: _const_spec((1, D))

    mixer = pl.pallas_call(
        functools.partial(_mixer_kernel, npt, tiles_per_seq, seq_per_tile, ls),
        grid=(nt,),
        in_specs=[
            pl.BlockSpec((TL, D), lambda i: (jnp.minimum(i, npt - 1), 0)),
            pl.BlockSpec((TL, D), lambda i: (jnp.maximum(i - npt, 0), 0)),
            pl.BlockSpec((seq_per_tile, TAPS_A - 1, D), lambda i: (jnp.maximum(i - npt, 0), 0, 0)),
            pl.BlockSpec((seq_per_tile, TAPS_B - 1, D), lambda i: (jnp.maximum(i - npt, 0), 0, 0)),
            f32_spec(),
            _const_spec((D, 7 * D)),
            _const_spec((1, 7 * D)),
            _const_spec((TAPS_A, SUBLANES, LANES)),
            _const_spec((D, D)),
            _const_spec((TAPS_B, SUBLANES, LANES)),
            f32_spec(), f32_spec(), f32_spec(),
            _const_spec((D, D)),
            f32_spec(),
            _const_spec((D, D)),
            f32_spec(),
            _const_spec((N_EXPERTS, D)),
            _const_spec((N_EXPERTS, 1)),
        ],
        out_specs=[
            pl.BlockSpec((TL, D), lambda i: (i, 0)),
            pl.BlockSpec((TL * SUBLANES, LANES), lambda i: (i, 0)),
            pl.BlockSpec((TOP_K, TL), lambda i: (0, i)),
            pl.BlockSpec((TOP_K, TL), lambda i: (0, i)),
            pl.BlockSpec((1, TAPS_A - 1, D), lambda i: (jnp.minimum(i // tiles_per_seq, bp - 1), 0, 0)),
            pl.BlockSpec((1, TAPS_B - 1, D), lambda i: (jnp.minimum(i // tiles_per_seq, bp - 1), 0, 0)),
            pl.BlockSpec((seq_per_tile, TAPS_A - 1, D), lambda i: (jnp.maximum(i - npt, 0), 0, 0)),
            pl.BlockSpec((seq_per_tile, TAPS_B - 1, D), lambda i: (jnp.maximum(i - npt, 0), 0, 0)),
        ],
        out_shape=[
            jax.ShapeDtypeStruct((t, D), F32),
            jax.ShapeDtypeStruct((t * SUBLANES, LANES), F32),
            jax.ShapeDtypeStruct((TOP_K, t), I32),
            jax.ShapeDtypeStruct((TOP_K, t), F32),
            jax.ShapeDtypeStruct((bp, TAPS_A - 1, D), F32),
            jax.ShapeDtypeStruct((bp, TAPS_B - 1, D), F32),
            jax.ShapeDtypeStruct((bs, TAPS_A - 1, D), F32),
            jax.ShapeDtypeStruct((bs, TAPS_B - 1, D), F32),
        ],
        scratch_shapes=[
            pltpu.VMEM((TL, D), F32),
            pltpu.VMEM((TL, D), F32),
            pltpu.VMEM((max(HIST_A + TL, seq_per_tile * (HIST_A + ls)) * SUBLANES, LANES), F32),
            pltpu.VMEM((max(HIST + TL, seq_per_tile * (HIST + ls)) * SUBLANES, LANES), F32),
            pltpu.VMEM((TL * SUBLANES, LANES), F32),
            pltpu.VMEM((TL * SUBLANES, LANES), F32),
        ],
        compiler_params=pltpu.CompilerParams(dimension_semantics=("arbitrary",), vmem_limit_bytes=VMEM_LIMIT),
        name="mixer",
    )
    h, hn_slab, idx_t, gw_t, na_p, nb_p, na_s, nb_s = mixer(
        x_prompt.reshape(tp, D), x_sample.reshape(ts, D), state_conv_a[0], state_conv_b[0],
        row(g_mix), w_in[0].astype(BF16), row(b_in), w_conv_a[0].reshape(TAPS_A, SUBLANES, LANES),
        w_out_a[0].astype(BF16), w_conv_b[0].reshape(TAPS_B, SUBLANES, LANES), row(b_conv_b), row(ln_g), row(ln_b),
        w_out_b[0].astype(BF16), row(b_out_b), w_o[0].astype(BF16), row(g_ffn),
        w_router[0].T, b_router[0].reshape(N_EXPERTS, 1))

    dest_t, block_e, n_used, pad_start, pad_len = pl.pallas_call(
        functools.partial(_route_kernel, t, n_blocks_pad),
        out_shape=[
            jax.ShapeDtypeStruct((TOP_K, t), I32),
            jax.ShapeDtypeStruct((1, n_blocks_pad), I32),
            jax.ShapeDtypeStruct((1, LANES), I32),
            jax.ShapeDtypeStruct((N_EXPERTS, LANES), I32),
            jax.ShapeDtypeStruct((N_EXPERTS, LANES), I32),
        ],
        scratch_shapes=[pltpu.VMEM((ROUTE_CHUNK, ROUTE_CHUNK), BF16)],
        compiler_params=pltpu.CompilerParams(vmem_limit_bytes=VMEM_LIMIT),
        name="route",
    )(idx_t)
    dest_sm = dest_t.reshape(TOP_K, nt, TL).transpose(1, 0, 2).reshape(nt, 1, TOP_K * TL)
    block_e = block_e[0, :n_blocks]
    n_used = n_used[0, :1]
    pad_start, pad_len = pad_start[:, 0], pad_len[:, 0]

    smem_tile_spec = pl.BlockSpec((1, 1, TOP_K * TL), lambda i: (i, 0, 0), memory_space=pltpu.SMEM)
    any_spec = pl.BlockSpec(memory_space=pl.ANY)
    x_buf = pl.pallas_call(
        functools.partial(_dispatch_kernel, n_blocks),
        grid_spec=pltpu.PrefetchScalarGridSpec(
            num_scalar_prefetch=3,
            grid=(nt,),
            in_specs=[
                pl.BlockSpec((1, 1, TOP_K * TL), lambda i, *_: (i, 0, 0), memory_space=pltpu.SMEM),
                pl.BlockSpec((TL * SUBLANES, LANES), lambda i, *_: (i, 0)),
            ],
            out_specs=any_spec,
            scratch_shapes=[pltpu.VMEM((ROWS * SUBLANES, LANES), F32), pltpu.SemaphoreType.DMA,
                            pltpu.SemaphoreType.DMA],
        ),
        out_shape=jax.ShapeDtypeStruct((p_rows * SUBLANES, LANES), F32),
        compiler_params=pltpu.CompilerParams(dimension_semantics=("arbitrary",)),
        name="dispatch",
    )(pad_start, pad_len, n_used, dest_sm, hn_slab)

    def blk(b, be, nu):
        return jnp.maximum(jnp.minimum(b, nu[0] - 1), 0)

    out_buf = pl.pallas_call(
        _expert_kernel,
        grid_spec=pltpu.PrefetchScalarGridSpec(
            num_scalar_prefetch=2,
            grid=(n_blocks,),
            in_specs=[
                pl.BlockSpec((ROWS * SUBLANES, LANES), lambda b, be, nu: (blk(b, be, nu), 0)),
                any_spec,
                pl.BlockSpec((1, 1, 2 * D), lambda b, be, nu: (be[blk(b, be, nu)], 0, 0)),
                any_spec,
                pl.BlockSpec((1, 1, D), lambda b, be, nu: (be[blk(b, be, nu)], 0, 0)),
            ],
            out_specs=pl.BlockSpec((ROWS * SUBLANES, LANES), lambda b, be, nu: (b, 0)),
            scratch_shapes=[
                pltpu.VMEM((2, D, 2 * D), F32), pltpu.VMEM((2, D, D), F32),
                pltpu.VMEM((D, 2 * D), BF16), pltpu.VMEM((D, D), BF16),
                pltpu.SMEM((1,), I32), pltpu.SemaphoreType.DMA((2, 2)),
            ],
        ),
        out_shape=jax.ShapeDtypeStruct((p_rows * SUBLANES, LANES), F32),
        compiler_params=pltpu.CompilerParams(dimension_semantics=("arbitrary",), vmem_limit_bytes=VMEM_LIMIT),
        name="experts",
    )(block_e, n_used, x_buf, w_gate_up[0], b_gate_up[0].reshape(N_EXPERTS, 1, 2 * D), w_down[0],
      b_down[0].reshape(N_EXPERTS, 1, D))

    y_p, y_s = pl.pallas_call(
        functools.partial(_combine_kernel, npt),
        grid=(nt,),
        in_specs=[
            smem_tile_spec,
            pl.BlockSpec((TL, D), lambda i: (i, 0)),
            pl.BlockSpec((TL, TOP_K), lambda i: (i, 0)),
            pl.BlockSpec((1, D), lambda i: (0, 0)),
            any_spec,
        ],
        out_specs=[
            pl.BlockSpec((TL, D), lambda i: (jnp.minimum(i, npt - 1), 0)),
            pl.BlockSpec((TL, D), lambda i: (jnp.maximum(i - npt, 0), 0)),
        ],
        out_shape=[jax.ShapeDtypeStruct((tp, D), F32), jax.ShapeDtypeStruct((ts, D), F32)],
        scratch_shapes=[pltpu.VMEM((TOP_K * TL * SUBLANES, LANES), F32), pltpu.SemaphoreType.DMA],
        compiler_params=pltpu.CompilerParams(dimension_semantics=("arbitrary",), vmem_limit_bytes=VMEM_LIMIT),
        name="combine",
    )(dest_sm, h, gw_t.T, row(g_final), out_buf)

    return (y_p.reshape(bp, lp, D), y_s.reshape(bs, ls, D), na_p[None], nb_p[None], na_s[None], nb_s[None])
```

```python
import functools

import jax
import jax.numpy as jnp
from jax import lax
from jax.experimental import pallas as pl
from jax.experimental.pallas import tpu as pltpu

F32 = jnp.float32
BF16 = jnp.bfloat16
I32 = jnp.int32

D = 1024
LANES = 128
SUBLANES = 8
NCHUNK = D // LANES
N_EXPERTS = 32
TOP_K = 4
TAPS_A = 3
TAPS_B = 31
EPS = 1e-5
SWIGLU_LIMIT = 7.0
SWIGLU_ALPHA = 1.702

TL = 256
SEG = 32
NSEG = TL // SEG
HIST = 32
HIST_A = 8
CONV_BLOCK = 4
ROWS = 256
ROUTE_CHUNK = 512
VMEM_LIMIT = 56 * 1024 * 1024


def _rows_to_slab(dst_ref, slot0, val):
    n = val.shape[0]
    for c in range(NCHUNK):
        dst_ref[pl.ds(SUBLANES * slot0 + c, n, stride=SUBLANES), :] = val[:, c * LANES:(c + 1) * LANES]


def _slab_to_rows(src_ref, slot0, n):
    return jnp.concatenate(
        [src_ref[pl.ds(SUBLANES * slot0 + c, n, stride=SUBLANES), :] for c in range(NCHUNK)], axis=1)


def _rmsnorm(x, g):
    return x * lax.rsqrt(jnp.mean(x * x, axis=-1, keepdims=True) + EPS) * g


def _conv_b_slab(vbuf, hist_rows, w_ref, dst_ref):
    first_tap = HIST - (TAPS_B - 1)
    for s in range(NSEG):
        cur_row0 = (HIST + s * SEG) * SUBLANES
        for t0 in range(0, SEG, CONV_BLOCK):
            ins = []
            for k in range(CONV_BLOCK + TAPS_B - 1):
                u = t0 + first_tap + k
                if u < HIST:
                    row = pl.multiple_of(hist_rows[s] + u * SUBLANES, SUBLANES)
                    ins.append(vbuf[pl.ds(row, SUBLANES), :])
                else:
                    row = cur_row0 + (u - HIST) * SUBLANES
                    ins.append(vbuf[row:row + SUBLANES, :])
            accs = [None] * CONV_BLOCK
            for j in range(TAPS_B):
                wj = w_ref[j]
                for o in range(CONV_BLOCK):
                    term = wj * ins[o + j]
                    accs[o] = term if accs[o] is None else accs[o] + term
            for o in range(CONV_BLOCK):
                row = (s * SEG + t0 + o) * SUBLANES
                dst_ref[row:row + SUBLANES, :] = accs[o]


def _mixer_kernel(n_prompt_tiles, tiles_per_seq, n_prompt_seq, n_sample_seq,
                  xp_ref, xs_ref, sta_ref, stb_ref, gmix_ref, win_ref, bin_ref, wca_ref, woa_ref,
                  wcb_ref, bcb_ref, lng_ref, lnb_ref, wob_ref, bob_ref, wo_ref, gffn_ref, wrt_ref, brt_ref,
                  h_ref, hn_ref, idx_ref, gw_ref, nap_ref, nbp_ref, nas_ref, nbs_ref,
                  cubuf, carry_a, vbuf, ybslab, sa_scr, sb_scr):
    i = pl.program_id(0)
    n_tiles = pl.num_programs(0)
    is_prompt = i < n_prompt_tiles
    seq_start = is_prompt & (lax.rem(i, tiles_per_seq) == 0)
    carry_rows = HIST * SUBLANES

    @pl.when(i == 0)
    def _():
        vbuf[0:carry_rows, :] = jnp.zeros((carry_rows, LANES), F32)
        carry_a[...] = jnp.zeros(carry_a.shape, F32)

    x = jnp.where(is_prompt, xp_ref[...], xs_ref[...])
    n_bf = _rmsnorm(x, gmix_ref[...]).astype(BF16)

    def proj(g):
        cols = slice(g * D, (g + 1) * D)
        return jnp.dot(n_bf, win_ref[:, cols], preferred_element_type=F32) + bin_ref[:, cols]

    cu = proj(1) * proj(2)
    v = proj(3) * jax.nn.sigmoid(proj(4))

    seg_rows = HIST_A + SEG
    prev_a = jnp.where(seq_start, 0.0, carry_a[HIST_A - (TAPS_A - 1):HIST_A, :])
    for s in range(NSEG):
        r0 = s * SEG
        prev = prev_a if s == 0 else cu[r0 - (TAPS_A - 1):r0, :]
        cubuf[s * seg_rows + HIST_A - (TAPS_A - 1):s * seg_rows + HIST_A, :] = jnp.where(is_prompt, prev, sta_ref[s])
        cubuf[s * seg_rows + HIST_A:(s + 1) * seg_rows, :] = cu[r0:r0 + SEG, :]
    carry_a[HIST_A - (TAPS_A - 1):HIST_A, :] = cu[TL - (TAPS_A - 1):TL, :]
    ya_parts = []
    for s in range(NSEG):
        acc = None
        for j in range(TAPS_A):
            lo = s * seg_rows + HIST_A - (TAPS_A - 1) + j
            term = wca_ref[j:j + 1, :] * cubuf[lo:lo + SEG, :]
            acc = term if acc is None else acc + term
        ya_parts.append(acc)
    ya = jnp.concatenate(ya_parts, axis=0)

    vbuf[0:carry_rows, :] = jnp.where(seq_start, 0.0, vbuf[0:carry_rows, :])
    _rows_to_slab(vbuf, HIST, v)
    state_slot0 = HIST + TL
    for s in range(NSEG):
        for c in range(NCHUNK):
            vbuf[pl.ds(SUBLANES * (state_slot0 + s * HIST + HIST - (TAPS_B - 1)) + c, TAPS_B - 1, stride=SUBLANES), :] = (
                stb_ref[s, :, c * LANES:(c + 1) * LANES])
    hist_rows = [jnp.where(is_prompt, s * SEG * SUBLANES, (state_slot0 + s * HIST) * SUBLANES) for s in range(NSEG)]
    _conv_b_slab(vbuf, hist_rows, wcb_ref, ybslab)
    vbuf[0:carry_rows, :] = vbuf[TL * SUBLANES:(TL + HIST) * SUBLANES, :]

    dummy = n_prompt_seq + n_sample_seq
    for s in range(NSEG):
        prompt_slot = i // tiles_per_seq if s == NSEG - 1 else dummy
        slot = jnp.where(is_prompt, prompt_slot, n_prompt_seq + (i - n_prompt_tiles) * NSEG + s)
        r1 = (s + 1) * SEG
        sa_scr[slot] = cu[r1 - (TAPS_A - 1):r1, :]
        sb_scr[slot] = v[r1 - (TAPS_B - 1):r1, :]

    @pl.when(i == n_tiles - 1)
    def _():
        nap_ref[...] = sa_scr[0:n_prompt_seq]
        nbp_ref[...] = sb_scr[0:n_prompt_seq]
        nas_ref[...] = sa_scr[n_prompt_seq:n_prompt_seq + n_sample_seq]
        nbs_ref[...] = sb_scr[n_prompt_seq:n_prompt_seq + n_sample_seq]

    yb = _slab_to_rows(ybslab, 0, TL) + bcb_ref[...]

    out_a = jnp.dot((proj(0) * ya).astype(BF16), woa_ref[...], preferred_element_type=F32)
    mu = jnp.mean(yb, axis=-1, keepdims=True)
    yc = yb - mu
    var = jnp.mean(yc * yc, axis=-1, keepdims=True)
    ln = yc * lax.rsqrt(var + EPS) * lng_ref[...] + lnb_ref[...]
    act = ln * jax.nn.sigmoid(ln)
    out_b = jnp.dot(act.astype(BF16), wob_ref[...], preferred_element_type=F32) + bob_ref[...]
    mix = jax.nn.sigmoid(proj(5)) * out_a + jax.nn.sigmoid(proj(6)) * out_b
    h = x + jnp.dot(mix.astype(BF16), wo_ref[...], preferred_element_type=F32)
    h_ref[...] = h
    hn = _rmsnorm(h, gffn_ref[...])
    _rows_to_slab(hn_ref, 0, hn)

    logits = lax.dot_general(wrt_ref[...], hn, (((1,), (1,)), ((), ())),
                             precision=lax.Precision.HIGHEST, preferred_element_type=F32) + brt_ref[...]
    iota_e = lax.broadcasted_iota(I32, (N_EXPERTS, TL), 0).astype(F32)
    vals, idxs = [], []
    for _ in range(TOP_K):
        m = jnp.max(logits, axis=0, keepdims=True)
        sel = jnp.min(jnp.where(logits == m, iota_e, float(N_EXPERTS)), axis=0, keepdims=True)
        vals.append(m)
        idxs.append(sel)
        logits = jnp.where(iota_e == sel, -jnp.inf, logits)
    ex = [jnp.exp(val - vals[0]) for val in vals]
    denom = ex[0] + ex[1] + ex[2] + ex[3]
    for k in range(TOP_K):
        idx_ref[k:k + 1, :] = idxs[k].astype(I32)
        gw_ref[k:k + 1, :] = ex[k] / denom


def _route_kernel(n_tokens, n_blocks_pad, idx_ref, dest_ref, be_ref, nu_ref, pad_start_ref, pad_len_ref, tri_scr):
    n_chunks = n_tokens // ROUTE_CHUNK
    iota_e = lax.broadcasted_iota(I32, (N_EXPERTS, ROUTE_CHUNK), 0)

    def masks(c):
        sl = pl.ds(pl.multiple_of(c * ROUTE_CHUNK, ROUTE_CHUNK), ROUTE_CHUNK)
        return sl, [idx_ref[k:k + 1, sl] == iota_e for k in range(TOP_K)]

    def onehot_sum(ms):
        tot = ms[0].astype(F32)
        for k in range(1, TOP_K):
            tot = tot + ms[k].astype(F32)
        return tot

    def count_body(c, acc):
        _, ms = masks(c)
        return acc + onehot_sum(ms)

    acc = lax.fori_loop(0, n_chunks, count_body, jnp.zeros((N_EXPERTS, ROUTE_CHUNK), F32))
    counts = jnp.sum(acc, axis=1, keepdims=True)
    nblk = jnp.floor((counts + (ROWS - 1)) * (1.0 / ROWS))
    r = lax.broadcasted_iota(I32, (N_EXPERTS, N_EXPERTS), 0)
    cidx = lax.broadcasted_iota(I32, (N_EXPERTS, N_EXPERTS), 1)
    lower = (cidx <= r).astype(F32)
    pend = jnp.dot(lower, jnp.broadcast_to(nblk, (N_EXPERTS, LANES)),
                   precision=lax.Precision.HIGHEST, preferred_element_type=F32)
    pend1 = pend[:, 0:1]
    pstart_rows = (pend1 - nblk) * ROWS
    pad_start_ref[...] = jnp.broadcast_to(pstart_rows + counts, (N_EXPERTS, LANES)).astype(I32)
    pad_len_ref[...] = jnp.broadcast_to(nblk * ROWS - counts, (N_EXPERTS, LANES)).astype(I32)

    bvec = lax.broadcasted_iota(I32, (N_EXPERTS, n_blocks_pad), 1).astype(F32)
    be = jnp.sum((pend1 <= bvec).astype(F32), axis=0, keepdims=True)
    be_ref[...] = jnp.minimum(be, N_EXPERTS - 1).astype(I32)
    last = lax.broadcasted_iota(I32, (N_EXPERTS, LANES), 0) == N_EXPERTS - 1
    nu_ref[...] = jnp.sum(jnp.where(last, pend, 0.0), axis=0, keepdims=True).astype(I32)

    tr = lax.broadcasted_iota(I32, (ROUTE_CHUNK, ROUTE_CHUNK), 0)
    tc = lax.broadcasted_iota(I32, (ROUTE_CHUNK, ROUTE_CHUNK), 1)
    tri_scr[...] = (tr < tc).astype(BF16)

    def dest_body(c, carry):
        sl, ms = masks(c)
        tot = onehot_sum(ms)
        before = jnp.dot(tot.astype(BF16), tri_scr[...], preferred_element_type=F32)
        pos = pstart_rows + carry + before
        for k in range(TOP_K):
            dest_ref[k:k + 1, sl] = jnp.sum(jnp.where(ms[k], pos, 0.0), axis=0, keepdims=True).astype(I32)
        return carry + jnp.sum(tot, axis=1, keepdims=True)

    lax.fori_loop(0, n_chunks, dest_body, jnp.zeros((N_EXPERTS, 1), F32))


def _dispatch_kernel(n_blocks, pad_start_ref, pad_len_ref, nu_ref, dest_ref, hn_ref, xb_hbm, ztile, sem, pad_sem):
    def row_copy(tt, k):
        src = pl.multiple_of(tt * SUBLANES, SUBLANES)
        dst = pl.multiple_of(dest_ref[0, 0, k * TL + tt] * SUBLANES, SUBLANES)
        return pltpu.make_async_copy(hn_ref.at[pl.ds(src, SUBLANES), :], xb_hbm.at[pl.ds(dst, SUBLANES), :], sem)

    def body(tt, carry):
        for k in range(TOP_K):
            row_copy(tt, k).start(priority=k % 2)
        return carry

    lax.fori_loop(0, TL, body, 0)

    @pl.when(pl.program_id(0) == 0)
    def _():
        ztile[...] = jnp.zeros(ztile.shape, F32)

        def pad_copy(row):
            dst = pl.multiple_of(row * SUBLANES, SUBLANES)
            return pltpu.make_async_copy(ztile.at[pl.ds(0, SUBLANES), :], xb_hbm.at[pl.ds(dst, SUBLANES), :], pad_sem)

        def tail_copy(blk):
            dst = pl.multiple_of(blk * (ROWS * SUBLANES), ROWS * SUBLANES)
            return pltpu.make_async_copy(ztile, xb_hbm.at[pl.ds(dst, ROWS * SUBLANES), :], pad_sem)

        def run(copy_at, lo, hi):
            def issue(r, carry):
                copy_at(r).start()
                return carry

            def drain(r, carry):
                copy_at(r).wait()
                return carry

            lax.fori_loop(lo, hi, issue, 0)
            lax.fori_loop(lo, hi, drain, 0)

        for e in range(N_EXPERTS):
            start = pad_start_ref[e]
            run(pad_copy, start, start + pad_len_ref[e])
        run(tail_copy, nu_ref[0], n_blocks)

    n = TL * SUBLANES
    for k in range(TOP_K):
        pltpu.make_async_copy(hn_ref, xb_hbm.at[pl.ds(0, n), :], sem).wait()


def _expert_kernel(be_ref, nu_ref, xs_ref, wgu_hbm, bgu_ref, wd_hbm, bd_ref, o_ref,
                   wgu_f32, wd_f32, wgu_bf, wd_bf, slot_ref, sems):
    b = pl.program_id(0)
    n_used = nu_ref[0]
    e = be_ref[b]
    prev = be_ref[jnp.maximum(b - 1, 0)]
    active = b < n_used

    def weight_copies(expert, slot):
        return (pltpu.make_async_copy(wgu_hbm.at[expert], wgu_f32.at[slot], sems.at[0, slot]),
                pltpu.make_async_copy(wd_hbm.at[expert], wd_f32.at[slot], sems.at[1, slot]))

    @pl.when(active & (b == 0))
    def _():
        slot_ref[0] = 0
        for c in weight_copies(e, 0):
            c.start()

    @pl.when(active & ((b == 0) | (e != prev)))
    def _():
        slot = slot_ref[0]
        for c in weight_copies(e, slot):
            c.wait()
        last = n_used - 1
        nxt = lax.while_loop(lambda j: (j <= last) & (be_ref[jnp.minimum(j, last)] == e), lambda j: j + 1, b + 1)

        @pl.when(nxt <= last)
        def _():
            for c in weight_copies(be_ref[jnp.minimum(nxt, last)], 1 - slot):
                c.start()

        wgu_bf[...] = wgu_f32[slot].astype(BF16)
        wd_bf[...] = wd_f32[slot].astype(BF16)
        slot_ref[0] = 1 - slot

    @pl.when(active)
    def _():
        xb = _slab_to_rows(xs_ref, 0, ROWS).astype(BF16)
        gu = jnp.dot(xb, wgu_bf[...], preferred_element_type=F32) + bgu_ref[0]
        gate = jnp.minimum(gu[:, :D], SWIGLU_LIMIT)
        up = jnp.clip(gu[:, D:], -SWIGLU_LIMIT, SWIGLU_LIMIT)
        hidden = (up + 1.0) * gate * jax.nn.sigmoid(SWIGLU_ALPHA * gate)
        out = jnp.dot(hidden.astype(BF16), wd_bf[...], preferred_element_type=F32) + bd_ref[0]
        _rows_to_slab(o_ref, 0, out)

    @pl.when(jnp.logical_not(active))
    def _():
        o_ref[...] = jnp.zeros(o_ref.shape, F32)


def _combine_kernel(n_tiles, n_prompt_tiles, dest_ref, dest_next_ref, h_ref, gw_ref, gfin_ref, ob_hbm,
                    yp_ref, ys_ref, gath0, gath1, sems):
    i = pl.program_id(0)
    n = TOP_K * TL * SUBLANES

    def issue(table_ref, gath, sem):
        def body(tt, carry):
            for k in range(TOP_K):
                src = pl.multiple_of(table_ref[0, 0, k * TL + tt] * SUBLANES, SUBLANES)
                dst = pl.multiple_of((k * TL + tt) * SUBLANES, SUBLANES)
                pltpu.make_async_copy(ob_hbm.at[pl.ds(src, SUBLANES), :], gath.at[pl.ds(dst, SUBLANES), :],
                                      sem).start(priority=k % 2)
            return carry

        lax.fori_loop(0, TL, body, 0)

    def step(cur, cur_sem, nxt, nxt_sem):
        @pl.when(i + 1 < n_tiles)
        def _():
            issue(dest_next_ref, nxt, nxt_sem)

        pltpu.make_async_copy(ob_hbm.at[pl.ds(0, n), :], cur, cur_sem).wait()
        acc = h_ref[...]
        gw = gw_ref[...]
        for k in range(TOP_K):
            acc = acc + gw[:, k:k + 1] * _slab_to_rows(cur, k * TL, TL)
        y = _rmsnorm(acc, gfin_ref[...])
        is_prompt = i < n_prompt_tiles

        @pl.when(is_prompt)
        def _():
            yp_ref[...] = y

        @pl.when(jnp.logical_not(is_prompt))
        def _():
            ys_ref[...] = y

    @pl.when(i == 0)
    def _():
        issue(dest_ref, gath0, sems.at[0])

    parity = lax.rem(i, 2)

    @pl.when(parity == 0)
    def _():
        step(gath0, sems.at[0], gath1, sems.at[1])

    @pl.when(parity == 1)
    def _():
        step(gath1, sems.at[1], gath0, sems.at[0])


def _const_spec(shape):
    nd = len(shape)
    return pl.BlockSpec(shape, lambda *_: (0,) * nd, pipeline_mode=pl.Buffered(1))


def kernel(x_prompt, x_sample, state_conv_a, state_conv_b, g_mix, w_in, b_in, w_conv_a, w_out_a, w_conv_b, b_conv_b, ln_g, ln_b, w_out_b, b_out_b, w_o, g_ffn, w_router, b_router, w_gate_up, b_gate_up, w_down, b_down, g_final):
    depth = g_mix.shape[0]
    assert depth == 1
    bp, lp, d = x_prompt.shape
    bs, ls, _ = x_sample.shape
    assert d == D and w_in.shape[2] == 7 * D
    tp, ts = bp * lp, bs * ls
    t = tp + ts
    assert lp % TL == 0 and ls == SEG and ts % TL == 0 and t % ROUTE_CHUNK == 0
    assert TAPS_B - 1 <= HIST <= SEG and TAPS_A - 1 <= HIST_A
    npt, nst = tp // TL, ts // TL
    nt = npt + nst
    tiles_per_seq = lp // TL
    n_blocks = (t * TOP_K) // ROWS + N_EXPERTS
    n_blocks_pad = -(-n_blocks // LANES) * LANES
    p_rows = n_blocks * ROWS

    row = lambda a: a.reshape(1, -1)
    f32_spec = lambda: _const_spec((1, D))

    mixer = pl.pallas_call(
        functools.partial(_mixer_kernel, npt, tiles_per_seq, bp, bs),
        grid=(nt,),
        in_specs=[
            pl.BlockSpec((TL, D), lambda i: (jnp.minimum(i, npt - 1), 0)),
            pl.BlockSpec((TL, D), lambda i: (jnp.maximum(i - npt, 0), 0)),
            pl.BlockSpec((NSEG, TAPS_A - 1, D), lambda i: (jnp.maximum(i - npt, 0), 0, 0)),
            pl.BlockSpec((NSEG, TAPS_B - 1, D), lambda i: (jnp.maximum(i - npt, 0), 0, 0)),
            f32_spec(),
            _const_spec((D, 7 * D)),
            _const_spec((1, 7 * D)),
            _const_spec((TAPS_A, D)),
            _const_spec((D, D)),
            _const_spec((TAPS_B, SUBLANES, LANES)),
            f32_spec(), f32_spec(), f32_spec(),
            _const_spec((D, D)),
            f32_spec(),
            _const_spec((D, D)),
            f32_spec(),
            _const_spec((N_EXPERTS, D)),
            _const_spec((N_EXPERTS, 1)),
        ],
        out_specs=[
            pl.BlockSpec((TL, D), lambda i: (i, 0)),
            pl.BlockSpec((TL * SUBLANES, LANES), lambda i: (i, 0)),
            pl.BlockSpec((TOP_K, TL), lambda i: (0, i)),
            pl.BlockSpec((TOP_K, TL), lambda i: (0, i)),
            pl.BlockSpec((bp, TAPS_A - 1, D), lambda i: (0, 0, 0)),
            pl.BlockSpec((bp, TAPS_B - 1, D), lambda i: (0, 0, 0)),
            pl.BlockSpec((bs, TAPS_A - 1, D), lambda i: (0, 0, 0)),
            pl.BlockSpec((bs, TAPS_B - 1, D), lambda i: (0, 0, 0)),
        ],
        out_shape=[
            jax.ShapeDtypeStruct((t, D), F32),
            jax.ShapeDtypeStruct((t * SUBLANES, LANES), F32),
            jax.ShapeDtypeStruct((TOP_K, t), I32),
            jax.ShapeDtypeStruct((TOP_K, t), F32),
            jax.ShapeDtypeStruct((bp, TAPS_A - 1, D), F32),
            jax.ShapeDtypeStruct((bp, TAPS_B - 1, D), F32),
            jax.ShapeDtypeStruct((bs, TAPS_A - 1, D), F32),
            jax.ShapeDtypeStruct((bs, TAPS_B - 1, D), F32),
        ],
        scratch_shapes=[
            pltpu.VMEM((NSEG * (HIST_A + SEG), D), F32),
            pltpu.VMEM((HIST_A, D), F32),
            pltpu.VMEM(((HIST + TL + NSEG * HIST) * SUBLANES, LANES), F32),
            pltpu.VMEM((TL * SUBLANES, LANES), F32),
            pltpu.VMEM((bp + bs + 1, TAPS_A - 1, D), F32),
            pltpu.VMEM((bp + bs + 1, TAPS_B - 1, D), F32),
        ],
        compiler_params=pltpu.CompilerParams(dimension_semantics=("arbitrary",), vmem_limit_bytes=VMEM_LIMIT),
        name="mixer",
    )
    h, hn_slab, idx_t, gw_t, na_p, nb_p, na_s, nb_s = mixer(
        x_prompt.reshape(tp, D), x_sample.reshape(ts, D), state_conv_a[0], state_conv_b[0],
        row(g_mix), w_in[0].astype(BF16), row(b_in), w_conv_a[0],
        w_out_a[0].astype(BF16), w_conv_b[0].reshape(TAPS_B, SUBLANES, LANES), row(b_conv_b), row(ln_g), row(ln_b),
        w_out_b[0].astype(BF16), row(b_out_b), w_o[0].astype(BF16), row(g_ffn),
        w_router[0].T, b_router[0].reshape(N_EXPERTS, 1))

    dest_t, block_e, n_used, pad_start, pad_len = pl.pallas_call(
        functools.partial(_route_kernel, t, n_blocks_pad),
        out_shape=[
            jax.ShapeDtypeStruct((TOP_K, t), I32),
            jax.ShapeDtypeStruct((1, n_blocks_pad), I32),
            jax.ShapeDtypeStruct((1, LANES), I32),
            jax.ShapeDtypeStruct((N_EXPERTS, LANES), I32),
            jax.ShapeDtypeStruct((N_EXPERTS, LANES), I32),
        ],
        scratch_shapes=[pltpu.VMEM((ROUTE_CHUNK, ROUTE_CHUNK), BF16)],
        compiler_params=pltpu.CompilerParams(vmem_limit_bytes=VMEM_LIMIT),
        name="route",
    )(idx_t)
    dest_sm = dest_t.reshape(TOP_K, nt, TL).transpose(1, 0, 2).reshape(nt, 1, TOP_K * TL)
    block_e = block_e[0, :n_blocks]
    n_used = n_used[0, :1]
    pad_start, pad_len = pad_start[:, 0], pad_len[:, 0]

    smem_tile_spec = pl.BlockSpec((1, 1, TOP_K * TL), lambda i: (i, 0, 0), memory_space=pltpu.SMEM)
    any_spec = pl.BlockSpec(memory_space=pl.ANY)
    x_buf = pl.pallas_call(
        functools.partial(_dispatch_kernel, n_blocks),
        grid_spec=pltpu.PrefetchScalarGridSpec(
            num_scalar_prefetch=3,
            grid=(nt,),
            in_specs=[
                pl.BlockSpec((1, 1, TOP_K * TL), lambda i, *_: (i, 0, 0), memory_space=pltpu.SMEM),
                pl.BlockSpec((TL * SUBLANES, LANES), lambda i, *_: (i, 0)),
            ],
            out_specs=any_spec,
            scratch_shapes=[pltpu.VMEM((ROWS * SUBLANES, LANES), F32), pltpu.SemaphoreType.DMA,
                            pltpu.SemaphoreType.DMA],
        ),
        out_shape=jax.ShapeDtypeStruct((p_rows * SUBLANES, LANES), F32),
        compiler_params=pltpu.CompilerParams(dimension_semantics=("arbitrary",)),
        name="dispatch",
    )(pad_start, pad_len, n_used, dest_sm, hn_slab)

    def blk(b, be, nu):
        return jnp.maximum(jnp.minimum(b, nu[0] - 1), 0)

    out_buf = pl.pallas_call(
        _expert_kernel,
        grid_spec=pltpu.PrefetchScalarGridSpec(
            num_scalar_prefetch=2,
            grid=(n_blocks,),
            in_specs=[
                pl.BlockSpec((ROWS * SUBLANES, LANES), lambda b, be, nu: (blk(b, be, nu), 0)),
                any_spec,
                pl.BlockSpec((1, 1, 2 * D), lambda b, be, nu: (be[blk(b, be, nu)], 0, 0)),
                any_spec,
                pl.BlockSpec((1, 1, D), lambda b, be, nu: (be[blk(b, be, nu)], 0, 0)),
            ],
            out_specs=pl.BlockSpec((ROWS * SUBLANES, LANES), lambda b, be, nu: (b, 0)),
            scratch_shapes=[
                pltpu.VMEM((2, D, 2 * D), F32), pltpu.VMEM((2, D, D), F32),
                pltpu.VMEM((D, 2 * D), BF16), pltpu.VMEM((D, D), BF16),
                pltpu.SMEM((1,), I32), pltpu.SemaphoreType.DMA((2, 2)),
            ],
        ),
        out_shape=jax.ShapeDtypeStruct((p_rows * SUBLANES, LANES), F32),
        compiler_params=pltpu.CompilerParams(dimension_semantics=("arbitrary",), vmem_limit_bytes=VMEM_LIMIT),
        name="experts",
    )(block_e, n_used, x_buf, w_gate_up[0], b_gate_up[0].reshape(N_EXPERTS, 1, 2 * D), w_down[0],
      b_down[0].reshape(N_EXPERTS, 1, D))

    y_p, y_s = pl.pallas_call(
        functools.partial(_combine_kernel, nt, npt),
        grid=(nt,),
        in_specs=[
            smem_tile_spec,
            pl.BlockSpec((1, 1, TOP_K * TL), lambda i: (jnp.minimum(i + 1, nt - 1), 0, 0), memory_space=pltpu.SMEM),
            pl.BlockSpec((TL, D), lambda i: (i, 0)),
            pl.BlockSpec((TL, TOP_K), lambda i: (i, 0)),
            pl.BlockSpec((1, D), lambda i: (0, 0)),
            any_spec,
        ],
        out_specs=[
            pl.BlockSpec((TL, D), lambda i: (jnp.minimum(i, npt - 1), 0)),
            pl.BlockSpec((TL, D), lambda i: (jnp.maximum(i - npt, 0), 0)),
        ],
        out_shape=[jax.ShapeDtypeStruct((tp, D), F32), jax.ShapeDtypeStruct((ts, D), F32)],
        scratch_shapes=[pltpu.VMEM((TOP_K * TL * SUBLANES, LANES), F32), pltpu.VMEM((TOP_K * TL * SUBLANES, LANES), F32),
                        pltpu.SemaphoreType.DMA((2,))],
        compiler_params=pltpu.CompilerParams(dimension_semantics=("arbitrary",), vmem_limit_bytes=VMEM_LIMIT),
        name="combine",
    )(dest_sm, dest_sm, h, gw_t.T, row(g_final), out_buf)

    return (y_p.reshape(bp, lp, D), y_s.reshape(bs, ls, D), na_p[None], nb_p[None], na_s[None], nb_s[None])
```

```python
import functools

import jax
import jax.numpy as jnp
from jax import lax
from jax.experimental import pallas as pl
from jax.experimental.pallas import tpu as pltpu

F32 = jnp.float32
BF16 = jnp.bfloat16
I32 = jnp.int32

D = 1024
LANES = 128
SUBLANES = 8
NCHUNK = D // LANES
N_EXPERTS = 32
TOP_K = 4
TAPS_A = 3
TAPS_B = 31
EPS = 1e-5
SWIGLU_LIMIT = 7.0
SWIGLU_ALPHA = 1.702

TL = 256
SEG = 32
NSEG = TL // SEG
HIST = 32
HIST_A = 8
CONV_BLOCK = 4
ROWS = 256
ROUTE_CHUNK = 512
VMEM_LIMIT = 56 * 1024 * 1024


def _rows_to_slab(dst_ref, slot0, val):
    n = val.shape[0]
    for c in range(NCHUNK):
        dst_ref[pl.ds(SUBLANES * slot0 + c, n, stride=SUBLANES), :] = val[:, c * LANES:(c + 1) * LANES]


def _slab_to_rows(src_ref, slot0, n):
    return jnp.concatenate(
        [src_ref[pl.ds(SUBLANES * slot0 + c, n, stride=SUBLANES), :] for c in range(NCHUNK)], axis=1)


def _rmsnorm(x, g):
    return x * lax.rsqrt(jnp.mean(x * x, axis=-1, keepdims=True) + EPS) * g


def _conv_b_slab(vbuf, hist_rows, w_ref, dst_ref):
    first_tap = HIST - (TAPS_B - 1)
    for s in range(NSEG):
        cur_row0 = (HIST + s * SEG) * SUBLANES
        for t0 in range(0, SEG, CONV_BLOCK):
            ins = []
            for k in range(CONV_BLOCK + TAPS_B - 1):
                u = t0 + first_tap + k
                if u < HIST:
                    row = pl.multiple_of(hist_rows[s] + u * SUBLANES, SUBLANES)
                    ins.append(vbuf[pl.ds(row, SUBLANES), :])
                else:
                    row = cur_row0 + (u - HIST) * SUBLANES
                    ins.append(vbuf[row:row + SUBLANES, :])
            accs = [None] * CONV_BLOCK
            for j in range(TAPS_B):
                wj = w_ref[j]
                for o in range(CONV_BLOCK):
                    term = wj * ins[o + j]
                    accs[o] = term if accs[o] is None else accs[o] + term
            for o in range(CONV_BLOCK):
                row = (s * SEG + t0 + o) * SUBLANES
                dst_ref[row:row + SUBLANES, :] = accs[o]


def _mixer_kernel(n_prompt_tiles, tiles_per_seq, n_prompt_seq, n_sample_seq,
                  xp_ref, xs_ref, sta_ref, stb_ref, gmix_ref, win_ref, bin_ref, wca_ref, woa_ref,
                  wcb_ref, bcb_ref, lng_ref, lnb_ref, wob_ref, bob_ref, wo_ref, gffn_ref, wrt_ref, brt_ref,
                  h_ref, hn_ref, idx_ref, gw_ref, nap_ref, nbp_ref, nas_ref, nbs_ref,
                  cubuf, carry_a, vbuf, ybslab, sa_scr, sb_scr):
    i = pl.program_id(0)
    n_tiles = pl.num_programs(0)
    is_prompt = i < n_prompt_tiles
    seq_start = is_prompt & (lax.rem(i, tiles_per_seq) == 0)
    carry_rows = HIST * SUBLANES

    @pl.when(i == 0)
    def _():
        vbuf[0:carry_rows, :] = jnp.zeros((carry_rows, LANES), F32)
        carry_a[...] = jnp.zeros(carry_a.shape, F32)

    x = jnp.where(is_prompt, xp_ref[...], xs_ref[...])
    n_bf = _rmsnorm(x, gmix_ref[...]).astype(BF16)

    def proj(g):
        cols = slice(g * D, (g + 1) * D)
        return jnp.dot(n_bf, win_ref[:, cols], preferred_element_type=F32) + bin_ref[:, cols]

    cu = proj(1) * proj(2)
    v = proj(3) * jax.nn.sigmoid(proj(4))

    seg_rows = HIST_A + SEG
    prev_a = jnp.where(seq_start, 0.0, carry_a[HIST_A - (TAPS_A - 1):HIST_A, :])
    for s in range(NSEG):
        r0 = s * SEG
        prev = prev_a if s == 0 else cu[r0 - (TAPS_A - 1):r0, :]
        cubuf[s * seg_rows + HIST_A - (TAPS_A - 1):s * seg_rows + HIST_A, :] = jnp.where(is_prompt, prev, sta_ref[s])
        cubuf[s * seg_rows + HIST_A:(s + 1) * seg_rows, :] = cu[r0:r0 + SEG, :]
    carry_a[HIST_A - (TAPS_A - 1):HIST_A, :] = cu[TL - (TAPS_A - 1):TL, :]
    ya_parts = []
    for s in range(NSEG):
        acc = None
        for j in range(TAPS_A):
            lo = s * seg_rows + HIST_A - (TAPS_A - 1) + j
            term = wca_ref[j:j + 1, :] * cubuf[lo:lo + SEG, :]
            acc = term if acc is None else acc + term
        ya_parts.append(acc)
    ya = jnp.concatenate(ya_parts, axis=0)

    vbuf[0:carry_rows, :] = jnp.where(seq_start, 0.0, vbuf[0:carry_rows, :])
    _rows_to_slab(vbuf, HIST, v)
    state_slot0 = HIST + TL
    for s in range(NSEG):
        for c in range(NCHUNK):
            vbuf[pl.ds(SUBLANES * (state_slot0 + s * HIST + HIST - (TAPS_B - 1)) + c, TAPS_B - 1, stride=SUBLANES), :] = (
                stb_ref[s, :, c * LANES:(c + 1) * LANES])
    hist_rows = [jnp.where(is_prompt, s * SEG * SUBLANES, (state_slot0 + s * HIST) * SUBLANES) for s in range(NSEG)]
    _conv_b_slab(vbuf, hist_rows, wcb_ref, ybslab)
    vbuf[0:carry_rows, :] = vbuf[TL * SUBLANES:(TL + HIST) * SUBLANES, :]

    dummy = n_prompt_seq + n_sample_seq
    for s in range(NSEG):
        prompt_slot = i // tiles_per_seq if s == NSEG - 1 else dummy
        slot = jnp.where(is_prompt, prompt_slot, n_prompt_seq + (i - n_prompt_tiles) * NSEG + s)
        r1 = (s + 1) * SEG
        sa_scr[slot] = cu[r1 - (TAPS_A - 1):r1, :]
        sb_scr[slot] = v[r1 - (TAPS_B - 1):r1, :]

    @pl.when(i == n_tiles - 1)
    def _():
        nap_ref[...] = sa_scr[0:n_prompt_seq]
        nbp_ref[...] = sb_scr[0:n_prompt_seq]
        nas_ref[...] = sa_scr[n_prompt_seq:n_prompt_seq + n_sample_seq]
        nbs_ref[...] = sb_scr[n_prompt_seq:n_prompt_seq + n_sample_seq]

    yb = _slab_to_rows(ybslab, 0, TL) + bcb_ref[...]

    out_a = jnp.dot((proj(0) * ya).astype(BF16), woa_ref[...], preferred_element_type=F32)
    mu = jnp.mean(yb, axis=-1, keepdims=True)
    yc = yb - mu
    var = jnp.mean(yc * yc, axis=-1, keepdims=True)
    ln = yc * lax.rsqrt(var + EPS) * lng_ref[...] + lnb_ref[...]
    act = ln * jax.nn.sigmoid(ln)
    out_b = jnp.dot(act.astype(BF16), wob_ref[...], preferred_element_type=F32) + bob_ref[...]
    mix = jax.nn.sigmoid(proj(5)) * out_a + jax.nn.sigmoid(proj(6)) * out_b
    h = x + jnp.dot(mix.astype(BF16), wo_ref[...], preferred_element_type=F32)
    h_ref[...] = h
    hn = _rmsnorm(h, gffn_ref[...])
    _rows_to_slab(hn_ref, 0, hn)

    logits = lax.dot_general(wrt_ref[...], hn, (((1,), (1,)), ((), ())),
                             precision=lax.Precision.HIGHEST, preferred_element_type=F32) + brt_ref[...]
    iota_e = lax.broadcasted_iota(I32, (N_EXPERTS, TL), 0).astype(F32)
    vals, idxs = [], []
    for _ in range(TOP_K):
        m = jnp.max(logits, axis=0, keepdims=True)
        sel = jnp.min(jnp.where(logits == m, iota_e, float(N_EXPERTS)), axis=0, keepdims=True)
        vals.append(m)
        idxs.append(sel)
        logits = jnp.where(iota_e == sel, -jnp.inf, logits)
    ex = [jnp.exp(val - vals[0]) for val in vals]
    denom = ex[0] + ex[1] + ex[2] + ex[3]
    for k in range(TOP_K):
        idx_ref[k:k + 1, :] = idxs[k].astype(I32)
        gw_ref[k:k + 1, :] = ex[k] / denom


def _route_kernel(n_tokens, n_blocks_pad, idx_ref, dest_ref, be_ref, nu_ref, pad_start_ref, pad_len_ref, tri_scr):
    n_chunks = n_tokens // ROUTE_CHUNK
    iota_e = lax.broadcasted_iota(I32, (N_EXPERTS, ROUTE_CHUNK), 0)

    def masks(c):
        sl = pl.ds(pl.multiple_of(c * ROUTE_CHUNK, ROUTE_CHUNK), ROUTE_CHUNK)
        return sl, [idx_ref[k:k + 1, sl] == iota_e for k in range(TOP_K)]

    def onehot_sum(ms):
        tot = ms[0].astype(F32)
        for k in range(1, TOP_K):
            tot = tot + ms[k].astype(F32)
        return tot

    def count_body(c, acc):
        _, ms = masks(c)
        return acc + onehot_sum(ms)

    acc = lax.fori_loop(0, n_chunks, count_body, jnp.zeros((N_EXPERTS, ROUTE_CHUNK), F32))
    counts = jnp.sum(acc, axis=1, keepdims=True)
    nblk = jnp.floor((counts + (ROWS - 1)) * (1.0 / ROWS))
    r = lax.broadcasted_iota(I32, (N_EXPERTS, N_EXPERTS), 0)
    cidx = lax.broadcasted_iota(I32, (N_EXPERTS, N_EXPERTS), 1)
    lower = (cidx <= r).astype(F32)
    pend = jnp.dot(lower, jnp.broadcast_to(nblk, (N_EXPERTS, LANES)),
                   precision=lax.Precision.HIGHEST, preferred_element_type=F32)
    pend1 = pend[:, 0:1]
    pstart_rows = (pend1 - nblk) * ROWS
    pad_start_ref[...] = jnp.broadcast_to(pstart_rows + counts, (N_EXPERTS, LANES)).astype(I32)
    pad_len_ref[...] = jnp.broadcast_to(nblk * ROWS - counts, (N_EXPERTS, LANES)).astype(I32)

    bvec = lax.broadcasted_iota(I32, (N_EXPERTS, n_blocks_pad), 1).astype(F32)
    be = jnp.sum((pend1 <= bvec).astype(F32), axis=0, keepdims=True)
    be_ref[...] = jnp.minimum(be, N_EXPERTS - 1).astype(I32)
    last = lax.broadcasted_iota(I32, (N_EXPERTS, LANES), 0) == N_EXPERTS - 1
    nu_ref[...] = jnp.sum(jnp.where(last, pend, 0.0), axis=0, keepdims=True).astype(I32)

    tr = lax.broadcasted_iota(I32, (ROUTE_CHUNK, ROUTE_CHUNK), 0)
    tc = lax.broadcasted_iota(I32, (ROUTE_CHUNK, ROUTE_CHUNK), 1)
    tri_scr[...] = (tr < tc).astype(BF16)

    def dest_body(c, carry):
        sl, ms = masks(c)
        tot = onehot_sum(ms)
        before = jnp.dot(tot.astype(BF16), tri_scr[...], preferred_element_type=F32)
        pos = pstart_rows + carry + before
        for k in range(TOP_K):
            dest_ref[k:k + 1, sl] = jnp.sum(jnp.where(ms[k], pos, 0.0), axis=0, keepdims=True).astype(I32)
        return carry + jnp.sum(tot, axis=1, keepdims=True)

    lax.fori_loop(0, n_chunks, dest_body, jnp.zeros((N_EXPERTS, 1), F32))


def _dispatch_kernel(n_blocks, pad_start_ref, pad_len_ref, nu_ref, dest_ref, hn_ref, xb_hbm, ztile, sem, pad_sem):
    def row_copy(tt, k):
        src = pl.multiple_of(tt * SUBLANES, SUBLANES)
        dst = pl.multiple_of(dest_ref[0, 0, k * TL + tt] * SUBLANES, SUBLANES)
        return pltpu.make_async_copy(hn_ref.at[pl.ds(src, SUBLANES), :], xb_hbm.at[pl.ds(dst, SUBLANES), :], sem)

    def body(tt, carry):
        for k in range(TOP_K):
            row_copy(tt, k).start(priority=k % 2)
        return carry

    lax.fori_loop(0, TL, body, 0)

    @pl.when(pl.program_id(0) == 0)
    def _():
        ztile[...] = jnp.zeros(ztile.shape, F32)

        def pad_copy(row):
            dst = pl.multiple_of(row * SUBLANES, SUBLANES)
            return pltpu.make_async_copy(ztile.at[pl.ds(0, SUBLANES), :], xb_hbm.at[pl.ds(dst, SUBLANES), :], pad_sem)

        def tail_copy(blk):
            dst = pl.multiple_of(blk * (ROWS * SUBLANES), ROWS * SUBLANES)
            return pltpu.make_async_copy(ztile, xb_hbm.at[pl.ds(dst, ROWS * SUBLANES), :], pad_sem)

        def run(copy_at, lo, hi):
            def issue(r, carry):
                copy_at(r).start()
                return carry

            def drain(r, carry):
                copy_at(r).wait()
                return carry

            lax.fori_loop(lo, hi, issue, 0)
            lax.fori_loop(lo, hi, drain, 0)

        for e in range(N_EXPERTS):
            start = pad_start_ref[e]
            run(pad_copy, start, start + pad_len_ref[e])
        run(tail_copy, nu_ref[0], n_blocks)

    n = TL * SUBLANES
    for k in range(TOP_K):
        pltpu.make_async_copy(hn_ref, xb_hbm.at[pl.ds(0, n), :], sem).wait()


def _expert_kernel(be_ref, nu_ref, xs_ref, wgu_hbm, bgu_ref, wd_hbm, bd_ref, o_ref,
                   wgu_f32, wd_f32, wgu_bf, wd_bf, slot_ref, sems):
    b = pl.program_id(0)
    n_used = nu_ref[0]
    e = be_ref[b]
    prev = be_ref[jnp.maximum(b - 1, 0)]
    active = b < n_used

    def weight_copies(expert, slot):
        return (pltpu.make_async_copy(wgu_hbm.at[expert], wgu_f32.at[slot], sems.at[0, slot]),
                pltpu.make_async_copy(wd_hbm.at[expert], wd_f32.at[slot], sems.at[1, slot]))

    @pl.when(active & (b == 0))
    def _():
        slot_ref[0] = 0
        for c in weight_copies(e, 0):
            c.start()

    @pl.when(active & ((b == 0) | (e != prev)))
    def _():
        slot = slot_ref[0]
        for c in weight_copies(e, slot):
            c.wait()
        last = n_used - 1
        nxt = lax.while_loop(lambda j: (j <= last) & (be_ref[jnp.minimum(j, last)] == e), lambda j: j + 1, b + 1)

        @pl.when(nxt <= last)
        def _():
            for c in weight_copies(be_ref[jnp.minimum(nxt, last)], 1 - slot):
                c.start(priority=1)

        wgu_bf[...] = wgu_f32[slot].astype(BF16)
        wd_bf[...] = wd_f32[slot].astype(BF16)
        slot_ref[0] = 1 - slot

    @pl.when(active)
    def _():
        xb = _slab_to_rows(xs_ref, 0, ROWS).astype(BF16)
        gu = jnp.dot(xb, wgu_bf[...], preferred_element_type=F32) + bgu_ref[0]
        gate = jnp.minimum(gu[:, :D], SWIGLU_LIMIT)
        up = jnp.clip(gu[:, D:], -SWIGLU_LIMIT, SWIGLU_LIMIT)
        hidden = (up + 1.0) * gate * jax.nn.sigmoid(SWIGLU_ALPHA * gate)
        out = jnp.dot(hidden.astype(BF16), wd_bf[...], preferred_element_type=F32) + bd_ref[0]
        _rows_to_slab(o_ref, 0, out)

    @pl.when(jnp.logical_not(active))
    def _():
        o_ref[...] = jnp.zeros(o_ref.shape, F32)


def _combine_kernel(n_tiles, n_prompt_tiles, dest_ref, dest_next_ref, h_ref, gw_ref, gfin_ref, ob_hbm,
                    yp_ref, ys_ref, gath0, gath1, sems):
    i = pl.program_id(0)
    n = TOP_K * TL * SUBLANES

    def issue(table_ref, gath, sem):
        def body(tt, carry):
            for k in range(TOP_K):
                src = pl.multiple_of(table_ref[0, 0, k * TL + tt] * SUBLANES, SUBLANES)
                dst = pl.multiple_of((k * TL + tt) * SUBLANES, SUBLANES)
                pltpu.make_async_copy(ob_hbm.at[pl.ds(src, SUBLANES), :], gath.at[pl.ds(dst, SUBLANES), :],
                                      sem).start(priority=k % 2)
            return carry

        lax.fori_loop(0, TL, body, 0)

    def step(cur, cur_sem, nxt, nxt_sem):
        @pl.when(i + 1 < n_tiles)
        def _():
            issue(dest_next_ref, nxt, nxt_sem)

        pltpu.make_async_copy(ob_hbm.at[pl.ds(0, n), :], cur, cur_sem).wait()
        acc = h_ref[...]
        gw = gw_ref[...]
        for k in range(TOP_K):
            acc = acc + gw[:, k:k + 1] * _slab_to_rows(cur, k * TL, TL)
        y = _rmsnorm(acc, gfin_ref[...])
        is_prompt = i < n_prompt_tiles

        @pl.when(is_prompt)
        def _():
            yp_ref[...] = y

        @pl.when(jnp.logical_not(is_prompt))
        def _():
            ys_ref[...] = y

    @pl.when(i == 0)
    def _():
        issue(dest_ref, gath0, sems.at[0])

    parity = lax.rem(i, 2)

    @pl.when(parity == 0)
    def _():
        step(gath0, sems.at[0], gath1, sems.at[1])

    @pl.when(parity == 1)
    def _():
        step(gath1, sems.at[1], gath0, sems.at[0])


def _const_spec(shape):
    nd = len(shape)
    return pl.BlockSpec(shape, lambda *_: (0,) * nd, pipeline_mode=pl.Buffered(1))


def kernel(x_prompt, x_sample, state_conv_a, state_conv_b, g_mix, w_in, b_in, w_conv_a, w_out_a, w_conv_b, b_conv_b, ln_g, ln_b, w_out_b, b_out_b, w_o, g_ffn, w_router, b_router, w_gate_up, b_gate_up, w_down, b_down, g_final):
    depth = g_mix.shape[0]
    assert depth == 1
    bp, lp, d = x_prompt.shape
    bs, ls, _ = x_sample.shape
    assert d == D and w_in.shape[2] == 7 * D
    tp, ts = bp * lp, bs * ls
    t = tp + ts
    assert lp % TL == 0 and ls == SEG and ts % TL == 0 and t % ROUTE_CHUNK == 0
    assert TAPS_B - 1 <= HIST <= SEG and TAPS_A - 1 <= HIST_A
    npt, nst = tp // TL, ts // TL
    nt = npt + nst
    tiles_per_seq = lp // TL
    n_blocks = (t * TOP_K) // ROWS + N_EXPERTS
    n_blocks_pad = -(-n_blocks // LANES) * LANES
    p_rows = n_blocks * ROWS

    row = lambda a: a.reshape(1, -1)
    f32_spec = lambda: _const_spec((1, D))

    mixer = pl.pallas_call(
        functools.partial(_mixer_kernel, npt, tiles_per_seq, bp, bs),
        grid=(nt,),
        in_specs=[
            pl.BlockSpec((TL, D), lambda i: (jnp.minimum(i, npt - 1), 0)),
            pl.BlockSpec((TL, D), lambda i: (jnp.maximum(i - npt, 0), 0)),
            pl.BlockSpec((NSEG, TAPS_A - 1, D), lambda i: (jnp.maximum(i - npt, 0), 0, 0)),
            pl.BlockSpec((NSEG, TAPS_B - 1, D), lambda i: (jnp.maximum(i - npt, 0), 0, 0)),
            f32_spec(),
            _const_spec((D, 7 * D)),
            _const_spec((1, 7 * D)),
            _const_spec((TAPS_A, D)),
            _const_spec((D, D)),
            _const_spec((TAPS_B, SUBLANES, LANES)),
            f32_spec(), f32_spec(), f32_spec(),
            _const_spec((D, D)),
            f32_spec(),
            _const_spec((D, D)),
            f32_spec(),
            _const_spec((N_EXPERTS, D)),
            _const_spec((N_EXPERTS, 1)),
        ],
        out_specs=[
            pl.BlockSpec((TL, D), lambda i: (i, 0)),
            pl.BlockSpec((TL * SUBLANES, LANES), lambda i: (i, 0)),
            pl.BlockSpec((TOP_K, TL), lambda i: (0, i)),
            pl.BlockSpec((TOP_K, TL), lambda i: (0, i)),
            pl.BlockSpec((bp, TAPS_A - 1, D), lambda i: (0, 0, 0)),
            pl.BlockSpec((bp, TAPS_B - 1, D), lambda i: (0, 0, 0)),
            pl.BlockSpec((bs, TAPS_A - 1, D), lambda i: (0, 0, 0)),
            pl.BlockSpec((bs, TAPS_B - 1, D), lambda i: (0, 0, 0)),
        ],
        out_shape=[
            jax.ShapeDtypeStruct((t, D), F32),
            jax.ShapeDtypeStruct((t * SUBLANES, LANES), F32),
            jax.ShapeDtypeStruct((TOP_K, t), I32),
            jax.ShapeDtypeStruct((TOP_K, t), F32),
            jax.ShapeDtypeStruct((bp, TAPS_A - 1, D), F32),
            jax.ShapeDtypeStruct((bp, TAPS_B - 1, D), F32),
            jax.ShapeDtypeStruct((bs, TAPS_A - 1, D), F32),
            jax.ShapeDtypeStruct((bs, TAPS_B - 1, D), F32),
        ],
        scratch_shapes=[
            pltpu.VMEM((NSEG * (HIST_A + SEG), D), F32),
            pltpu.VMEM((HIST_A, D), F32),
            pltpu.VMEM(((HIST + TL + NSEG * HIST) * SUBLANES, LANES), F32),
            pltpu.VMEM((TL * SUBLANES, LANES), F32),
            pltpu.VMEM((bp + bs + 1, TAPS_A - 1, D), F32),
            pltpu.VMEM((bp + bs + 1, TAPS_B - 1, D), F32),
        ],
        compiler_params=pltpu.CompilerParams(dimension_semantics=("arbitrary",), vmem_limit_bytes=VMEM_LIMIT),
        name="mixer",
    )
    h, hn_slab, idx_t, gw_t, na_p, nb_p, na_s, nb_s = mixer(
        x_prompt.reshape(tp, D), x_sample.reshape(ts, D), state_conv_a[0], state_conv_b[0],
        row(g_mix), w_in[0].astype(BF16), row(b_in), w_conv_a[0],
        w_out_a[0].astype(BF16), w_conv_b[0].reshape(TAPS_B, SUBLANES, LANES), row(b_conv_b), row(ln_g), row(ln_b),
        w_out_b[0].astype(BF16), row(b_out_b), w_o[0].astype(BF16), row(g_ffn),
        w_router[0].T, b_router[0].reshape(N_EXPERTS, 1))

    dest_t, block_e, n_used, pad_start, pad_len = pl.pallas_call(
        functools.partial(_route_kernel, t, n_blocks_pad),
        out_shape=[
            jax.ShapeDtypeStruct((TOP_K, t), I32),
            jax.ShapeDtypeStruct((1, n_blocks_pad), I32),
            jax.ShapeDtypeStruct((1, LANES), I32),
            jax.ShapeDtypeStruct((N_EXPERTS, LANES), I32),
            jax.ShapeDtypeStruct((N_EXPERTS, LANES), I32),
        ],
        scratch_shapes=[pltpu.VMEM((ROUTE_CHUNK, ROUTE_CHUNK), BF16)],
        compiler_params=pltpu.CompilerParams(vmem_limit_bytes=VMEM_LIMIT),
        name="route",
    )(idx_t)
    dest_sm = dest_t.reshape(TOP_K, nt, TL).transpose(1, 0, 2).reshape(nt, 1, TOP_K * TL)
    block_e = block_e[0, :n_blocks]
    n_used = n_used[0, :1]
    pad_start, pad_len = pad_start[:, 0], pad_len[:, 0]

    smem_tile_spec = pl.BlockSpec((1, 1, TOP_K * TL), lambda i: (i, 0, 0), memory_space=pltpu.SMEM)
    any_spec = pl.BlockSpec(memory_space=pl.ANY)
    x_buf = pl.pallas_call(
        functools.partial(_dispatch_kernel, n_blocks),
        grid_spec=pltpu.PrefetchScalarGridSpec(
            num_scalar_prefetch=3,
            grid=(nt,),
            in_specs=[
                pl.BlockSpec((1, 1, TOP_K * TL), lambda i, *_: (i, 0, 0), memory_space=pltpu.SMEM),
                pl.BlockSpec((TL * SUBLANES, LANES), lambda i, *_: (i, 0)),
            ],
            out_specs=any_spec,
            scratch_shapes=[pltpu.VMEM((ROWS * SUBLANES, LANES), F32), pltpu.SemaphoreType.DMA,
                            pltpu.SemaphoreType.DMA],
        ),
        out_shape=jax.ShapeDtypeStruct((p_rows * SUBLANES, LANES), F32),
        compiler_params=pltpu.CompilerParams(dimension_semantics=("arbitrary",)),
        name="dispatch",
    )(pad_start, pad_len, n_used, dest_sm, hn_slab)

    def blk(b, be, nu):
        return jnp.maximum(jnp.minimum(b, nu[0] - 1), 0)

    out_buf = pl.pallas_call(
        _expert_kernel,
        grid_spec=pltpu.PrefetchScalarGridSpec(
            num_scalar_prefetch=2,
            grid=(n_blocks,),
            in_specs=[
                pl.BlockSpec((ROWS * SUBLANES, LANES), lambda b, be, nu: (blk(b, be, nu), 0)),
                any_spec,
                pl.BlockSpec((1, 1, 2 * D), lambda b, be, nu: (be[blk(b, be, nu)], 0, 0)),
                any_spec,
                pl.BlockSpec((1, 1, D), lambda b, be, nu: (be[blk(b, be, nu)], 0, 0)),
            ],
            out_specs=pl.BlockSpec((ROWS * SUBLANES, LANES), lambda b, be, nu: (b, 0)),
            scratch_shapes=[
                pltpu.VMEM((2, D, 2 * D), F32), pltpu.VMEM((2, D, D), F32),
                pltpu.VMEM((D, 2 * D), BF16), pltpu.VMEM((D, D), BF16),
                pltpu.SMEM((1,), I32), pltpu.SemaphoreType.DMA((2, 2)),
            ],
        ),
        out_shape=jax.ShapeDtypeStruct((p_rows * SUBLANES, LANES), F32),
        compiler_params=pltpu.CompilerParams(dimension_semantics=("arbitrary",), vmem_limit_bytes=VMEM_LIMIT),
        name="experts",
    )(block_e, n_used, x_buf, w_gate_up[0], b_gate_up[0].reshape(N_EXPERTS, 1, 2 * D), w_down[0],
      b_down[0].reshape(N_EXPERTS, 1, D))

    y_p, y_s = pl.pallas_call(
        functools.partial(_combine_kernel, nt, npt),
        grid=(nt,),
        in_specs=[
            smem_tile_spec,
            pl.BlockSpec((1, 1, TOP_K * TL), lambda i: (jnp.minimum(i + 1, nt - 1), 0, 0), memory_space=pltpu.SMEM),
            pl.BlockSpec((TL, D), lambda i: (i, 0)),
            pl.BlockSpec((TL, TOP_K), lambda i: (i, 0)),
            pl.BlockSpec((1, D), lambda i: (0, 0)),
            any_spec,
        ],
        out_specs=[
            pl.BlockSpec((TL, D), lambda i: (jnp.minimum(i, npt - 1), 0)),
            pl.BlockSpec((TL, D), lambda i: (jnp.maximum(i - npt, 0), 0)),
        ],
        out_shape=[jax.ShapeDtypeStruct((tp, D), F32), jax.ShapeDtypeStruct((ts, D), F32)],
        scratch_shapes=[pltpu.VMEM((TOP_K * TL * SUBLANES, LANES), F32), pltpu.VMEM((TOP_K * TL * SUBLANES, LANES), F32),
                        pltpu.SemaphoreType.DMA((2,))],
        compiler_params=pltpu.CompilerParams(dimension_semantics=("arbitrary",), vmem_limit_bytes=VMEM_LIMIT),
        name="combine",
    )(dest_sm, dest_sm, h, gw_t.T, row(g_final), out_buf)

    return (y_p.reshape(bp, lp, D), y_s.reshape(bs, ls, D), na_p[None], nb_p[None], na_s[None], nb_s[None])
```

```python
import functools

import jax
import jax.numpy as jnp
from jax import lax
from jax.experimental import pallas as pl
from jax.experimental.pallas import tpu as pltpu

F32 = jnp.float32
BF16 = jnp.bfloat16
I32 = jnp.int32

D = 1024
LANES = 128
SUBLANES = 8
NCHUNK = D // LANES
N_EXPERTS = 32
TOP_K = 4
TAPS_A = 3
TAPS_B = 31
EPS = 1e-5
SWIGLU_LIMIT = 7.0
SWIGLU_ALPHA = 1.702

TL = 256
SEG = 32
NSEG = TL // SEG
HIST = 32
HIST_A = 8
CONV_BLOCK = 4
ROWS = 256
WEIGHT_CHUNK = 64
ROUTE_CHUNK = 512
VMEM_LIMIT = 56 * 1024 * 1024


def _rows_to_slab(dst_ref, slot0, val):
    n = val.shape[0]
    for c in range(NCHUNK):
        dst_ref[pl.ds(SUBLANES * slot0 + c, n, stride=SUBLANES), :] = val[:, c * LANES:(c + 1) * LANES]


def _slab_to_rows(src_ref, slot0, n):
    return jnp.concatenate(
        [src_ref[pl.ds(SUBLANES * slot0 + c, n, stride=SUBLANES), :] for c in range(NCHUNK)], axis=1)


def _rmsnorm(x, g):
    return x * lax.rsqrt(jnp.mean(x * x, axis=-1, keepdims=True) + EPS) * g


def _conv_b_slab(vbuf, hist_rows, w_ref, dst_ref):
    first_tap = HIST - (TAPS_B - 1)
    for s in range(NSEG):
        cur_row0 = (HIST + s * SEG) * SUBLANES
        for t0 in range(0, SEG, CONV_BLOCK):
            ins = []
            for k in range(CONV_BLOCK + TAPS_B - 1):
                u = t0 + first_tap + k
                if u < HIST:
                    row = pl.multiple_of(hist_rows[s] + u * SUBLANES, SUBLANES)
                    ins.append(vbuf[pl.ds(row, SUBLANES), :])
                else:
                    row = cur_row0 + (u - HIST) * SUBLANES
                    ins.append(vbuf[row:row + SUBLANES, :])
            accs = [None] * CONV_BLOCK
            for j in range(TAPS_B):
                wj = w_ref[j]
                for o in range(CONV_BLOCK):
                    term = wj * ins[o + j]
                    accs[o] = term if accs[o] is None else accs[o] + term
            for o in range(CONV_BLOCK):
                row = (s * SEG + t0 + o) * SUBLANES
                dst_ref[row:row + SUBLANES, :] = accs[o]


def _stage_weight_as_bf16(w_hbm, w_bf, stage, sems):
    n_cols = w_hbm.shape[1]
    n_chunks = w_hbm.shape[0] // WEIGHT_CHUNK

    def chunk_copy(k, slot):
        rows = pl.ds(pl.multiple_of(k * WEIGHT_CHUNK, WEIGHT_CHUNK), WEIGHT_CHUNK)
        return pltpu.make_async_copy(w_hbm.at[rows, :], stage.at[slot, :, pl.ds(0, n_cols)], sems.at[slot])

    chunk_copy(0, 0).start()

    def body(k, carry):
        slot = lax.rem(k, 2)

        @pl.when(k + 1 < n_chunks)
        def _():
            chunk_copy(k + 1, 1 - slot).start()

        chunk_copy(k, slot).wait()
        rows = pl.ds(pl.multiple_of(k * WEIGHT_CHUNK, WEIGHT_CHUNK), WEIGHT_CHUNK)
        w_bf[rows, :] = stage[slot, :, 0:n_cols].astype(BF16)
        return carry

    lax.fori_loop(0, n_chunks, body, 0)


def _mixer_kernel(n_prompt_tiles, tiles_per_seq, n_prompt_seq, n_sample_seq,
                  xp_ref, xs_ref, sta_ref, stb_ref, gmix_ref, win_hbm, bin_ref, wca_ref, woa_hbm,
                  wcb_ref, bcb_ref, lng_ref, lnb_ref, wob_hbm, bob_ref, wo_hbm, gffn_ref, wrt_ref, brt_ref,
                  h_ref, hn_ref, idx_ref, gw_ref, nap_ref, nbp_ref, nas_ref, nbs_ref,
                  win_bf, woa_bf, wob_bf, wo_bf, wstage, wsems, cubuf, carry_a, vbuf, ybslab, sa_scr, sb_scr):
    i = pl.program_id(0)
    n_tiles = pl.num_programs(0)
    is_prompt = i < n_prompt_tiles
    seq_start = is_prompt & (lax.rem(i, tiles_per_seq) == 0)
    carry_rows = HIST * SUBLANES

    @pl.when(i == 0)
    def _():
        vbuf[0:carry_rows, :] = jnp.zeros((carry_rows, LANES), F32)
        carry_a[...] = jnp.zeros(carry_a.shape, F32)
        for w_hbm, w_bf in ((win_hbm, win_bf), (woa_hbm, woa_bf), (wob_hbm, wob_bf), (wo_hbm, wo_bf)):
            _stage_weight_as_bf16(w_hbm, w_bf, wstage, wsems)

    x = jnp.where(is_prompt, xp_ref[...], xs_ref[...])
    n_bf = _rmsnorm(x, gmix_ref[...]).astype(BF16)

    def proj(g):
        cols = slice(g * D, (g + 1) * D)
        return jnp.dot(n_bf, win_bf[:, cols], preferred_element_type=F32) + bin_ref[:, cols]

    cu = proj(1) * proj(2)
    v = proj(3) * jax.nn.sigmoid(proj(4))

    seg_rows = HIST_A + SEG
    prev_a = jnp.where(seq_start, 0.0, carry_a[HIST_A - (TAPS_A - 1):HIST_A, :])
    for s in range(NSEG):
        r0 = s * SEG
        prev = prev_a if s == 0 else cu[r0 - (TAPS_A - 1):r0, :]
        cubuf[s * seg_rows + HIST_A - (TAPS_A - 1):s * seg_rows + HIST_A, :] = jnp.where(is_prompt, prev, sta_ref[s])
        cubuf[s * seg_rows + HIST_A:(s + 1) * seg_rows, :] = cu[r0:r0 + SEG, :]
    carry_a[HIST_A - (TAPS_A - 1):HIST_A, :] = cu[TL - (TAPS_A - 1):TL, :]
    ya_parts = []
    for s in range(NSEG):
        acc = None
        for j in range(TAPS_A):
            lo = s * seg_rows + HIST_A - (TAPS_A - 1) + j
            term = wca_ref[j:j + 1, :] * cubuf[lo:lo + SEG, :]
            acc = term if acc is None else acc + term
        ya_parts.append(acc)
    ya = jnp.concatenate(ya_parts, axis=0)

    vbuf[0:carry_rows, :] = jnp.where(seq_start, 0.0, vbuf[0:carry_rows, :])
    _rows_to_slab(vbuf, HIST, v)
    state_slot0 = HIST + TL
    for s in range(NSEG):
        for c in range(NCHUNK):
            vbuf[pl.ds(SUBLANES * (state_slot0 + s * HIST + HIST - (TAPS_B - 1)) + c, TAPS_B - 1, stride=SUBLANES), :] = (
                stb_ref[s, :, c * LANES:(c + 1) * LANES])
    hist_rows = [jnp.where(is_prompt, s * SEG * SUBLANES, (state_slot0 + s * HIST) * SUBLANES) for s in range(NSEG)]
    _conv_b_slab(vbuf, hist_rows, wcb_ref, ybslab)
    vbuf[0:carry_rows, :] = vbuf[TL * SUBLANES:(TL + HIST) * SUBLANES, :]

    dummy = n_prompt_seq + n_sample_seq
    for s in range(NSEG):
        prompt_slot = i // tiles_per_seq if s == NSEG - 1 else dummy
        slot = jnp.where(is_prompt, prompt_slot, n_prompt_seq + (i - n_prompt_tiles) * NSEG + s)
        r1 = (s + 1) * SEG
        sa_scr[slot] = cu[r1 - (TAPS_A - 1):r1, :]
        sb_scr[slot] = v[r1 - (TAPS_B - 1):r1, :]

    @pl.when(i == n_tiles - 1)
    def _():
        nap_ref[...] = sa_scr[0:n_prompt_seq]
        nbp_ref[...] = sb_scr[0:n_prompt_seq]
        nas_ref[...] = sa_scr[n_prompt_seq:n_prompt_seq + n_sample_seq]
        nbs_ref[...] = sb_scr[n_prompt_seq:n_prompt_seq + n_sample_seq]

    yb = _slab_to_rows(ybslab, 0, TL) + bcb_ref[...]

    out_a = jnp.dot((proj(0) * ya).astype(BF16), woa_bf[...], preferred_element_type=F32)
    mu = jnp.mean(yb, axis=-1, keepdims=True)
    yc = yb - mu
    var = jnp.mean(yc * yc, axis=-1, keepdims=True)
    ln = yc * lax.rsqrt(var + EPS) * lng_ref[...] + lnb_ref[...]
    act = ln * jax.nn.sigmoid(ln)
    out_b = jnp.dot(act.astype(BF16), wob_bf[...], preferred_element_type=F32) + bob_ref[...]
    mix = jax.nn.sigmoid(proj(5)) * out_a + jax.nn.sigmoid(proj(6)) * out_b
    h = x + jnp.dot(mix.astype(BF16), wo_bf[...], preferred_element_type=F32)
    h_ref[...] = h
    hn = _rmsnorm(h, gffn_ref[...])
    _rows_to_slab(hn_ref, 0, hn)

    logits = lax.dot_general(wrt_ref[...], hn, (((1,), (1,)), ((), ())),
                             precision=lax.Precision.HIGHEST, preferred_element_type=F32) + brt_ref[...]
    iota_e = lax.broadcasted_iota(I32, (N_EXPERTS, TL), 0).astype(F32)
    vals, idxs = [], []
    for _ in range(TOP_K):
        m = jnp.max(logits, axis=0, keepdims=True)
        sel = jnp.min(jnp.where(logits == m, iota_e, float(N_EXPERTS)), axis=0, keepdims=True)
        vals.append(m)
        idxs.append(sel)
        logits = jnp.where(iota_e == sel, -jnp.inf, logits)
    ex = [jnp.exp(val - vals[0]) for val in vals]
    denom = ex[0] + ex[1] + ex[2] + ex[3]
    for k in range(TOP_K):
        idx_ref[k:k + 1, :] = idxs[k].astype(I32)
        gw_ref[k:k + 1, :] = ex[k] / denom


def _route_kernel(n_tokens, n_blocks_pad, idx_ref, dest_ref, be_ref, nu_ref, pad_start_ref, pad_len_ref, tri_scr):
    n_chunks = n_tokens // ROUTE_CHUNK
    iota_e = lax.broadcasted_iota(I32, (N_EXPERTS, ROUTE_CHUNK), 0)

    def masks(c):
        sl = pl.ds(pl.multiple_of(c * ROUTE_CHUNK, ROUTE_CHUNK), ROUTE_CHUNK)
        return sl, [idx_ref[k:k + 1, sl] == iota_e for k in range(TOP_K)]

    def onehot_sum(ms):
        tot = ms[0].astype(F32)
        for k in range(1, TOP_K):
            tot = tot + ms[k].astype(F32)
        return tot

    def count_body(c, acc):
        _, ms = masks(c)
        return acc + onehot_sum(ms)

    acc = lax.fori_loop(0, n_chunks, count_body, jnp.zeros((N_EXPERTS, ROUTE_CHUNK), F32))
    counts = jnp.sum(acc, axis=1, keepdims=True)
    nblk = jnp.floor((counts + (ROWS - 1)) * (1.0 / ROWS))
    r = lax.broadcasted_iota(I32, (N_EXPERTS, N_EXPERTS), 0)
    cidx = lax.broadcasted_iota(I32, (N_EXPERTS, N_EXPERTS), 1)
    lower = (cidx <= r).astype(F32)
    pend = jnp.dot(lower, jnp.broadcast_to(nblk, (N_EXPERTS, LANES)),
                   precision=lax.Precision.HIGHEST, preferred_element_type=F32)
    pend1 = pend[:, 0:1]
    pstart_rows = (pend1 - nblk) * ROWS
    pad_start_ref[...] = jnp.broadcast_to(pstart_rows + counts, (N_EXPERTS, LANES)).astype(I32)
    pad_len_ref[...] = jnp.broadcast_to(nblk * ROWS - counts, (N_EXPERTS, LANES)).astype(I32)

    bvec = lax.broadcasted_iota(I32, (N_EXPERTS, n_blocks_pad), 1).astype(F32)
    be = jnp.sum((pend1 <= bvec).astype(F32), axis=0, keepdims=True)
    be_ref[...] = jnp.minimum(be, N_EXPERTS - 1).astype(I32)
    last = lax.broadcasted_iota(I32, (N_EXPERTS, LANES), 0) == N_EXPERTS - 1
    nu_ref[...] = jnp.sum(jnp.where(last, pend, 0.0), axis=0, keepdims=True).astype(I32)

    tr = lax.broadcasted_iota(I32, (ROUTE_CHUNK, ROUTE_CHUNK), 0)
    tc = lax.broadcasted_iota(I32, (ROUTE_CHUNK, ROUTE_CHUNK), 1)
    tri_scr[...] = (tr < tc).astype(BF16)

    def dest_body(c, carry):
        sl, ms = masks(c)
        tot = onehot_sum(ms)
        before = jnp.dot(tot.astype(BF16), tri_scr[...], preferred_element_type=F32)
        pos = pstart_rows + carry + before
        for k in range(TOP_K):
            dest_ref[k:k + 1, sl] = jnp.sum(jnp.where(ms[k], pos, 0.0), axis=0, keepdims=True).astype(I32)
        return carry + jnp.sum(tot, axis=1, keepdims=True)

    lax.fori_loop(0, n_chunks, dest_body, jnp.zeros((N_EXPERTS, 1), F32))


def _dispatch_kernel(n_blocks, pad_start_ref, pad_len_ref, nu_ref, dest_ref, hn_ref, xb_hbm, ztile, sem, pad_sem):
    def row_copy(tt, k):
        src = pl.multiple_of(tt * SUBLANES, SUBLANES)
        dst = pl.multiple_of(dest_ref[0, 0, k * TL + tt] * SUBLANES, SUBLANES)
        return pltpu.make_async_copy(hn_ref.at[pl.ds(src, SUBLANES), :], xb_hbm.at[pl.ds(dst, SUBLANES), :], sem)

    def body(tt, carry):
        for k in range(TOP_K):
            row_copy(tt, k).start(priority=k % 2)
        return carry

    lax.fori_loop(0, TL, body, 0)

    @pl.when(pl.program_id(0) == 0)
    def _():
        ztile[...] = jnp.zeros(ztile.shape, F32)

        def pad_copy(row):
            dst = pl.multiple_of(row * SUBLANES, SUBLANES)
            return pltpu.make_async_copy(ztile.at[pl.ds(0, SUBLANES), :], xb_hbm.at[pl.ds(dst, SUBLANES), :], pad_sem)

        def tail_copy(blk):
            dst = pl.multiple_of(blk * (ROWS * SUBLANES), ROWS * SUBLANES)
            return pltpu.make_async_copy(ztile, xb_hbm.at[pl.ds(dst, ROWS * SUBLANES), :], pad_sem)

        def run(copy_at, lo, hi):
            def issue(r, carry):
                copy_at(r).start()
                return carry

            def drain(r, carry):
                copy_at(r).wait()
                return carry

            lax.fori_loop(lo, hi, issue, 0)
            lax.fori_loop(lo, hi, drain, 0)

        for e in range(N_EXPERTS):
            start = pad_start_ref[e]
            run(pad_copy, start, start + pad_len_ref[e])
        run(tail_copy, nu_ref[0], n_blocks)

    n = TL * SUBLANES
    for k in range(TOP_K):
        pltpu.make_async_copy(hn_ref, xb_hbm.at[pl.ds(0, n), :], sem).wait()


def _expert_kernel(be_ref, nu_ref, xs_ref, wgu_hbm, bgu_ref, wd_hbm, bd_ref, o_ref,
                   wgu_f32, wd_f32, wgu_bf, wd_bf, slot_ref, sems):
    b = pl.program_id(0)
    n_used = nu_ref[0]
    e = be_ref[b]
    prev = be_ref[jnp.maximum(b - 1, 0)]
    active = b < n_used

    def weight_copies(expert, slot):
        return (pltpu.make_async_copy(wgu_hbm.at[expert], wgu_f32.at[slot], sems.at[0, slot]),
                pltpu.make_async_copy(wd_hbm.at[expert], wd_f32.at[slot], sems.at[1, slot]))

    @pl.when(active & (b == 0))
    def _():
        slot_ref[0] = 0
        for c in weight_copies(e, 0):
            c.start()

    @pl.when(active & ((b == 0) | (e != prev)))
    def _():
        slot = slot_ref[0]
        for c in weight_copies(e, slot):
            c.wait()
        last = n_used - 1
        nxt = lax.while_loop(lambda j: (j <= last) & (be_ref[jnp.minimum(j, last)] == e), lambda j: j + 1, b + 1)

        @pl.when(nxt <= last)
        def _():
            for c in weight_copies(be_ref[jnp.minimum(nxt, last)], 1 - slot):
                c.start(priority=1)

        wgu_bf[...] = wgu_f32[slot].astype(BF16)
        wd_bf[...] = wd_f32[slot].astype(BF16)
        slot_ref[0] = 1 - slot

    @pl.when(active)
    def _():
        xb = _slab_to_rows(xs_ref, 0, ROWS).astype(BF16)
        gu = jnp.dot(xb, wgu_bf[...], preferred_element_type=F32) + bgu_ref[0]
        gate = jnp.minimum(gu[:, :D], SWIGLU_LIMIT)
        up = jnp.clip(gu[:, D:], -SWIGLU_LIMIT, SWIGLU_LIMIT)
        hidden = (up + 1.0) * gate * jax.nn.sigmoid(SWIGLU_ALPHA * gate)
        out = jnp.dot(hidden.astype(BF16), wd_bf[...], preferred_element_type=F32) + bd_ref[0]
        _rows_to_slab(o_ref, 0, out)

    @pl.when(jnp.logical_not(active))
    def _():
        o_ref[...] = jnp.zeros(o_ref.shape, F32)


def _combine_kernel(n_tiles, n_prompt_tiles, dest_ref, dest_next_ref, h_ref, gw_ref, gfin_ref, ob_hbm,
                    yp_ref, ys_ref, gath0, gath1, sems):
    i = pl.program_id(0)
    n = TOP_K * TL * SUBLANES

    def issue(table_ref, gath, sem):
        def body(tt, carry):
            for k in range(TOP_K):
                src = pl.multiple_of(table_ref[0, 0, k * TL + tt] * SUBLANES, SUBLANES)
                dst = pl.multiple_of((k * TL + tt) * SUBLANES, SUBLANES)
                pltpu.make_async_copy(ob_hbm.at[pl.ds(src, SUBLANES), :], gath.at[pl.ds(dst, SUBLANES), :],
                                      sem).start(priority=k % 2)
            return carry

        lax.fori_loop(0, TL, body, 0)

    def step(cur, cur_sem, nxt, nxt_sem):
        @pl.when(i + 1 < n_tiles)
        def _():
            issue(dest_next_ref, nxt, nxt_sem)

        pltpu.make_async_copy(ob_hbm.at[pl.ds(0, n), :], cur, cur_sem).wait()
        acc = h_ref[...]
        gw = gw_ref[...]
        for k in range(TOP_K):
            acc = acc + gw[:, k:k + 1] * _slab_to_rows(cur, k * TL, TL)
        y = _rmsnorm(acc, gfin_ref[...])
        is_prompt = i < n_prompt_tiles

        @pl.when(is_prompt)
        def _():
            yp_ref[...] = y

        @pl.when(jnp.logical_not(is_prompt))
        def _():
            ys_ref[...] = y

    @pl.when(i == 0)
    def _():
        issue(dest_ref, gath0, sems.at[0])

    parity = lax.rem(i, 2)

    @pl.when(parity == 0)
    def _():
        step(gath0, sems.at[0], gath1, sems.at[1])

    @pl.when(parity == 1)
    def _():
        step(gath1, sems.at[1], gath0, sems.at[0])


def _const_spec(shape):
    nd = len(shape)
    return pl.BlockSpec(shape, lambda *_: (0,) * nd, pipeline_mode=pl.Buffered(1))


def kernel(x_prompt, x_sample, state_conv_a, state_conv_b, g_mix, w_in, b_in, w_conv_a, w_out_a, w_conv_b, b_conv_b, ln_g, ln_b, w_out_b, b_out_b, w_o, g_ffn, w_router, b_router, w_gate_up, b_gate_up, w_down, b_down, g_final):
    depth = g_mix.shape[0]
    assert depth == 1
    bp, lp, d = x_prompt.shape
    bs, ls, _ = x_sample.shape
    assert d == D and w_in.shape[2] == 7 * D
    tp, ts = bp * lp, bs * ls
    t = tp + ts
    assert lp % TL == 0 and ls == SEG and ts % TL == 0 and t % ROUTE_CHUNK == 0
    assert TAPS_B - 1 <= HIST <= SEG and TAPS_A - 1 <= HIST_A
    npt, nst = tp // TL, ts // TL
    nt = npt + nst
    tiles_per_seq = lp // TL
    n_blocks = (t * TOP_K) // ROWS + N_EXPERTS
    n_blocks_pad = -(-n_blocks // LANES) * LANES
    p_rows = n_blocks * ROWS

    row = lambda a: a.reshape(1, -1)
    f32_spec = lambda: _const_spec((1, D))

    mixer = pl.pallas_call(
        functools.partial(_mixer_kernel, npt, tiles_per_seq, bp, bs),
        grid=(nt,),
        in_specs=[
            pl.BlockSpec((TL, D), lambda i: (jnp.minimum(i, npt - 1), 0)),
            pl.BlockSpec((TL, D), lambda i: (jnp.maximum(i - npt, 0), 0)),
            pl.BlockSpec((NSEG, TAPS_A - 1, D), lambda i: (jnp.maximum(i - npt, 0), 0, 0)),
            pl.BlockSpec((NSEG, TAPS_B - 1, D), lambda i: (jnp.maximum(i - npt, 0), 0, 0)),
            f32_spec(),
            pl.BlockSpec(memory_space=pl.ANY),
            _const_spec((1, 7 * D)),
            _const_spec((TAPS_A, D)),
            pl.BlockSpec(memory_space=pl.ANY),
            _const_spec((TAPS_B, SUBLANES, LANES)),
            f32_spec(), f32_spec(), f32_spec(),
            pl.BlockSpec(memory_space=pl.ANY),
            f32_spec(),
            pl.BlockSpec(memory_space=pl.ANY),
            f32_spec(),
            _const_spec((N_EXPERTS, D)),
            _const_spec((N_EXPERTS, 1)),
        ],
        out_specs=[
            pl.BlockSpec((TL, D), lambda i: (i, 0)),
            pl.BlockSpec((TL * SUBLANES, LANES), lambda i: (i, 0)),
            pl.BlockSpec((TOP_K, TL), lambda i: (0, i)),
            pl.BlockSpec((TOP_K, TL), lambda i: (0, i)),
            pl.BlockSpec((bp, TAPS_A - 1, D), lambda i: (0, 0, 0)),
            pl.BlockSpec((bp, TAPS_B - 1, D), lambda i: (0, 0, 0)),
            pl.BlockSpec((bs, TAPS_A - 1, D), lambda i: (0, 0, 0)),
            pl.BlockSpec((bs, TAPS_B - 1, D), lambda i: (0, 0, 0)),
        ],
        out_shape=[
            jax.ShapeDtypeStruct((t, D), F32),
            jax.ShapeDtypeStruct((t * SUBLANES, LANES), F32),
            jax.ShapeDtypeStruct((TOP_K, t), I32),
            jax.ShapeDtypeStruct((TOP_K, t), F32),
            jax.ShapeDtypeStruct((bp, TAPS_A - 1, D), F32),
            jax.ShapeDtypeStruct((bp, TAPS_B - 1, D), F32),
            jax.ShapeDtypeStruct((bs, TAPS_A - 1, D), F32),
            jax.ShapeDtypeStruct((bs, TAPS_B - 1, D), F32),
        ],
        scratch_shapes=[
            pltpu.VMEM((D, 7 * D), BF16), pltpu.VMEM((D, D), BF16), pltpu.VMEM((D, D), BF16), pltpu.VMEM((D, D), BF16),
            pltpu.VMEM((2, WEIGHT_CHUNK, 7 * D), F32), pltpu.SemaphoreType.DMA((2,)),
            pltpu.VMEM((NSEG * (HIST_A + SEG), D), F32),
            pltpu.VMEM((HIST_A, D), F32),
            pltpu.VMEM(((HIST + TL + NSEG * HIST) * SUBLANES, LANES), F32),
            pltpu.VMEM((TL * SUBLANES, LANES), F32),
            pltpu.VMEM((bp + bs + 1, TAPS_A - 1, D), F32),
            pltpu.VMEM((bp + bs + 1, TAPS_B - 1, D), F32),
        ],
        compiler_params=pltpu.CompilerParams(dimension_semantics=("arbitrary",), vmem_limit_bytes=VMEM_LIMIT),
        name="mixer",
    )
    h, hn_slab, idx_t, gw_t, na_p, nb_p, na_s, nb_s = mixer(
        x_prompt.reshape(tp, D), x_sample.reshape(ts, D), state_conv_a[0], state_conv_b[0],
        row(g_mix), w_in[0], row(b_in), w_conv_a[0],
        w_out_a[0], w_conv_b[0].reshape(TAPS_B, SUBLANES, LANES), row(b_conv_b), row(ln_g), row(ln_b),
        w_out_b[0], row(b_out_b), w_o[0], row(g_ffn),
        w_router[0].T, b_router[0].reshape(N_EXPERTS, 1))

    dest_t, block_e, n_used, pad_start, pad_len = pl.pallas_call(
        functools.partial(_route_kernel, t, n_blocks_pad),
        out_shape=[
            jax.ShapeDtypeStruct((TOP_K, t), I32),
            jax.ShapeDtypeStruct((1, n_blocks_pad), I32),
            jax.ShapeDtypeStruct((1, LANES), I32),
            jax.ShapeDtypeStruct((N_EXPERTS, LANES), I32),
            jax.ShapeDtypeStruct((N_EXPERTS, LANES), I32),
        ],
        scratch_shapes=[pltpu.VMEM((ROUTE_CHUNK, ROUTE_CHUNK), BF16)],
        compiler_params=pltpu.CompilerParams(vmem_limit_bytes=VMEM_LIMIT),
        name="route",
    )(idx_t)
    dest_sm = dest_t.reshape(TOP_K, nt, TL).transpose(1, 0, 2).reshape(nt, 1, TOP_K * TL)
    block_e = block_e[0, :n_blocks]
    n_used = n_used[0, :1]
    pad_start, pad_len = pad_start[:, 0], pad_len[:, 0]

    smem_tile_spec = pl.BlockSpec((1, 1, TOP_K * TL), lambda i: (i, 0, 0), memory_space=pltpu.SMEM)
    any_spec = pl.BlockSpec(memory_space=pl.ANY)
    x_buf = pl.pallas_call(
        functools.partial(_dispatch_kernel, n_blocks),
        grid_spec=pltpu.PrefetchScalarGridSpec(
            num_scalar_prefetch=3,
            grid=(nt,),
            in_specs=[
                pl.BlockSpec((1, 1, TOP_K * TL), lambda i, *_: (i, 0, 0), memory_space=pltpu.SMEM),
                pl.BlockSpec((TL * SUBLANES, LANES), lambda i, *_: (i, 0)),
            ],
            out_specs=any_spec,
            scratch_shapes=[pltpu.VMEM((ROWS * SUBLANES, LANES), F32), pltpu.SemaphoreType.DMA,
                            pltpu.SemaphoreType.DMA],
        ),
        out_shape=jax.ShapeDtypeStruct((p_rows * SUBLANES, LANES), F32),
        compiler_params=pltpu.CompilerParams(dimension_semantics=("arbitrary",)),
        name="dispatch",
    )(pad_start, pad_len, n_used, dest_sm, hn_slab)

    def blk(b, be, nu):
        return jnp.maximum(jnp.minimum(b, nu[0] - 1), 0)

    out_buf = pl.pallas_call(
        _expert_kernel,
        grid_spec=pltpu.PrefetchScalarGridSpec(
            num_scalar_prefetch=2,
            grid=(n_blocks,),
            in_specs=[
                pl.BlockSpec((ROWS * SUBLANES, LANES), lambda b, be, nu: (blk(b, be, nu), 0)),
                any_spec,
                pl.BlockSpec((1, 1, 2 * D), lambda b, be, nu: (be[blk(b, be, nu)], 0, 0)),
                any_spec,
                pl.BlockSpec((1, 1, D), lambda b, be, nu: (be[blk(b, be, nu)], 0, 0)),
            ],
            out_specs=pl.BlockSpec((ROWS * SUBLANES, LANES), lambda b, be, nu: (b, 0)),
            scratch_shapes=[
                pltpu.VMEM((2, D, 2 * D), F32), pltpu.VMEM((2, D, D), F32),
                pltpu.VMEM((D, 2 * D), BF16), pltpu.VMEM((D, D), BF16),
                pltpu.SMEM((1,), I32), pltpu.SemaphoreType.DMA((2, 2)),
            ],
        ),
        out_shape=jax.ShapeDtypeStruct((p_rows * SUBLANES, LANES), F32),
        compiler_params=pltpu.CompilerParams(dimension_semantics=("arbitrary",), vmem_limit_bytes=VMEM_LIMIT),
        name="experts",
    )(block_e, n_used, x_buf, w_gate_up[0], b_gate_up[0].reshape(N_EXPERTS, 1, 2 * D), w_down[0],
      b_down[0].reshape(N_EXPERTS, 1, D))

    y_p, y_s = pl.pallas_call(
        functools.partial(_combine_kernel, nt, npt),
        grid=(nt,),
        in_specs=[
            smem_tile_spec,
            pl.BlockSpec((1, 1, TOP_K * TL), lambda i: (jnp.minimum(i + 1, nt - 1), 0, 0), memory_space=pltpu.SMEM),
            pl.BlockSpec((TL, D), lambda i: (i, 0)),
            pl.BlockSpec((TL, TOP_K), lambda i: (i, 0)),
            pl.BlockSpec((1, D), lambda i: (0, 0)),
            any_spec,
        ],
        out_specs=[
            pl.BlockSpec((TL, D), lambda i: (jnp.minimum(i, npt - 1), 0)),
            pl.BlockSpec((TL, D), lambda i: (jnp.maximum(i - npt, 0), 0)),
        ],
        out_shape=[jax.ShapeDtypeStruct((tp, D), F32), jax.ShapeDtypeStruct((ts, D), F32)],
        scratch_shapes=[pltpu.VMEM((TOP_K * TL * SUBLANES, LANES), F32), pltpu.VMEM((TOP_K * TL * SUBLANES, LANES), F32),
                        pltpu.SemaphoreType.DMA((2,))],
        compiler_params=pltpu.CompilerParams(dimension_semantics=("arbitrary",), vmem_limit_bytes=VMEM_LIMIT),
        name="combine",
    )(dest_sm, dest_sm, h, gw_t.T, row(g_final), out_buf)

    return (y_p.reshape(bp, lp, D), y_s.reshape(bs, ls, D), na_p[None], nb_p[None], na_s[None], nb_s[None])
```

```python
import functools

import jax
import jax.numpy as jnp
from jax import lax
from jax.experimental import pallas as pl
from jax.experimental.pallas import tpu as pltpu

F32 = jnp.float32
BF16 = jnp.bfloat16
I32 = jnp.int32

D = 1024
LANES = 128
SUBLANES = 8
NCHUNK = D // LANES
N_EXPERTS = 32
TOP_K = 4
TAPS_A = 3
TAPS_B = 31
EPS = 1e-5
SWIGLU_LIMIT = 7.0
SWIGLU_ALPHA = 1.702

TL = 256
SEG = 32
NSEG = TL // SEG
HIST = 32
HIST_A = 8
CONV_BLOCK = 4
ROWS = 256
ROUTE_CHUNK = 512
VMEM_LIMIT = 56 * 1024 * 1024


def _rows_to_slab(dst_ref, slot0, val):
    n = val.shape[0]
    for c in range(NCHUNK):
        dst_ref[pl.ds(SUBLANES * slot0 + c, n, stride=SUBLANES), :] = val[:, c * LANES:(c + 1) * LANES]


def _slab_to_rows(src_ref, slot0, n):
    return jnp.concatenate(
        [src_ref[pl.ds(SUBLANES * slot0 + c, n, stride=SUBLANES), :] for c in range(NCHUNK)], axis=1)


def _rmsnorm(x, g):
    return x * lax.rsqrt(jnp.mean(x * x, axis=-1, keepdims=True) + EPS) * g


def _conv_b_slab(vbuf, hist_rows, w_ref, dst_ref):
    first_tap = HIST - (TAPS_B - 1)
    for s in range(NSEG):
        cur_row0 = (HIST + s * SEG) * SUBLANES
        for t0 in range(0, SEG, CONV_BLOCK):
            ins = []
            for k in range(CONV_BLOCK + TAPS_B - 1):
                u = t0 + first_tap + k
                if u < HIST:
                    row = pl.multiple_of(hist_rows[s] + u * SUBLANES, SUBLANES)
                    ins.append(vbuf[pl.ds(row, SUBLANES), :])
                else:
                    row = cur_row0 + (u - HIST) * SUBLANES
                    ins.append(vbuf[row:row + SUBLANES, :])
            accs = [None] * CONV_BLOCK
            for j in range(TAPS_B):
                wj = w_ref[j]
                for o in range(CONV_BLOCK):
                    term = wj * ins[o + j]
                    accs[o] = term if accs[o] is None else accs[o] + term
            for o in range(CONV_BLOCK):
                row = (s * SEG + t0 + o) * SUBLANES
                dst_ref[row:row + SUBLANES, :] = accs[o]


def _mixer_kernel(n_prompt_tiles, tiles_per_seq, n_prompt_seq, n_sample_seq,
                  xp_ref, xs_ref, sta_ref, stb_ref, gmix_ref, win_ref, bin_ref, wca_ref, woa_ref,
                  wcb_ref, bcb_ref, lng_ref, lnb_ref, wob_ref, bob_ref, wo_ref, gffn_ref, wrt_ref, brt_ref,
                  h_ref, hn_ref, idx_ref, gw_ref, nap_ref, nbp_ref, nas_ref, nbs_ref,
                  cubuf, carry_a, vbuf, ybslab, sa_scr, sb_scr):
    i = pl.program_id(0)
    n_tiles = pl.num_programs(0)
    is_prompt = i < n_prompt_tiles
    seq_start = is_prompt & (lax.rem(i, tiles_per_seq) == 0)
    carry_rows = HIST * SUBLANES

    @pl.when(i == 0)
    def _():
        vbuf[0:carry_rows, :] = jnp.zeros((carry_rows, LANES), F32)
        carry_a[...] = jnp.zeros(carry_a.shape, F32)

    x = jnp.where(is_prompt, xp_ref[...], xs_ref[...])
    n_bf = _rmsnorm(x, gmix_ref[...]).astype(BF16)

    def proj(g):
        cols = slice(g * D, (g + 1) * D)
        return jnp.dot(n_bf, win_ref[:, cols], preferred_element_type=F32) + bin_ref[:, cols]

    cu = proj(1) * proj(2)
    v = proj(3) * jax.nn.sigmoid(proj(4))

    seg_rows = HIST_A + SEG
    prev_a = jnp.where(seq_start, 0.0, carry_a[HIST_A - (TAPS_A - 1):HIST_A, :])
    for s in range(NSEG):
        r0 = s * SEG
        prev = prev_a if s == 0 else cu[r0 - (TAPS_A - 1):r0, :]
        cubuf[s * seg_rows + HIST_A - (TAPS_A - 1):s * seg_rows + HIST_A, :] = jnp.where(is_prompt, prev, sta_ref[s])
        cubuf[s * seg_rows + HIST_A:(s + 1) * seg_rows, :] = cu[r0:r0 + SEG, :]
    carry_a[HIST_A - (TAPS_A - 1):HIST_A, :] = cu[TL - (TAPS_A - 1):TL, :]
    ya_parts = []
    for s in range(NSEG):
        acc = None
        for j in range(TAPS_A):
            lo = s * seg_rows + HIST_A - (TAPS_A - 1) + j
            term = wca_ref[j:j + 1, :] * cubuf[lo:lo + SEG, :]
            acc = term if acc is None else acc + term
        ya_parts.append(acc)
    ya = jnp.concatenate(ya_parts, axis=0)

    vbuf[0:carry_rows, :] = jnp.where(seq_start, 0.0, vbuf[0:carry_rows, :])
    _rows_to_slab(vbuf, HIST, v)
    state_slot0 = HIST + TL
    for s in range(NSEG):
        for c in range(NCHUNK):
            vbuf[pl.ds(SUBLANES * (state_slot0 + s * HIST + HIST - (TAPS_B - 1)) + c, TAPS_B - 1, stride=SUBLANES), :] = (
                stb_ref[s, :, c * LANES:(c + 1) * LANES])
    hist_rows = [jnp.where(is_prompt, s * SEG * SUBLANES, (state_slot0 + s * HIST) * SUBLANES) for s in range(NSEG)]
    _conv_b_slab(vbuf, hist_rows, wcb_ref, ybslab)
    vbuf[0:carry_rows, :] = vbuf[TL * SUBLANES:(TL + HIST) * SUBLANES, :]

    dummy = n_prompt_seq + n_sample_seq
    for s in range(NSEG):
        prompt_slot = i // tiles_per_seq if s == NSEG - 1 else dummy
        slot = jnp.where(is_prompt, prompt_slot, n_prompt_seq + (i - n_prompt_tiles) * NSEG + s)
        r1 = (s + 1) * SEG
        sa_scr[slot] = cu[r1 - (TAPS_A - 1):r1, :]
        sb_scr[slot] = v[r1 - (TAPS_B - 1):r1, :]

    @pl.when(i == n_tiles - 1)
    def _():
        nap_ref[...] = sa_scr[0:n_prompt_seq]
        nbp_ref[...] = sb_scr[0:n_prompt_seq]
        nas_ref[...] = sa_scr[n_prompt_seq:n_prompt_seq + n_sample_seq]
        nbs_ref[...] = sb_scr[n_prompt_seq:n_prompt_seq + n_sample_seq]

    yb = _slab_to_rows(ybslab, 0, TL) + bcb_ref[...]

    out_a = jnp.dot((proj(0) * ya).astype(BF16), woa_ref[...], preferred_element_type=F32)
    mu = jnp.mean(yb, axis=-1, keepdims=True)
    yc = yb - mu
    var = jnp.mean(yc * yc, axis=-1, keepdims=True)
    ln = yc * lax.rsqrt(var + EPS) * lng_ref[...] + lnb_ref[...]
    act = ln * jax.nn.sigmoid(ln)
    out_b = jnp.dot(act.astype(BF16), wob_ref[...], preferred_element_type=F32) + bob_ref[...]
    mix = jax.nn.sigmoid(proj(5)) * out_a + jax.nn.sigmoid(proj(6)) * out_b
    h = x + jnp.dot(mix.astype(BF16), wo_ref[...], preferred_element_type=F32)
    h_ref[...] = h
    hn = _rmsnorm(h, gffn_ref[...])
    _rows_to_slab(hn_ref, 0, hn)

    logits = lax.dot_general(wrt_ref[...], hn, (((1,), (1,)), ((), ())),
                             precision=lax.Precision.HIGHEST, preferred_element_type=F32) + brt_ref[...]
    iota_e = lax.broadcasted_iota(I32, (N_EXPERTS, TL), 0).astype(F32)
    vals, idxs = [], []
    for _ in range(TOP_K):
        m = jnp.max(logits, axis=0, keepdims=True)
        sel = jnp.min(jnp.where(logits == m, iota_e, float(N_EXPERTS)), axis=0, keepdims=True)
        vals.append(m)
        idxs.append(sel)
        logits = jnp.where(iota_e == sel, -jnp.inf, logits)
    ex = [jnp.exp(val - vals[0]) for val in vals]
    denom = ex[0] + ex[1] + ex[2] + ex[3]
    for k in range(TOP_K):
        idx_ref[k:k + 1, :] = idxs[k].astype(I32)
        gw_ref[k:k + 1, :] = ex[k] / denom


def _route_kernel(n_tokens, n_blocks_pad, idx_ref, dest_ref, be_ref, nu_ref, pad_start_ref, pad_len_ref, tri_scr):
    n_chunks = n_tokens // ROUTE_CHUNK
    iota_e = lax.broadcasted_iota(I32, (N_EXPERTS, ROUTE_CHUNK), 0)

    def masks(c):
        sl = pl.ds(pl.multiple_of(c * ROUTE_CHUNK, ROUTE_CHUNK), ROUTE_CHUNK)
        return sl, [idx_ref[k:k + 1, sl] == iota_e for k in range(TOP_K)]

    def onehot_sum(ms):
        tot = ms[0].astype(F32)
        for k in range(1, TOP_K):
            tot = tot + ms[k].astype(F32)
        return tot

    def count_body(c, acc):
        _, ms = masks(c)
        return acc + onehot_sum(ms)

    acc = lax.fori_loop(0, n_chunks, count_body, jnp.zeros((N_EXPERTS, ROUTE_CHUNK), F32))
    counts = jnp.sum(acc, axis=1, keepdims=True)
    nblk = jnp.floor((counts + (ROWS - 1)) * (1.0 / ROWS))
    r = lax.broadcasted_iota(I32, (N_EXPERTS, N_EXPERTS), 0)
    cidx = lax.broadcasted_iota(I32, (N_EXPERTS, N_EXPERTS), 1)
    lower = (cidx <= r).astype(F32)
    pend = jnp.dot(lower, jnp.broadcast_to(nblk, (N_EXPERTS, LANES)),
                   precision=lax.Precision.HIGHEST, preferred_element_type=F32)
    pend1 = pend[:, 0:1]
    pstart_rows = (pend1 - nblk) * ROWS
    pad_start_ref[...] = jnp.broadcast_to(pstart_rows + counts, (N_EXPERTS, LANES)).astype(I32)
    pad_len_ref[...] = jnp.broadcast_to(nblk * ROWS - counts, (N_EXPERTS, LANES)).astype(I32)

    bvec = lax.broadcasted_iota(I32, (N_EXPERTS, n_blocks_pad), 1).astype(F32)
    be = jnp.sum((pend1 <= bvec).astype(F32), axis=0, keepdims=True)
    be_ref[...] = jnp.minimum(be, N_EXPERTS - 1).astype(I32)
    last = lax.broadcasted_iota(I32, (N_EXPERTS, LANES), 0) == N_EXPERTS - 1
    nu_ref[...] = jnp.sum(jnp.where(last, pend, 0.0), axis=0, keepdims=True).astype(I32)

    tr = lax.broadcasted_iota(I32, (ROUTE_CHUNK, ROUTE_CHUNK), 0)
    tc = lax.broadcasted_iota(I32, (ROUTE_CHUNK, ROUTE_CHUNK), 1)
    tri_scr[...] = (tr < tc).astype(BF16)

    def dest_body(c, carry):
        sl, ms = masks(c)
        tot = onehot_sum(ms)
        before = jnp.dot(tot.astype(BF16), tri_scr[...], preferred_element_type=F32)
        pos = pstart_rows + carry + before
        for k in range(TOP_K):
            dest_k = jnp.sum(jnp.where(ms[k], pos, 0.0), axis=0, keepdims=True).astype(I32)
            for part in range(ROUTE_CHUNK // TL):
                dest_ref[c * (ROUTE_CHUNK // TL) + part, :, k * TL:(k + 1) * TL] = dest_k[:, part * TL:(part + 1) * TL]
        return carry + jnp.sum(tot, axis=1, keepdims=True)

    lax.fori_loop(0, n_chunks, dest_body, jnp.zeros((N_EXPERTS, 1), F32))


def _dispatch_kernel(n_blocks, pad_start_ref, pad_len_ref, nu_ref, dest_ref, hn_ref, xb_hbm, ztile, sem, pad_sem):
    def row_copy(tt, k):
        src = pl.multiple_of(tt * SUBLANES, SUBLANES)
        dst = pl.multiple_of(dest_ref[0, 0, k * TL + tt] * SUBLANES, SUBLANES)
        return pltpu.make_async_copy(hn_ref.at[pl.ds(src, SUBLANES), :], xb_hbm.at[pl.ds(dst, SUBLANES), :], sem)

    def body(tt, carry):
        for k in range(TOP_K):
            row_copy(tt, k).start(priority=k % 2)
        return carry

    lax.fori_loop(0, TL, body, 0)

    @pl.when(pl.program_id(0) == 0)
    def _():
        ztile[...] = jnp.zeros(ztile.shape, F32)

        def zero_copy(row, n_slabs):
            dst = pl.multiple_of(row * SUBLANES, SUBLANES)
            return pltpu.make_async_copy(ztile.at[pl.ds(0, n_slabs * SUBLANES), :],
                                         xb_hbm.at[pl.ds(dst, n_slabs * SUBLANES), :], pad_sem)

        def pad_pass(act):
            def per_expert(e, carry):
                row, length = pad_start_ref[e], pad_len_ref[e]
                for bit in reversed(range(ROWS.bit_length() - 1)):
                    take = (length & (1 << bit)) != 0

                    @pl.when(take)
                    def _():
                        act(zero_copy(row, 1 << bit))

                    row = row + jnp.where(take, 1 << bit, 0)
                return carry

            lax.fori_loop(0, N_EXPERTS, per_expert, 0)

        def tail_pass(act):
            def per_block(blk, carry):
                act(zero_copy(blk * ROWS, ROWS))
                return carry

            lax.fori_loop(nu_ref[0], n_blocks, per_block, 0)

        for act in (lambda c: c.start(), lambda c: c.wait()):
            pad_pass(act)
            tail_pass(act)

    n = TL * SUBLANES
    for k in range(TOP_K):
        pltpu.make_async_copy(hn_ref, xb_hbm.at[pl.ds(0, n), :], sem).wait()


def _expert_kernel(be_ref, nu_ref, xs_ref, wgu_hbm, bgu_ref, wd_hbm, bd_ref, o_ref,
                   wgu_f32, wd_f32, wgu_bf, wd_bf, slot_ref, sems):
    b = pl.program_id(0)
    n_used = nu_ref[0]
    e = be_ref[b]
    prev = be_ref[jnp.maximum(b - 1, 0)]
    active = b < n_used

    def weight_copies(expert, slot):
        return (pltpu.make_async_copy(wgu_hbm.at[expert], wgu_f32.at[slot], sems.at[0, slot]),
                pltpu.make_async_copy(wd_hbm.at[expert], wd_f32.at[slot], sems.at[1, slot]))

    @pl.when(active & (b == 0))
    def _():
        slot_ref[0] = 0
        for c in weight_copies(e, 0):
            c.start()

    @pl.when(active & ((b == 0) | (e != prev)))
    def _():
        slot = slot_ref[0]
        for c in weight_copies(e, slot):
            c.wait()
        last = n_used - 1
        nxt = lax.while_loop(lambda j: (j <= last) & (be_ref[jnp.minimum(j, last)] == e), lambda j: j + 1, b + 1)

        @pl.when(nxt <= last)
        def _():
            for c in weight_copies(be_ref[jnp.minimum(nxt, last)], 1 - slot):
                c.start(priority=1)

        wgu_bf[...] = wgu_f32[slot].astype(BF16)
        wd_bf[...] = wd_f32[slot].astype(BF16)
        slot_ref[0] = 1 - slot

    @pl.when(active)
    def _():
        xb = _slab_to_rows(xs_ref, 0, ROWS).astype(BF16)
        gu = jnp.dot(xb, wgu_bf[...], preferred_element_type=F32) + bgu_ref[0]
        gate = jnp.minimum(gu[:, :D], SWIGLU_LIMIT)
        up = jnp.clip(gu[:, D:], -SWIGLU_LIMIT, SWIGLU_LIMIT)
        hidden = (up + 1.0) * gate * jax.nn.sigmoid(SWIGLU_ALPHA * gate)
        out = jnp.dot(hidden.astype(BF16), wd_bf[...], preferred_element_type=F32) + bd_ref[0]
        _rows_to_slab(o_ref, 0, out)

    @pl.when(jnp.logical_not(active))
    def _():
        o_ref[...] = jnp.zeros(o_ref.shape, F32)


def _combine_kernel(n_tiles, n_prompt_tiles, dest_ref, dest_next_ref, h_ref, gw_ref, gfin_ref, ob_hbm,
                    yp_ref, ys_ref, gath0, gath1, sems):
    i = pl.program_id(0)
    n = TOP_K * TL * SUBLANES

    def issue(table_ref, gath, sem):
        def body(tt, carry):
            for k in range(TOP_K):
                src = pl.multiple_of(table_ref[0, 0, k * TL + tt] * SUBLANES, SUBLANES)
                dst = pl.multiple_of((k * TL + tt) * SUBLANES, SUBLANES)
                pltpu.make_async_copy(ob_hbm.at[pl.ds(src, SUBLANES), :], gath.at[pl.ds(dst, SUBLANES), :],
                                      sem).start(priority=k % 2)
            return carry

        lax.fori_loop(0, TL, body, 0)

    def step(cur, cur_sem, nxt, nxt_sem):
        @pl.when(i + 1 < n_tiles)
        def _():
            issue(dest_next_ref, nxt, nxt_sem)

        pltpu.make_async_copy(ob_hbm.at[pl.ds(0, n), :], cur, cur_sem).wait()
        acc = h_ref[...]
        gw = gw_ref[...]
        for k in range(TOP_K):
            acc = acc + gw[:, k:k + 1] * _slab_to_rows(cur, k * TL, TL)
        y = _rmsnorm(acc, gfin_ref[...])
        is_prompt = i < n_prompt_tiles

        @pl.when(is_prompt)
        def _():
            yp_ref[...] = y

        @pl.when(jnp.logical_not(is_prompt))
        def _():
            ys_ref[...] = y

    @pl.when(i == 0)
    def _():
        issue(dest_ref, gath0, sems.at[0])

    parity = lax.rem(i, 2)

    @pl.when(parity == 0)
    def _():
        step(gath0, sems.at[0], gath1, sems.at[1])

    @pl.when(parity == 1)
    def _():
        step(gath1, sems.at[1], gath0, sems.at[0])


def _const_spec(shape):
    nd = len(shape)
    return pl.BlockSpec(shape, lambda *_: (0,) * nd, pipeline_mode=pl.Buffered(1))


def kernel(x_prompt, x_sample, state_conv_a, state_conv_b, g_mix, w_in, b_in, w_conv_a, w_out_a, w_conv_b, b_conv_b, ln_g, ln_b, w_out_b, b_out_b, w_o, g_ffn, w_router, b_router, w_gate_up, b_gate_up, w_down, b_down, g_final):
    depth = g_mix.shape[0]
    assert depth == 1
    bp, lp, d = x_prompt.shape
    bs, ls, _ = x_sample.shape
    assert d == D and w_in.shape[2] == 7 * D
    tp, ts = bp * lp, bs * ls
    t = tp + ts
    assert lp % TL == 0 and ls == SEG and ts % TL == 0 and t % ROUTE_CHUNK == 0 and ROUTE_CHUNK % TL == 0
    assert TAPS_B - 1 <= HIST <= SEG and TAPS_A - 1 <= HIST_A
    npt, nst = tp // TL, ts // TL
    nt = npt + nst
    tiles_per_seq = lp // TL
    n_blocks = (t * TOP_K) // ROWS + N_EXPERTS
    n_blocks_pad = -(-n_blocks // LANES) * LANES
    p_rows = n_blocks * ROWS

    row = lambda a: a.reshape(1, -1)
    f32_spec = lambda: _const_spec((1, D))

    mixer = pl.pallas_call(
        functools.partial(_mixer_kernel, npt, tiles_per_seq, bp, bs),
        grid=(nt,),
        in_specs=[
            pl.BlockSpec((TL, D), lambda i: (jnp.minimum(i, npt - 1), 0)),
            pl.BlockSpec((TL, D), lambda i: (jnp.maximum(i - npt, 0), 0)),
            pl.BlockSpec((NSEG, TAPS_A - 1, D), lambda i: (jnp.maximum(i - npt, 0), 0, 0)),
            pl.BlockSpec((NSEG, TAPS_B - 1, D), lambda i: (jnp.maximum(i - npt, 0), 0, 0)),
            f32_spec(),
            _const_spec((D, 7 * D)),
            _const_spec((1, 7 * D)),
            _const_spec((TAPS_A, D)),
            _const_spec((D, D)),
            _const_spec((TAPS_B, SUBLANES, LANES)),
            f32_spec(), f32_spec(), f32_spec(),
            _const_spec((D, D)),
            f32_spec(),
            _const_spec((D, D)),
            f32_spec(),
            _const_spec((N_EXPERTS, D)),
            _const_spec((N_EXPERTS, 1)),
        ],
        out_specs=[
            pl.BlockSpec((TL, D), lambda i: (i, 0)),
            pl.BlockSpec((TL * SUBLANES, LANES), lambda i: (i, 0)),
            pl.BlockSpec((TOP_K, TL), lambda i: (0, i)),
            pl.BlockSpec((TOP_K, TL), lambda i: (0, i)),
            pl.BlockSpec((bp, TAPS_A - 1, D), lambda i: (0, 0, 0)),
            pl.BlockSpec((bp, TAPS_B - 1, D), lambda i: (0, 0, 0)),
            pl.BlockSpec((bs, TAPS_A - 1, D), lambda i: (0, 0, 0)),
            pl.BlockSpec((bs, TAPS_B - 1, D), lambda i: (0, 0, 0)),
        ],
        out_shape=[
            jax.ShapeDtypeStruct((t, D), F32),
            jax.ShapeDtypeStruct((t * SUBLANES, LANES), F32),
            jax.ShapeDtypeStruct((TOP_K, t), I32),
            jax.ShapeDtypeStruct((TOP_K, t), F32),
            jax.ShapeDtypeStruct((bp, TAPS_A - 1, D), F32),
            jax.ShapeDtypeStruct((bp, TAPS_B - 1, D), F32),
            jax.ShapeDtypeStruct((bs, TAPS_A - 1, D), F32),
            jax.ShapeDtypeStruct((bs, TAPS_B - 1, D), F32),
        ],
        scratch_shapes=[
            pltpu.VMEM((NSEG * (HIST_A + SEG), D), F32),
            pltpu.VMEM((HIST_A, D), F32),
            pltpu.VMEM(((HIST + TL + NSEG * HIST) * SUBLANES, LANES), F32),
            pltpu.VMEM((TL * SUBLANES, LANES), F32),
            pltpu.VMEM((bp + bs + 1, TAPS_A - 1, D), F32),
            pltpu.VMEM((bp + bs + 1, TAPS_B - 1, D), F32),
        ],
        compiler_params=pltpu.CompilerParams(dimension_semantics=("arbitrary",), vmem_limit_bytes=VMEM_LIMIT),
        name="mixer",
    )
    h, hn_slab, idx_t, gw_t, na_p, nb_p, na_s, nb_s = mixer(
        x_prompt.reshape(tp, D), x_sample.reshape(ts, D), state_conv_a[0], state_conv_b[0],
        row(g_mix), w_in[0].astype(BF16), row(b_in), w_conv_a[0],
        w_out_a[0].astype(BF16), w_conv_b[0].reshape(TAPS_B, SUBLANES, LANES), row(b_conv_b), row(ln_g), row(ln_b),
        w_out_b[0].astype(BF16), row(b_out_b), w_o[0].astype(BF16), row(g_ffn),
        w_router[0].T, b_router[0].reshape(N_EXPERTS, 1))

    dest_sm, block_e, n_used, pad_start, pad_len = pl.pallas_call(
        functools.partial(_route_kernel, t, n_blocks_pad),
        out_shape=[
            jax.ShapeDtypeStruct((nt, 1, TOP_K * TL), I32),
            jax.ShapeDtypeStruct((1, n_blocks_pad), I32),
            jax.ShapeDtypeStruct((1, LANES), I32),
            jax.ShapeDtypeStruct((N_EXPERTS, LANES), I32),
            jax.ShapeDtypeStruct((N_EXPERTS, LANES), I32),
        ],
        scratch_shapes=[pltpu.VMEM((ROUTE_CHUNK, ROUTE_CHUNK), BF16)],
        compiler_params=pltpu.CompilerParams(vmem_limit_bytes=VMEM_LIMIT),
        name="route",
    )(idx_t)
    block_e = block_e[0, :n_blocks]
    n_used = n_used[0, :1]
    pad_start, pad_len = pad_start[:, 0], pad_len[:, 0]

    smem_tile_spec = pl.BlockSpec((1, 1, TOP_K * TL), lambda i: (i, 0, 0), memory_space=pltpu.SMEM)
    any_spec = pl.BlockSpec(memory_space=pl.ANY)
    x_buf = pl.pallas_call(
        functools.partial(_dispatch_kernel, n_blocks),
        grid_spec=pltpu.PrefetchScalarGridSpec(
            num_scalar_prefetch=3,
            grid=(nt,),
            in_specs=[
                pl.BlockSpec((1, 1, TOP_K * TL), lambda i, *_: (i, 0, 0), memory_space=pltpu.SMEM),
                pl.BlockSpec((TL * SUBLANES, LANES), lambda i, *_: (i, 0)),
            ],
            out_specs=any_spec,
            scratch_shapes=[pltpu.VMEM((ROWS * SUBLANES, LANES), F32), pltpu.SemaphoreType.DMA,
                            pltpu.SemaphoreType.DMA],
        ),
        out_shape=jax.ShapeDtypeStruct((p_rows * SUBLANES, LANES), F32),
        compiler_params=pltpu.CompilerParams(dimension_semantics=("arbitrary",)),
        name="dispatch",
    )(pad_start, pad_len, n_used, dest_sm, hn_slab)

    def blk(b, be, nu):
        return jnp.maximum(jnp.minimum(b, nu[0] - 1), 0)

    out_buf = pl.pallas_call(
        _expert_kernel,
        grid_spec=pltpu.PrefetchScalarGridSpec(
            num_scalar_prefetch=2,
            grid=(n_blocks,),
            in_specs=[
                pl.BlockSpec((ROWS * SUBLANES, LANES), lambda b, be, nu: (blk(b, be, nu), 0)),
                any_spec,
                pl.BlockSpec((1, 1, 2 * D), lambda b, be, nu: (be[blk(b, be, nu)], 0, 0)),
                any_spec,
                pl.BlockSpec((1, 1, D), lambda b, be, nu: (be[blk(b, be, nu)], 0, 0)),
            ],
            out_specs=pl.BlockSpec((ROWS * SUBLANES, LANES), lambda b, be, nu: (b, 0)),
            scratch_shapes=[
                pltpu.VMEM((2, D, 2 * D), F32), pltpu.VMEM((2, D, D), F32),
                pltpu.VMEM((D, 2 * D), BF16), pltpu.VMEM((D, D), BF16),
                pltpu.SMEM((1,), I32), pltpu.SemaphoreType.DMA((2, 2)),
            ],
        ),
        out_shape=jax.ShapeDtypeStruct((p_rows * SUBLANES, LANES), F32),
        compiler_params=pltpu.CompilerParams(dimension_semantics=("arbitrary",), vmem_limit_bytes=VMEM_LIMIT),
        name="experts",
    )(block_e, n_used, x_buf, w_gate_up[0], b_gate_up[0].reshape(N_EXPERTS, 1, 2 * D), w_down[0],
      b_down[0].reshape(N_EXPERTS, 1, D))

    y_p, y_s = pl.pallas_call(
        functools.partial(_combine_kernel, nt, npt),
        grid=(nt,),
        in_specs=[
            smem_tile_spec,
            pl.BlockSpec((1, 1, TOP_K * TL), lambda i: (jnp.minimum(i + 1, nt - 1), 0, 0), memory_space=pltpu.SMEM),
            pl.BlockSpec((TL, D), lambda i: (i, 0)),
            pl.BlockSpec((TL, TOP_K), lambda i: (i, 0)),
            pl.BlockSpec((1, D), lambda i: (0, 0)),
            any_spec,
        ],
        out_specs=[
            pl.BlockSpec((TL, D), lambda i: (jnp.minimum(i, npt - 1), 0)),
            pl.BlockSpec((TL, D), lambda i: (jnp.maximum(i - npt, 0), 0)),
        ],
        out_shape=[jax.ShapeDtypeStruct((tp, D), F32), jax.ShapeDtypeStruct((ts, D), F32)],
        scratch_shapes=[pltpu.VMEM((TOP_K * TL * SUBLANES, LANES), F32), pltpu.VMEM((TOP_K * TL * SUBLANES, LANES), F32),
                        pltpu.SemaphoreType.DMA((2,))],
        compiler_params=pltpu.CompilerParams(dimension_semantics=("arbitrary",), vmem_limit_bytes=VMEM_LIMIT),
        name="combine",
    )(dest_sm, dest_sm, h, gw_t.T, row(g_final), out_buf)

    return (y_p.reshape(bp, lp, D), y_s.reshape(bs, ls, D), na_p[None], nb_p[None], na_s[None], nb_s[None])
```

```python
import functools

import jax
import jax.numpy as jnp
from jax import lax
from jax.experimental import pallas as pl
from jax.experimental.pallas import tpu as pltpu

F32 = jnp.float32
BF16 = jnp.bfloat16
I32 = jnp.int32

D = 1024
LANES = 128
SUBLANES = 8
NCHUNK = D // LANES
N_EXPERTS = 32
TOP_K = 4
TAPS_A = 3
TAPS_B = 31
EPS = 1e-5
SWIGLU_LIMIT = 7.0
SWIGLU_ALPHA = 1.702

TL = 256
SEG = 32
NSEG = TL // SEG
HIST = 32
HIST_A = 8
CONV_BLOCK = 4
ROWS = 256
ISSUE_UNROLL = 4
ROUTE_CHUNK = 512
VMEM_LIMIT = 56 * 1024 * 1024


def _rows_to_slab(dst_ref, slot0, val):
    n = val.shape[0]
    for c in range(NCHUNK):
        dst_ref[pl.ds(SUBLANES * slot0 + c, n, stride=SUBLANES), :] = val[:, c * LANES:(c + 1) * LANES]


def _slab_to_rows(src_ref, slot0, n):
    return jnp.concatenate(
        [src_ref[pl.ds(SUBLANES * slot0 + c, n, stride=SUBLANES), :] for c in range(NCHUNK)], axis=1)


def _rmsnorm(x, g):
    return x * lax.rsqrt(jnp.mean(x * x, axis=-1, keepdims=True) + EPS) * g


def _conv_b_slab(vbuf, hist_rows, w_ref, dst_ref):
    first_tap = HIST - (TAPS_B - 1)
    for s in range(NSEG):
        cur_row0 = (HIST + s * SEG) * SUBLANES
        for t0 in range(0, SEG, CONV_BLOCK):
            ins = []
            for k in range(CONV_BLOCK + TAPS_B - 1):
                u = t0 + first_tap + k
                if u < HIST:
                    row = pl.multiple_of(hist_rows[s] + u * SUBLANES, SUBLANES)
                    ins.append(vbuf[pl.ds(row, SUBLANES), :])
                else:
                    row = cur_row0 + (u - HIST) * SUBLANES
                    ins.append(vbuf[row:row + SUBLANES, :])
            accs = [None] * CONV_BLOCK
            for j in range(TAPS_B):
                wj = w_ref[j]
                for o in range(CONV_BLOCK):
                    term = wj * ins[o + j]
                    accs[o] = term if accs[o] is None else accs[o] + term
            for o in range(CONV_BLOCK):
                row = (s * SEG + t0 + o) * SUBLANES
                dst_ref[row:row + SUBLANES, :] = accs[o]


def _mixer_kernel(n_prompt_tiles, tiles_per_seq, n_prompt_seq, n_sample_seq,
                  xp_ref, xs_ref, sta_ref, stb_ref, gmix_ref, win_ref, bin_ref, wca_ref, woa_ref,
                  wcb_ref, bcb_ref, lng_ref, lnb_ref, wob_ref, bob_ref, wo_ref, gffn_ref, wrt_ref, brt_ref,
                  h_ref, hn_ref, idx_ref, gw_ref, nap_ref, nbp_ref, nas_ref, nbs_ref,
                  cubuf, carry_a, vbuf, ybslab, sa_scr, sb_scr):
    i = pl.program_id(0)
    n_tiles = pl.num_programs(0)
    is_prompt = i < n_prompt_tiles
    seq_start = is_prompt & (lax.rem(i, tiles_per_seq) == 0)
    carry_rows = HIST * SUBLANES

    @pl.when(i == 0)
    def _():
        vbuf[0:carry_rows, :] = jnp.zeros((carry_rows, LANES), F32)
        carry_a[...] = jnp.zeros(carry_a.shape, F32)

    x = jnp.where(is_prompt, xp_ref[...], xs_ref[...])
    n_bf = _rmsnorm(x, gmix_ref[...]).astype(BF16)

    def proj(g):
        cols = slice(g * D, (g + 1) * D)
        return jnp.dot(n_bf, win_ref[:, cols], preferred_element_type=F32) + bin_ref[:, cols]

    cu = proj(1) * proj(2)
    v = proj(3) * jax.nn.sigmoid(proj(4))

    seg_rows = HIST_A + SEG
    prev_a = jnp.where(seq_start, 0.0, carry_a[HIST_A - (TAPS_A - 1):HIST_A, :])
    for s in range(NSEG):
        r0 = s * SEG
        prev = prev_a if s == 0 else cu[r0 - (TAPS_A - 1):r0, :]
        cubuf[s * seg_rows + HIST_A - (TAPS_A - 1):s * seg_rows + HIST_A, :] = jnp.where(is_prompt, prev, sta_ref[s])
        cubuf[s * seg_rows + HIST_A:(s + 1) * seg_rows, :] = cu[r0:r0 + SEG, :]
    carry_a[HIST_A - (TAPS_A - 1):HIST_A, :] = cu[TL - (TAPS_A - 1):TL, :]
    ya_parts = []
    for s in range(NSEG):
        acc = None
        for j in range(TAPS_A):
            lo = s * seg_rows + HIST_A - (TAPS_A - 1) + j
            term = wca_ref[j:j + 1, :] * cubuf[lo:lo + SEG, :]
            acc = term if acc is None else acc + term
        ya_parts.append(acc)
    ya = jnp.concatenate(ya_parts, axis=0)

    vbuf[0:carry_rows, :] = jnp.where(seq_start, 0.0, vbuf[0:carry_rows, :])
    _rows_to_slab(vbuf, HIST, v)
    state_slot0 = HIST + TL
    for s in range(NSEG):
        for c in range(NCHUNK):
            vbuf[pl.ds(SUBLANES * (state_slot0 + s * HIST + HIST - (TAPS_B - 1)) + c, TAPS_B - 1, stride=SUBLANES), :] = (
                stb_ref[s, :, c * LANES:(c + 1) * LANES])
    hist_rows = [jnp.where(is_prompt, s * SEG * SUBLANES, (state_slot0 + s * HIST) * SUBLANES) for s in range(NSEG)]
    _conv_b_slab(vbuf, hist_rows, wcb_ref, ybslab)
    vbuf[0:carry_rows, :] = vbuf[TL * SUBLANES:(TL + HIST) * SUBLANES, :]

    dummy = n_prompt_seq + n_sample_seq
    for s in range(NSEG):
        prompt_slot = i // tiles_per_seq if s == NSEG - 1 else dummy
        slot = jnp.where(is_prompt, prompt_slot, n_prompt_seq + (i - n_prompt_tiles) * NSEG + s)
        r1 = (s + 1) * SEG
        sa_scr[slot] = cu[r1 - (TAPS_A - 1):r1, :]
        sb_scr[slot] = v[r1 - (TAPS_B - 1):r1, :]

    @pl.when(i == n_tiles - 1)
    def _():
        nap_ref[...] = sa_scr[0:n_prompt_seq]
        nbp_ref[...] = sb_scr[0:n_prompt_seq]
        nas_ref[...] = sa_scr[n_prompt_seq:n_prompt_seq + n_sample_seq]
        nbs_ref[...] = sb_scr[n_prompt_seq:n_prompt_seq + n_sample_seq]

    yb = _slab_to_rows(ybslab, 0, TL) + bcb_ref[...]

    out_a = jnp.dot((proj(0) * ya).astype(BF16), woa_ref[...], preferred_element_type=F32)
    mu = jnp.mean(yb, axis=-1, keepdims=True)
    yc = yb - mu
    var = jnp.mean(yc * yc, axis=-1, keepdims=True)
    ln = yc * lax.rsqrt(var + EPS) * lng_ref[...] + lnb_ref[...]
    act = ln * jax.nn.sigmoid(ln)
    out_b = jnp.dot(act.astype(BF16), wob_ref[...], preferred_element_type=F32) + bob_ref[...]
    mix = jax.nn.sigmoid(proj(5)) * out_a + jax.nn.sigmoid(proj(6)) * out_b
    h = x + jnp.dot(mix.astype(BF16), wo_ref[...], preferred_element_type=F32)
    h_ref[...] = h
    hn = _rmsnorm(h, gffn_ref[...])
    _rows_to_slab(hn_ref, 0, hn)

    logits = lax.dot_general(wrt_ref[...], hn.astype(BF16), (((1,), (1,)), ((), ())),
                             preferred_element_type=F32) + brt_ref[...]
    iota_e = lax.broadcasted_iota(I32, (N_EXPERTS, TL), 0).astype(F32)
    vals, idxs = [], []
    for _ in range(TOP_K):
        m = jnp.max(logits, axis=0, keepdims=True)
        sel = jnp.min(jnp.where(logits == m, iota_e, float(N_EXPERTS)), axis=0, keepdims=True)
        vals.append(m)
        idxs.append(sel)
        logits = jnp.where(iota_e == sel, -jnp.inf, logits)
    ex = [jnp.exp(val - vals[0]) for val in vals]
    denom = ex[0] + ex[1] + ex[2] + ex[3]
    for k in range(TOP_K):
        idx_ref[k:k + 1, :] = idxs[k].astype(I32)
        gw_ref[k:k + 1, :] = ex[k] / denom


def _route_kernel(n_tokens, n_blocks_pad, idx_ref, dest_ref, be_ref, nu_ref, pad_start_ref, pad_len_ref, tri_scr):
    n_chunks = n_tokens // ROUTE_CHUNK
    iota_e = lax.broadcasted_iota(I32, (N_EXPERTS, ROUTE_CHUNK), 0)

    def masks(c):
        sl = pl.ds(pl.multiple_of(c * ROUTE_CHUNK, ROUTE_CHUNK), ROUTE_CHUNK)
        return sl, [idx_ref[k:k + 1, sl] == iota_e for k in range(TOP_K)]

    def onehot_sum(ms):
        tot = ms[0].astype(F32)
        for k in range(1, TOP_K):
            tot = tot + ms[k].astype(F32)
        return tot

    def count_body(c, acc):
        _, ms = masks(c)
        return acc + onehot_sum(ms)

    acc = lax.fori_loop(0, n_chunks, count_body, jnp.zeros((N_EXPERTS, ROUTE_CHUNK), F32))
    counts = jnp.sum(acc, axis=1, keepdims=True)
    nblk = jnp.floor((counts + (ROWS - 1)) * (1.0 / ROWS))
    r = lax.broadcasted_iota(I32, (N_EXPERTS, N_EXPERTS), 0)
    cidx = lax.broadcasted_iota(I32, (N_EXPERTS, N_EXPERTS), 1)
    lower = (cidx <= r).astype(F32)
    pend = jnp.dot(lower, jnp.broadcast_to(nblk, (N_EXPERTS, LANES)),
                   precision=lax.Precision.HIGHEST, preferred_element_type=F32)
    pend1 = pend[:, 0:1]
    pstart_rows = (pend1 - nblk) * ROWS
    pad_start_ref[...] = jnp.broadcast_to(pstart_rows + counts, (N_EXPERTS, LANES)).astype(I32)
    pad_len_ref[...] = jnp.broadcast_to(nblk * ROWS - counts, (N_EXPERTS, LANES)).astype(I32)

    bvec = lax.broadcasted_iota(I32, (N_EXPERTS, n_blocks_pad), 1).astype(F32)
    be = jnp.sum((pend1 <= bvec).astype(F32), axis=0, keepdims=True)
    be_ref[...] = jnp.minimum(be, N_EXPERTS - 1).astype(I32)
    last = lax.broadcasted_iota(I32, (N_EXPERTS, LANES), 0) == N_EXPERTS - 1
    nu_ref[...] = jnp.sum(jnp.where(last, pend, 0.0), axis=0, keepdims=True).astype(I32)

    tr = lax.broadcasted_iota(I32, (ROUTE_CHUNK, ROUTE_CHUNK), 0)
    tc = lax.broadcasted_iota(I32, (ROUTE_CHUNK, ROUTE_CHUNK), 1)
    tri_scr[...] = (tr < tc).astype(BF16)

    def dest_body(c, carry):
        sl, ms = masks(c)
        tot = onehot_sum(ms)
        before = jnp.dot(tot.astype(BF16), tri_scr[...], preferred_element_type=F32)
        pos = pstart_rows + carry + before
        for k in range(TOP_K):
            dest_k = jnp.sum(jnp.where(ms[k], pos, 0.0), axis=0, keepdims=True).astype(I32)
            for part in range(ROUTE_CHUNK // TL):
                dest_ref[c * (ROUTE_CHUNK // TL) + part, :, k * TL:(k + 1) * TL] = dest_k[:, part * TL:(part + 1) * TL]
        return carry + jnp.sum(tot, axis=1, keepdims=True)

    lax.fori_loop(0, n_chunks, dest_body, jnp.zeros((N_EXPERTS, 1), F32))


def _dispatch_kernel(n_blocks, pad_start_ref, pad_len_ref, nu_ref, dest_ref, hn_ref, xb_hbm, ztile, sem, pad_sem):
    def row_copy(tt, k):
        src = pl.multiple_of(tt * SUBLANES, SUBLANES)
        dst = pl.multiple_of(dest_ref[0, 0, k * TL + tt] * SUBLANES, SUBLANES)
        return pltpu.make_async_copy(hn_ref.at[pl.ds(src, SUBLANES), :], xb_hbm.at[pl.ds(dst, SUBLANES), :], sem)

    def body(tt, carry):
        for k in range(TOP_K):
            row_copy(tt, k).start(priority=k % 2)
        return carry

    lax.fori_loop(0, TL, body, 0)

    @pl.when(pl.program_id(0) == 0)
    def _():
        ztile[...] = jnp.zeros(ztile.shape, F32)

        def zero_copy(row, n_slabs):
            dst = pl.multiple_of(row * SUBLANES, SUBLANES)
            return pltpu.make_async_copy(ztile.at[pl.ds(0, n_slabs * SUBLANES), :],
                                         xb_hbm.at[pl.ds(dst, n_slabs * SUBLANES), :], pad_sem)

        def pad_pass(act):
            def per_expert(e, carry):
                row, length = pad_start_ref[e], pad_len_ref[e]
                for bit in reversed(range(ROWS.bit_length() - 1)):
                    take = (length & (1 << bit)) != 0

                    @pl.when(take)
                    def _():
                        act(zero_copy(row, 1 << bit))

                    row = row + jnp.where(take, 1 << bit, 0)
                return carry

            lax.fori_loop(0, N_EXPERTS, per_expert, 0)

        def tail_pass(act):
            def per_block(blk, carry):
                act(zero_copy(blk * ROWS, ROWS))
                return carry

            lax.fori_loop(nu_ref[0], n_blocks, per_block, 0)

        for act in (lambda c: c.start(), lambda c: c.wait()):
            pad_pass(act)
            tail_pass(act)

    n = TL * SUBLANES
    for k in range(TOP_K):
        pltpu.make_async_copy(hn_ref, xb_hbm.at[pl.ds(0, n), :], sem).wait()


def _expert_kernel(be_ref, nu_ref, xs_ref, wgu_hbm, bgu_ref, wd_hbm, bd_ref, o_ref,
                   wgu_f32, wd_f32, wgu_bf, wd_bf, slot_ref, sems):
    b = pl.program_id(0)
    n_used = nu_ref[0]
    e = be_ref[b]
    prev = be_ref[jnp.maximum(b - 1, 0)]
    active = b < n_used

    def weight_copies(expert, slot):
        return (pltpu.make_async_copy(wgu_hbm.at[expert], wgu_f32.at[slot], sems.at[0, slot]),
                pltpu.make_async_copy(wd_hbm.at[expert], wd_f32.at[slot], sems.at[1, slot]))

    @pl.when(active & (b == 0))
    def _():
        slot_ref[0] = 0
        for c in weight_copies(e, 0):
            c.start()

    @pl.when(active & ((b == 0) | (e != prev)))
    def _():
        slot = slot_ref[0]
        for c in weight_copies(e, slot):
            c.wait()
        last = n_used - 1
        nxt = lax.while_loop(lambda j: (j <= last) & (be_ref[jnp.minimum(j, last)] == e), lambda j: j + 1, b + 1)

        @pl.when(nxt <= last)
        def _():
            for c in weight_copies(be_ref[jnp.minimum(nxt, last)], 1 - slot):
                c.start(priority=1)

        wgu_bf[...] = wgu_f32[slot].astype(BF16)
        wd_bf[...] = wd_f32[slot].astype(BF16)
        slot_ref[0] = 1 - slot

    @pl.when(active)
    def _():
        xb = _slab_to_rows(xs_ref, 0, ROWS).astype(BF16)
        gu = jnp.dot(xb, wgu_bf[...], preferred_element_type=F32) + bgu_ref[0]
        gate = jnp.minimum(gu[:, :D], SWIGLU_LIMIT)
        up = jnp.clip(gu[:, D:], -SWIGLU_LIMIT, SWIGLU_LIMIT)
        hidden = (up + 1.0) * gate * jax.nn.sigmoid(SWIGLU_ALPHA * gate)
        out = jnp.dot(hidden.astype(BF16), wd_bf[...], preferred_element_type=F32) + bd_ref[0]
        _rows_to_slab(o_ref, 0, out)

    @pl.when(jnp.logical_not(active))
    def _():
        o_ref[...] = jnp.zeros(o_ref.shape, F32)


def _combine_kernel(n_tiles, n_prompt_tiles, dest_ref, dest_next_ref, h_ref, gw_ref, gfin_ref, ob_hbm,
                    yp_ref, ys_ref, gath0, gath1, sems):
    i = pl.program_id(0)
    n = TOP_K * TL * SUBLANES

    def issue(table_ref, gath, sem):
        def body(q, carry):
            for u in range(ISSUE_UNROLL):
                tt = q * ISSUE_UNROLL + u
                for k in range(TOP_K):
                    src = pl.multiple_of(table_ref[0, 0, k * TL + tt] * SUBLANES, SUBLANES)
                    dst = pl.multiple_of((k * TL + tt) * SUBLANES, SUBLANES)
                    pltpu.make_async_copy(ob_hbm.at[pl.ds(src, SUBLANES), :], gath.at[pl.ds(dst, SUBLANES), :],
                                          sem).start(priority=k % 2)
            return carry

        lax.fori_loop(0, TL // ISSUE_UNROLL, body, 0)

    def step(cur, cur_sem, nxt, nxt_sem):
        @pl.when(i + 1 < n_tiles)
        def _():
            issue(dest_next_ref, nxt, nxt_sem)

        pltpu.make_async_copy(ob_hbm.at[pl.ds(0, n), :], cur, cur_sem).wait()
        acc = h_ref[...]
        gw = gw_ref[...]
        for k in range(TOP_K):
            acc = acc + gw[:, k:k + 1] * _slab_to_rows(cur, k * TL, TL)
        y = _rmsnorm(acc, gfin_ref[...])
        is_prompt = i < n_prompt_tiles

        @pl.when(is_prompt)
        def _():
            yp_ref[...] = y

        @pl.when(jnp.logical_not(is_prompt))
        def _():
            ys_ref[...] = y

    @pl.when(i == 0)
    def _():
        issue(dest_ref, gath0, sems.at[0])

    parity = lax.rem(i, 2)

    @pl.when(parity == 0)
    def _():
        step(gath0, sems.at[0], gath1, sems.at[1])

    @pl.when(parity == 1)
    def _():
        step(gath1, sems.at[1], gath0, sems.at[0])


def _const_spec(shape):
    nd = len(shape)
    return pl.BlockSpec(shape, lambda *_: (0,) * nd, pipeline_mode=pl.Buffered(1))


def kernel(x_prompt, x_sample, state_conv_a, state_conv_b, g_mix, w_in, b_in, w_conv_a, w_out_a, w_conv_b, b_conv_b, ln_g, ln_b, w_out_b, b_out_b, w_o, g_ffn, w_router, b_router, w_gate_up, b_gate_up, w_down, b_down, g_final):
    depth = g_mix.shape[0]
    assert depth == 1
    bp, lp, d = x_prompt.shape
    bs, ls, _ = x_sample.shape
    assert d == D and w_in.shape[2] == 7 * D
    tp, ts = bp * lp, bs * ls
    t = tp + ts
    assert lp % TL == 0 and ls == SEG and ts % TL == 0 and t % ROUTE_CHUNK == 0 and ROUTE_CHUNK % TL == 0
    assert TAPS_B - 1 <= HIST <= SEG and TAPS_A - 1 <= HIST_A
    npt, nst = tp // TL, ts // TL
    nt = npt + nst
    tiles_per_seq = lp // TL
    n_blocks = (t * TOP_K) // ROWS + N_EXPERTS
    n_blocks_pad = -(-n_blocks // LANES) * LANES
    p_rows = n_blocks * ROWS

    row = lambda a: a.reshape(1, -1)
    f32_spec = lambda: _const_spec((1, D))

    mixer = pl.pallas_call(
        functools.partial(_mixer_kernel, npt, tiles_per_seq, bp, bs),
        grid=(nt,),
        in_specs=[
            pl.BlockSpec((TL, D), lambda i: (jnp.minimum(i, npt - 1), 0)),
            pl.BlockSpec((TL, D), lambda i: (jnp.maximum(i - npt, 0), 0)),
            pl.BlockSpec((NSEG, TAPS_A - 1, D), lambda i: (jnp.maximum(i - npt, 0), 0, 0)),
            pl.BlockSpec((NSEG, TAPS_B - 1, D), lambda i: (jnp.maximum(i - npt, 0), 0, 0)),
            f32_spec(),
            _const_spec((D, 7 * D)),
            _const_spec((1, 7 * D)),
            _const_spec((TAPS_A, D)),
            _const_spec((D, D)),
            _const_spec((TAPS_B, SUBLANES, LANES)),
            f32_spec(), f32_spec(), f32_spec(),
            _const_spec((D, D)),
            f32_spec(),
            _const_spec((D, D)),
            f32_spec(),
            _const_spec((N_EXPERTS, D)),
            _const_spec((N_EXPERTS, 1)),
        ],
        out_specs=[
            pl.BlockSpec((TL, D), lambda i: (i, 0)),
            pl.BlockSpec((TL * SUBLANES, LANES), lambda i: (i, 0)),
            pl.BlockSpec((TOP_K, TL), lambda i: (0, i)),
            pl.BlockSpec((TOP_K, TL), lambda i: (0, i)),
            pl.BlockSpec((bp, TAPS_A - 1, D), lambda i: (0, 0, 0)),
            pl.BlockSpec((bp, TAPS_B - 1, D), lambda i: (0, 0, 0)),
            pl.BlockSpec((bs, TAPS_A - 1, D), lambda i: (0, 0, 0)),
            pl.BlockSpec((bs, TAPS_B - 1, D), lambda i: (0, 0, 0)),
        ],
        out_shape=[
            jax.ShapeDtypeStruct((t, D), F32),
            jax.ShapeDtypeStruct((t * SUBLANES, LANES), F32),
            jax.ShapeDtypeStruct((TOP_K, t), I32),
            jax.ShapeDtypeStruct((TOP_K, t), F32),
            jax.ShapeDtypeStruct((bp, TAPS_A - 1, D), F32),
            jax.ShapeDtypeStruct((bp, TAPS_B - 1, D), F32),
            jax.ShapeDtypeStruct((bs, TAPS_A - 1, D), F32),
            jax.ShapeDtypeStruct((bs, TAPS_B - 1, D), F32),
        ],
        scratch_shapes=[
            pltpu.VMEM((NSEG * (HIST_A + SEG), D), F32),
            pltpu.VMEM((HIST_A, D), F32),
            pltpu.VMEM(((HIST + TL + NSEG * HIST) * SUBLANES, LANES), F32),
            pltpu.VMEM((TL * SUBLANES, LANES), F32),
            pltpu.VMEM((bp + bs + 1, TAPS_A - 1, D), F32),
            pltpu.VMEM((bp + bs + 1, TAPS_B - 1, D), F32),
        ],
        compiler_params=pltpu.CompilerParams(dimension_semantics=("arbitrary",), vmem_limit_bytes=VMEM_LIMIT),
        name="mixer",
    )
    h, hn_slab, idx_t, gw_t, na_p, nb_p, na_s, nb_s = mixer(
        x_prompt.reshape(tp, D), x_sample.reshape(ts, D), state_conv_a[0], state_conv_b[0],
        row(g_mix), w_in[0].astype(BF16), row(b_in), w_conv_a[0],
        w_out_a[0].astype(BF16), w_conv_b[0].reshape(TAPS_B, SUBLANES, LANES), row(b_conv_b), row(ln_g), row(ln_b),
        w_out_b[0].astype(BF16), row(b_out_b), w_o[0].astype(BF16), row(g_ffn),
        w_router[0].T.astype(BF16), b_router[0].reshape(N_EXPERTS, 1))

    dest_sm, block_e, n_used, pad_start, pad_len = pl.pallas_call(
        functools.partial(_route_kernel, t, n_blocks_pad),
        out_shape=[
            jax.ShapeDtypeStruct((nt, 1, TOP_K * TL), I32),
            jax.ShapeDtypeStruct((1, n_blocks_pad), I32),
            jax.ShapeDtypeStruct((1, LANES), I32),
            jax.ShapeDtypeStruct((N_EXPERTS, LANES), I32),
            jax.ShapeDtypeStruct((N_EXPERTS, LANES), I32),
        ],
        scratch_shapes=[pltpu.VMEM((ROUTE_CHUNK, ROUTE_CHUNK), BF16)],
        compiler_params=pltpu.CompilerParams(vmem_limit_bytes=VMEM_LIMIT),
        name="route",
    )(idx_t)
    block_e = block_e[0, :n_blocks]
    n_used = n_used[0, :1]
    pad_start, pad_len = pad_start[:, 0], pad_len[:, 0]

    smem_tile_spec = pl.BlockSpec((1, 1, TOP_K * TL), lambda i: (i, 0, 0), memory_space=pltpu.SMEM)
    any_spec = pl.BlockSpec(memory_space=pl.ANY)
    x_buf = pl.pallas_call(
        functools.partial(_dispatch_kernel, n_blocks),
        grid_spec=pltpu.PrefetchScalarGridSpec(
            num_scalar_prefetch=3,
            grid=(nt,),
            in_specs=[
                pl.BlockSpec((1, 1, TOP_K * TL), lambda i, *_: (i, 0, 0), memory_space=pltpu.SMEM),
                pl.BlockSpec((TL * SUBLANES, LANES), lambda i, *_: (i, 0)),
            ],
            out_specs=any_spec,
            scratch_shapes=[pltpu.VMEM((ROWS * SUBLANES, LANES), F32), pltpu.SemaphoreType.DMA,
                            pltpu.SemaphoreType.DMA],
        ),
        out_shape=jax.ShapeDtypeStruct((p_rows * SUBLANES, LANES), F32),
        compiler_params=pltpu.CompilerParams(dimension_semantics=("arbitrary",)),
        name="dispatch",
    )(pad_start, pad_len, n_used, dest_sm, hn_slab)

    def blk(b, be, nu):
        return jnp.maximum(jnp.minimum(b, nu[0] - 1), 0)

    out_buf = pl.pallas_call(
        _expert_kernel,
        grid_spec=pltpu.PrefetchScalarGridSpec(
            num_scalar_prefetch=2,
            grid=(n_blocks,),
            in_specs=[
                pl.BlockSpec((ROWS * SUBLANES, LANES), lambda b, be, nu: (blk(b, be, nu), 0)),
                any_spec,
                pl.BlockSpec((1, 1, 2 * D), lambda b, be, nu: (be[blk(b, be, nu)], 0, 0)),
                any_spec,
                pl.BlockSpec((1, 1, D), lambda b, be, nu: (be[blk(b, be, nu)], 0, 0)),
            ],
            out_specs=pl.BlockSpec((ROWS * SUBLANES, LANES), lambda b, be, nu: (b, 0)),
            scratch_shapes=[
                pltpu.VMEM((2, D, 2 * D), F32), pltpu.VMEM((2, D, D), F32),
                pltpu.VMEM((D, 2 * D), BF16), pltpu.VMEM((D, D), BF16),
                pltpu.SMEM((1,), I32), pltpu.SemaphoreType.DMA((2, 2)),
            ],
        ),
        out_shape=jax.ShapeDtypeStruct((p_rows * SUBLANES, LANES), F32),
        compiler_params=pltpu.CompilerParams(dimension_semantics=("arbitrary",), vmem_limit_bytes=VMEM_LIMIT),
        name="experts",
    )(block_e, n_used, x_buf, w_gate_up[0], b_gate_up[0].reshape(N_EXPERTS, 1, 2 * D), w_down[0],
      b_down[0].reshape(N_EXPERTS, 1, D))

    y_p, y_s = pl.pallas_call(
        functools.partial(_combine_kernel, nt, npt),
        grid=(nt,),
        in_specs=[
            smem_tile_spec,
            pl.BlockSpec((1, 1, TOP_K * TL), lambda i: (jnp.minimum(i + 1, nt - 1), 0, 0), memory_space=pltpu.SMEM),
            pl.BlockSpec((TL, D), lambda i: (i, 0)),
            pl.BlockSpec((TL, TOP_K), lambda i: (i, 0)),
            pl.BlockSpec((1, D), lambda i: (0, 0)),
            any_spec,
        ],
        out_specs=[
            pl.BlockSpec((TL, D), lambda i: (jnp.minimum(i, npt - 1), 0)),
            pl.BlockSpec((TL, D), lambda i: (jnp.maximum(i - npt, 0), 0)),
        ],
        out_shape=[jax.ShapeDtypeStruct((tp, D), F32), jax.ShapeDtypeStruct((ts, D), F32)],
        scratch_shapes=[pltpu.VMEM((TOP_K * TL * SUBLANES, LANES), F32), pltpu.VMEM((TOP_K * TL * SUBLANES, LANES), F32),
                        pltpu.SemaphoreType.DMA((2,))],
        compiler_params=pltpu.CompilerParams(dimension_semantics=("arbitrary",), vmem_limit_bytes=VMEM_LIMIT),
        name="combine",
    )(dest_sm, dest_sm, h, gw_t.T, row(g_final), out_buf)

    return (y_p.reshape(bp, lp, D), y_s.reshape(bs, ls, D), na_p[None], nb_p[None], na_s[None], nb_s[None])
```

```python
import functools

import jax
import jax.numpy as jnp
from jax import lax
from jax.experimental import pallas as pl
from jax.experimental.pallas import tpu as pltpu

F32 = jnp.float32
BF16 = jnp.bfloat16
I32 = jnp.int32

D = 1024
LANES = 128
SUBLANES = 8
NCHUNK = D // LANES
N_EXPERTS = 32
TOP_K = 4
TAPS_A = 3
TAPS_B = 31
EPS = 1e-5
SWIGLU_LIMIT = 7.0
SWIGLU_ALPHA = 1.702

TL = 256
SEG = 32
NSEG = TL // SEG
HIST = 32
HIST_A = 8
CONV_BLOCK = 4
ROWS = 256
ISSUE_UNROLL = 8
ROUTE_CHUNK = 512
VMEM_LIMIT = 56 * 1024 * 1024


def _rows_to_slab(dst_ref, slot0, val):
    n = val.shape[0]
    for c in range(NCHUNK):
        dst_ref[pl.ds(SUBLANES * slot0 + c, n, stride=SUBLANES), :] = val[:, c * LANES:(c + 1) * LANES]


def _slab_to_rows(src_ref, slot0, n):
    return jnp.concatenate(
        [src_ref[pl.ds(SUBLANES * slot0 + c, n, stride=SUBLANES), :] for c in range(NCHUNK)], axis=1)


def _rmsnorm(x, g):
    return x * lax.rsqrt(jnp.mean(x * x, axis=-1, keepdims=True) + EPS) * g


def _conv_b_slab(vbuf, hist_rows, w_ref, dst_ref):
    first_tap = HIST - (TAPS_B - 1)
    for s in range(NSEG):
        cur_row0 = (HIST + s * SEG) * SUBLANES
        for t0 in range(0, SEG, CONV_BLOCK):
            ins = []
            for k in range(CONV_BLOCK + TAPS_B - 1):
                u = t0 + first_tap + k
                if u < HIST:
                    row = pl.multiple_of(hist_rows[s] + u * SUBLANES, SUBLANES)
                    ins.append(vbuf[pl.ds(row, SUBLANES), :])
                else:
                    row = cur_row0 + (u - HIST) * SUBLANES
                    ins.append(vbuf[row:row + SUBLANES, :])
            accs = [None] * CONV_BLOCK
            for j in range(TAPS_B):
                wj = w_ref[j]
                for o in range(CONV_BLOCK):
                    term = wj * ins[o + j]
                    accs[o] = term if accs[o] is None else accs[o] + term
            for o in range(CONV_BLOCK):
                row = (s * SEG + t0 + o) * SUBLANES
                dst_ref[row:row + SUBLANES, :] = accs[o]


def _mixer_kernel(n_prompt_tiles, tiles_per_seq, n_prompt_seq, n_sample_seq,
                  xp_ref, xs_ref, sta_ref, stb_ref, gmix_ref, win_ref, bin_ref, wca_ref, woa_ref,
                  wcb_ref, bcb_ref, lng_ref, lnb_ref, wob_ref, bob_ref, wo_ref, gffn_ref, wrt_ref, brt_ref,
                  h_ref, hn_ref, idx_ref, gw_ref, nap_ref, nbp_ref, nas_ref, nbs_ref,
                  cubuf, carry_a, vbuf, ybslab, sa_scr, sb_scr):
    i = pl.program_id(0)
    n_tiles = pl.num_programs(0)
    is_prompt = i < n_prompt_tiles
    seq_start = is_prompt & (lax.rem(i, tiles_per_seq) == 0)
    carry_rows = HIST * SUBLANES

    @pl.when(i == 0)
    def _():
        vbuf[0:carry_rows, :] = jnp.zeros((carry_rows, LANES), F32)
        carry_a[...] = jnp.zeros(carry_a.shape, F32)

    x = jnp.where(is_prompt, xp_ref[...], xs_ref[...])
    n_bf = _rmsnorm(x, gmix_ref[...]).astype(BF16)

    def proj(g):
        cols = slice(g * D, (g + 1) * D)
        return jnp.dot(n_bf, win_ref[:, cols], preferred_element_type=F32) + bin_ref[:, cols]

    cu = proj(1) * proj(2)
    v = proj(3) * jax.nn.sigmoid(proj(4))

    seg_rows = HIST_A + SEG
    prev_a = jnp.where(seq_start, 0.0, carry_a[HIST_A - (TAPS_A - 1):HIST_A, :])
    for s in range(NSEG):
        r0 = s * SEG
        prev = prev_a if s == 0 else cu[r0 - (TAPS_A - 1):r0, :]
        cubuf[s * seg_rows + HIST_A - (TAPS_A - 1):s * seg_rows + HIST_A, :] = jnp.where(is_prompt, prev, sta_ref[s])
        cubuf[s * seg_rows + HIST_A:(s + 1) * seg_rows, :] = cu[r0:r0 + SEG, :]
    carry_a[HIST_A - (TAPS_A - 1):HIST_A, :] = cu[TL - (TAPS_A - 1):TL, :]
    ya_parts = []
    for s in range(NSEG):
        acc = None
        for j in range(TAPS_A):
            lo = s * seg_rows + HIST_A - (TAPS_A - 1) + j
            term = wca_ref[j:j + 1, :] * cubuf[lo:lo + SEG, :]
            acc = term if acc is None else acc + term
        ya_parts.append(acc)
    ya = jnp.concatenate(ya_parts, axis=0)

    vbuf[0:carry_rows, :] = jnp.where(seq_start, 0.0, vbuf[0:carry_rows, :])
    _rows_to_slab(vbuf, HIST, v)
    state_slot0 = HIST + TL
    for s in range(NSEG):
        for c in range(NCHUNK):
            vbuf[pl.ds(SUBLANES * (state_slot0 + s * HIST + HIST - (TAPS_B - 1)) + c, TAPS_B - 1, stride=SUBLANES), :] = (
                stb_ref[s, :, c * LANES:(c + 1) * LANES])
    hist_rows = [jnp.where(is_prompt, s * SEG * SUBLANES, (state_slot0 + s * HIST) * SUBLANES) for s in range(NSEG)]
    _conv_b_slab(vbuf, hist_rows, wcb_ref, ybslab)
    vbuf[0:carry_rows, :] = vbuf[TL * SUBLANES:(TL + HIST) * SUBLANES, :]

    dummy = n_prompt_seq + n_sample_seq
    for s in range(NSEG):
        prompt_slot = i // tiles_per_seq if s == NSEG - 1 else dummy
        slot = jnp.where(is_prompt, prompt_slot, n_prompt_seq + (i - n_prompt_tiles) * NSEG + s)
        r1 = (s + 1) * SEG
        sa_scr[slot] = cu[r1 - (TAPS_A - 1):r1, :]
        sb_scr[slot] = v[r1 - (TAPS_B - 1):r1, :]

    @pl.when(i == n_tiles - 1)
    def _():
        nap_ref[...] = sa_scr[0:n_prompt_seq]
        nbp_ref[...] = sb_scr[0:n_prompt_seq]
        nas_ref[...] = sa_scr[n_prompt_seq:n_prompt_seq + n_sample_seq]
        nbs_ref[...] = sb_scr[n_prompt_seq:n_prompt_seq + n_sample_seq]

    yb = _slab_to_rows(ybslab, 0, TL) + bcb_ref[...]

    out_a = jnp.dot((proj(0) * ya).astype(BF16), woa_ref[...], preferred_element_type=F32)
    mu = jnp.mean(yb, axis=-1, keepdims=True)
    yc = yb - mu
    var = jnp.mean(yc * yc, axis=-1, keepdims=True)
    ln = yc * lax.rsqrt(var + EPS) * lng_ref[...] + lnb_ref[...]
    act = ln * jax.nn.sigmoid(ln)
    out_b = jnp.dot(act.astype(BF16), wob_ref[...], preferred_element_type=F32) + bob_ref[...]
    mix = jax.nn.sigmoid(proj(5)) * out_a + jax.nn.sigmoid(proj(6)) * out_b
    h = x + jnp.dot(mix.astype(BF16), wo_ref[...], preferred_element_type=F32)
    h_ref[...] = h
    hn = _rmsnorm(h, gffn_ref[...])
    _rows_to_slab(hn_ref, 0, hn)

    logits = lax.dot_general(wrt_ref[...], hn.astype(BF16), (((1,), (1,)), ((), ())),
                             preferred_element_type=F32) + brt_ref[...]
    iota_e = lax.broadcasted_iota(I32, (N_EXPERTS, TL), 0).astype(F32)
    vals, idxs = [], []
    for _ in range(TOP_K):
        m = jnp.max(logits, axis=0, keepdims=True)
        sel = jnp.min(jnp.where(logits == m, iota_e, float(N_EXPERTS)), axis=0, keepdims=True)
        vals.append(m)
        idxs.append(sel)
        logits = jnp.where(iota_e == sel, -jnp.inf, logits)
    ex = [jnp.exp(val - vals[0]) for val in vals]
    denom = ex[0] + ex[1] + ex[2] + ex[3]
    for k in range(TOP_K):
        idx_ref[k:k + 1, :] = idxs[k].astype(I32)
        gw_ref[k:k + 1, :] = ex[k] / denom


def _route_kernel(n_tokens, n_blocks_pad, idx_ref, dest_ref, be_ref, nu_ref, nv_ref, pad_start_ref, pad_len_ref, tri_scr):
    n_chunks = n_tokens // ROUTE_CHUNK
    iota_e = lax.broadcasted_iota(I32, (N_EXPERTS, ROUTE_CHUNK), 0)

    def masks(c):
        sl = pl.ds(pl.multiple_of(c * ROUTE_CHUNK, ROUTE_CHUNK), ROUTE_CHUNK)
        return sl, [idx_ref[k:k + 1, sl] == iota_e for k in range(TOP_K)]

    def onehot_sum(ms):
        tot = ms[0].astype(F32)
        for k in range(1, TOP_K):
            tot = tot + ms[k].astype(F32)
        return tot

    def count_body(c, acc):
        _, ms = masks(c)
        return acc + onehot_sum(ms)

    acc = lax.fori_loop(0, n_chunks, count_body, jnp.zeros((N_EXPERTS, ROUTE_CHUNK), F32))
    counts = jnp.sum(acc, axis=1, keepdims=True)
    nblk = jnp.floor((counts + (ROWS - 1)) * (1.0 / ROWS))
    r = lax.broadcasted_iota(I32, (N_EXPERTS, N_EXPERTS), 0)
    cidx = lax.broadcasted_iota(I32, (N_EXPERTS, N_EXPERTS), 1)
    lower = (cidx <= r).astype(F32)
    pend = jnp.dot(lower, jnp.broadcast_to(nblk, (N_EXPERTS, LANES)),
                   precision=lax.Precision.HIGHEST, preferred_element_type=F32)
    pend1 = pend[:, 0:1]
    pstart_rows = (pend1 - nblk) * ROWS
    pad_start_ref[...] = jnp.broadcast_to(pstart_rows + counts, (N_EXPERTS, LANES)).astype(I32)
    pad_len_ref[...] = jnp.broadcast_to(nblk * ROWS - counts, (N_EXPERTS, LANES)).astype(I32)

    bvec = lax.broadcasted_iota(I32, (N_EXPERTS, n_blocks_pad), 1).astype(F32)
    be = jnp.sum((pend1 <= bvec).astype(F32), axis=0, keepdims=True)
    be = jnp.minimum(be, N_EXPERTS - 1)
    be_ref[...] = be.astype(I32)
    of_block = lax.broadcasted_iota(I32, (N_EXPERTS, n_blocks_pad), 0).astype(F32) == be
    count_b = jnp.sum(jnp.where(of_block, counts, 0.0), axis=0, keepdims=True)
    first_b = jnp.sum(jnp.where(of_block, pend1 - nblk, 0.0), axis=0, keepdims=True)
    nv_ref[...] = jnp.clip(count_b - ROWS * (bvec[0:1, :] - first_b), 0.0, float(ROWS)).astype(I32)
    last = lax.broadcasted_iota(I32, (N_EXPERTS, LANES), 0) == N_EXPERTS - 1
    nu_ref[...] = jnp.sum(jnp.where(last, pend, 0.0), axis=0, keepdims=True).astype(I32)

    tr = lax.broadcasted_iota(I32, (ROUTE_CHUNK, ROUTE_CHUNK), 0)
    tc = lax.broadcasted_iota(I32, (ROUTE_CHUNK, ROUTE_CHUNK), 1)
    tri_scr[...] = (tr < tc).astype(BF16)

    def dest_body(c, carry):
        sl, ms = masks(c)
        tot = onehot_sum(ms)
        before = jnp.dot(tot.astype(BF16), tri_scr[...], preferred_element_type=F32)
        pos = pstart_rows + carry + before
        for k in range(TOP_K):
            dest_k = jnp.sum(jnp.where(ms[k], pos, 0.0), axis=0, keepdims=True).astype(I32)
            for part in range(ROUTE_CHUNK // TL):
                dest_ref[c * (ROUTE_CHUNK // TL) + part, :, k * TL:(k + 1) * TL] = dest_k[:, part * TL:(part + 1) * TL]
        return carry + jnp.sum(tot, axis=1, keepdims=True)

    lax.fori_loop(0, n_chunks, dest_body, jnp.zeros((N_EXPERTS, 1), F32))


def _dispatch_kernel(n_blocks, pad_start_ref, pad_len_ref, nu_ref, dest_ref, hn_ref, xb_hbm, ztile, sem, pad_sem):
    def row_copy(tt, k):
        src = pl.multiple_of(tt * SUBLANES, SUBLANES)
        dst = pl.multiple_of(dest_ref[0, 0, k * TL + tt] * SUBLANES, SUBLANES)
        return pltpu.make_async_copy(hn_ref.at[pl.ds(src, SUBLANES), :], xb_hbm.at[pl.ds(dst, SUBLANES), :], sem)

    def body(q, carry):
        for u in range(ISSUE_UNROLL):
            for k in range(TOP_K):
                row_copy(q * ISSUE_UNROLL + u, k).start(priority=k % 2)
        return carry

    lax.fori_loop(0, TL // ISSUE_UNROLL, body, 0)

    @pl.when(pl.program_id(0) == 0)
    def _():
        ztile[...] = jnp.zeros(ztile.shape, F32)

        def zero_copy(row, n_slabs):
            dst = pl.multiple_of(row * SUBLANES, SUBLANES)
            return pltpu.make_async_copy(ztile.at[pl.ds(0, n_slabs * SUBLANES), :],
                                         xb_hbm.at[pl.ds(dst, n_slabs * SUBLANES), :], pad_sem)

        def pad_pass(act):
            def per_expert(e, carry):
                row, length = pad_start_ref[e], pad_len_ref[e]
                for bit in reversed(range(ROWS.bit_length() - 1)):
                    take = (length & (1 << bit)) != 0

                    @pl.when(take)
                    def _():
                        act(zero_copy(row, 1 << bit))

                    row = row + jnp.where(take, 1 << bit, 0)
                return carry

            lax.fori_loop(0, N_EXPERTS, per_expert, 0)

        def tail_pass(act):
            def per_block(blk, carry):
                act(zero_copy(blk * ROWS, ROWS))
                return carry

            lax.fori_loop(nu_ref[0], n_blocks, per_block, 0)

        for act in (lambda c: c.start(), lambda c: c.wait()):
            pad_pass(act)
            tail_pass(act)

    n = TL * SUBLANES
    for k in range(TOP_K):
        pltpu.make_async_copy(hn_ref, xb_hbm.at[pl.ds(0, n), :], sem).wait()


def _expert_kernel(be_ref, nu_ref, nv_ref, xs_ref, wgu_hbm, bgu_ref, wd_hbm, bd_ref, o_ref,
                   wgu_f32, wd_f32, wgu_bf, wd_bf, slot_ref, sems):
    b = pl.program_id(0)
    n_used = nu_ref[0]
    e = be_ref[b]
    prev = be_ref[jnp.maximum(b - 1, 0)]
    active = b < n_used

    def weight_copies(expert, slot):
        return (pltpu.make_async_copy(wgu_hbm.at[expert], wgu_f32.at[slot], sems.at[0, slot]),
                pltpu.make_async_copy(wd_hbm.at[expert], wd_f32.at[slot], sems.at[1, slot]))

    @pl.when(active & (b == 0))
    def _():
        slot_ref[0] = 0
        for c in weight_copies(e, 0):
            c.start()

    @pl.when(active & ((b == 0) | (e != prev)))
    def _():
        slot = slot_ref[0]
        for c in weight_copies(e, slot):
            c.wait()
        last = n_used - 1
        nxt = lax.while_loop(lambda j: (j <= last) & (be_ref[jnp.minimum(j, last)] == e), lambda j: j + 1, b + 1)

        @pl.when(nxt <= last)
        def _():
            for c in weight_copies(be_ref[jnp.minimum(nxt, last)], 1 - slot):
                c.start(priority=1)

        wgu_bf[...] = wgu_f32[slot].astype(BF16)
        wd_bf[...] = wd_f32[slot].astype(BF16)
        slot_ref[0] = 1 - slot

    def mlp(n_rows):
        xb = _slab_to_rows(xs_ref, 0, n_rows).astype(BF16)
        gu = jnp.dot(xb, wgu_bf[...], preferred_element_type=F32) + bgu_ref[0]
        gate = jnp.minimum(gu[:, :D], SWIGLU_LIMIT)
        up = jnp.clip(gu[:, D:], -SWIGLU_LIMIT, SWIGLU_LIMIT)
        hidden = (up + 1.0) * gate * jax.nn.sigmoid(SWIGLU_ALPHA * gate)
        out = jnp.dot(hidden.astype(BF16), wd_bf[...], preferred_element_type=F32) + bd_ref[0]
        _rows_to_slab(o_ref, 0, out)

    half = ROWS // 2
    few = nv_ref[b] <= half

    @pl.when(active & jnp.logical_not(few))
    def _():
        mlp(ROWS)

    @pl.when(active & few)
    def _():
        mlp(half)
        o_ref[half * SUBLANES:ROWS * SUBLANES, :] = jnp.zeros((half * SUBLANES, LANES), F32)

    @pl.when(jnp.logical_not(active))
    def _():
        o_ref[...] = jnp.zeros(o_ref.shape, F32)


def _combine_kernel(n_tiles, n_prompt_tiles, dest_ref, dest_next_ref, h_ref, gw_ref, gfin_ref, ob_hbm,
                    yp_ref, ys_ref, gath0, gath1, sems):
    i = pl.program_id(0)
    n = TOP_K * TL * SUBLANES

    def issue(table_ref, gath, sem):
        def body(q, carry):
            for u in range(ISSUE_UNROLL):
                tt = q * ISSUE_UNROLL + u
                for k in range(TOP_K):
                    src = pl.multiple_of(table_ref[0, 0, k * TL + tt] * SUBLANES, SUBLANES)
                    dst = pl.multiple_of((k * TL + tt) * SUBLANES, SUBLANES)
                    pltpu.make_async_copy(ob_hbm.at[pl.ds(src, SUBLANES), :], gath.at[pl.ds(dst, SUBLANES), :],
                                          sem).start(priority=k % 2)
            return carry

        lax.fori_loop(0, TL // ISSUE_UNROLL, body, 0)

    def step(cur, cur_sem, nxt, nxt_sem):
        @pl.when(i + 1 < n_tiles)
        def _():
            issue(dest_next_ref, nxt, nxt_sem)

        pltpu.make_async_copy(ob_hbm.at[pl.ds(0, n), :], cur, cur_sem).wait()
        acc = h_ref[...]
        gw = gw_ref[...]
        for k in range(TOP_K):
            acc = acc + gw[:, k:k + 1] * _slab_to_rows(cur, k * TL, TL)
        y = _rmsnorm(acc, gfin_ref[...])
        is_prompt = i < n_prompt_tiles

        @pl.when(is_prompt)
        def _():
            yp_ref[...] = y

        @pl.when(jnp.logical_not(is_prompt))
        def _():
            ys_ref[...] = y

    @pl.when(i == 0)
    def _():
        issue(dest_ref, gath0, sems.at[0])

    parity = lax.rem(i, 2)

    @pl.when(parity == 0)
    def _():
        step(gath0, sems.at[0], gath1, sems.at[1])

    @pl.when(parity == 1)
    def _():
        step(gath1, sems.at[1], gath0, sems.at[0])


def _const_spec(shape):
    nd = len(shape)
    return pl.BlockSpec(shape, lambda *_: (0,) * nd, pipeline_mode=pl.Buffered(1))


def kernel(x_prompt, x_sample, state_conv_a, state_conv_b, g_mix, w_in, b_in, w_conv_a, w_out_a, w_conv_b, b_conv_b, ln_g, ln_b, w_out_b, b_out_b, w_o, g_ffn, w_router, b_router, w_gate_up, b_gate_up, w_down, b_down, g_final):
    depth = g_mix.shape[0]
    assert depth == 1
    bp, lp, d = x_prompt.shape
    bs, ls, _ = x_sample.shape
    assert d == D and w_in.shape[2] == 7 * D
    tp, ts = bp * lp, bs * ls
    t = tp + ts
    assert lp % TL == 0 and ls == SEG and ts % TL == 0 and t % ROUTE_CHUNK == 0 and ROUTE_CHUNK % TL == 0
    assert TAPS_B - 1 <= HIST <= SEG and TAPS_A - 1 <= HIST_A
    npt, nst = tp // TL, ts // TL
    nt = npt + nst
    tiles_per_seq = lp // TL
    n_blocks = (t * TOP_K) // ROWS + N_EXPERTS
    n_blocks_pad = -(-n_blocks // LANES) * LANES
    p_rows = n_blocks * ROWS

    row = lambda a: a.reshape(1, -1)
    f32_spec = lambda: _const_spec((1, D))

    mixer = pl.pallas_call(
        functools.partial(_mixer_kernel, npt, tiles_per_seq, bp, bs),
        grid=(nt,),
        in_specs=[
            pl.BlockSpec((TL, D), lambda i: (jnp.minimum(i, npt - 1), 0)),
            pl.BlockSpec((TL, D), lambda i: (jnp.maximum(i - npt, 0), 0)),
            pl.BlockSpec((NSEG, TAPS_A - 1, D), lambda i: (jnp.maximum(i - npt, 0), 0, 0)),
            pl.BlockSpec((NSEG, TAPS_B - 1, D), lambda i: (jnp.maximum(i - npt, 0), 0, 0)),
            f32_spec(),
            _const_spec((D, 7 * D)),
            _const_spec((1, 7 * D)),
            _const_spec((TAPS_A, D)),
            _const_spec((D, D)),
            _const_spec((TAPS_B, SUBLANES, LANES)),
            f32_spec(), f32_spec(), f32_spec(),
            _const_spec((D, D)),
            f32_spec(),
            _const_spec((D, D)),
            f32_spec(),
            _const_spec((N_EXPERTS, D)),
            _const_spec((N_EXPERTS, 1)),
        ],
        out_specs=[
            pl.BlockSpec((TL, D), lambda i: (i, 0)),
            pl.BlockSpec((TL * SUBLANES, LANES), lambda i: (i, 0)),
            pl.BlockSpec((TOP_K, TL), lambda i: (0, i)),
            pl.BlockSpec((TOP_K, TL), lambda i: (0, i)),
            pl.BlockSpec((bp, TAPS_A - 1, D), lambda i: (0, 0, 0)),
            pl.BlockSpec((bp, TAPS_B - 1, D), lambda i: (0, 0, 0)),
            pl.BlockSpec((bs, TAPS_A - 1, D), lambda i: (0, 0, 0)),
            pl.BlockSpec((bs, TAPS_B - 1, D), lambda i: (0, 0, 0)),
        ],
        out_shape=[
            jax.ShapeDtypeStruct((t, D), F32),
            jax.ShapeDtypeStruct((t * SUBLANES, LANES), F32),
            jax.ShapeDtypeStruct((TOP_K, t), I32),
            jax.ShapeDtypeStruct((TOP_K, t), F32),
            jax.ShapeDtypeStruct((bp, TAPS_A - 1, D), F32),
            jax.ShapeDtypeStruct((bp, TAPS_B - 1, D), F32),
            jax.ShapeDtypeStruct((bs, TAPS_A - 1, D), F32),
            jax.ShapeDtypeStruct((bs, TAPS_B - 1, D), F32),
        ],
        scratch_shapes=[
            pltpu.VMEM((NSEG * (HIST_A + SEG), D), F32),
            pltpu.VMEM((HIST_A, D), F32),
            pltpu.VMEM(((HIST + TL + NSEG * HIST) * SUBLANES, LANES), F32),
            pltpu.VMEM((TL * SUBLANES, LANES), F32),
            pltpu.VMEM((bp + bs + 1, TAPS_A - 1, D), F32),
            pltpu.VMEM((bp + bs + 1, TAPS_B - 1, D), F32),
        ],
        compiler_params=pltpu.CompilerParams(dimension_semantics=("arbitrary",), vmem_limit_bytes=VMEM_LIMIT),
        name="mixer",
    )
    h, hn_slab, idx_t, gw_t, na_p, nb_p, na_s, nb_s = mixer(
        x_prompt.reshape(tp, D), x_sample.reshape(ts, D), state_conv_a[0], state_conv_b[0],
        row(g_mix), w_in[0].astype(BF16), row(b_in), w_conv_a[0],
        w_out_a[0].astype(BF16), w_conv_b[0].reshape(TAPS_B, SUBLANES, LANES), row(b_conv_b), row(ln_g), row(ln_b),
        w_out_b[0].astype(BF16), row(b_out_b), w_o[0].astype(BF16), row(g_ffn),
        w_router[0].T.astype(BF16), b_router[0].reshape(N_EXPERTS, 1))

    dest_sm, block_e, n_used, n_valid, pad_start, pad_len = pl.pallas_call(
        functools.partial(_route_kernel, t, n_blocks_pad),
        out_shape=[
            jax.ShapeDtypeStruct((nt, 1, TOP_K * TL), I32),
            jax.ShapeDtypeStruct((1, n_blocks_pad), I32),
            jax.ShapeDtypeStruct((1, LANES), I32),
            jax.ShapeDtypeStruct((1, n_blocks_pad), I32),
            jax.ShapeDtypeStruct((N_EXPERTS, LANES), I32),
            jax.ShapeDtypeStruct((N_EXPERTS, LANES), I32),
        ],
        scratch_shapes=[pltpu.VMEM((ROUTE_CHUNK, ROUTE_CHUNK), BF16)],
        compiler_params=pltpu.CompilerParams(vmem_limit_bytes=VMEM_LIMIT),
        name="route",
    )(idx_t)
    block_e = block_e[0, :n_blocks]
    n_valid = n_valid[0, :n_blocks]
    n_used = n_used[0, :1]
    pad_start, pad_len = pad_start[:, 0], pad_len[:, 0]

    smem_tile_spec = pl.BlockSpec((1, 1, TOP_K * TL), lambda i: (i, 0, 0), memory_space=pltpu.SMEM)
    any_spec = pl.BlockSpec(memory_space=pl.ANY)
    x_buf = pl.pallas_call(
        functools.partial(_dispatch_kernel, n_blocks),
        grid_spec=pltpu.PrefetchScalarGridSpec(
            num_scalar_prefetch=3,
            grid=(nt,),
            in_specs=[
                pl.BlockSpec((1, 1, TOP_K * TL), lambda i, *_: (i, 0, 0), memory_space=pltpu.SMEM),
                pl.BlockSpec((TL * SUBLANES, LANES), lambda i, *_: (i, 0)),
            ],
            out_specs=any_spec,
            scratch_shapes=[pltpu.VMEM((ROWS * SUBLANES, LANES), F32), pltpu.SemaphoreType.DMA,
                            pltpu.SemaphoreType.DMA],
        ),
        out_shape=jax.ShapeDtypeStruct((p_rows * SUBLANES, LANES), F32),
        compiler_params=pltpu.CompilerParams(dimension_semantics=("arbitrary",)),
        name="dispatch",
    )(pad_start, pad_len, n_used, dest_sm, hn_slab)

    def blk(b, be, nu):
        return jnp.maximum(jnp.minimum(b, nu[0] - 1), 0)

    out_buf = pl.pallas_call(
        _expert_kernel,
        grid_spec=pltpu.PrefetchScalarGridSpec(
            num_scalar_prefetch=3,
            grid=(n_blocks,),
            in_specs=[
                pl.BlockSpec((ROWS * SUBLANES, LANES), lambda b, be, nu, nv: (blk(b, be, nu), 0)),
                any_spec,
                pl.BlockSpec((1, 1, 2 * D), lambda b, be, nu, nv: (be[blk(b, be, nu)], 0, 0)),
                any_spec,
                pl.BlockSpec((1, 1, D), lambda b, be, nu, nv: (be[blk(b, be, nu)], 0, 0)),
            ],
            out_specs=pl.BlockSpec((ROWS * SUBLANES, LANES), lambda b, be, nu, nv: (b, 0)),
            scratch_shapes=[
                pltpu.VMEM((2, D, 2 * D), F32), pltpu.VMEM((2, D, D), F32),
                pltpu.VMEM((D, 2 * D), BF16), pltpu.VMEM((D, D), BF16),
                pltpu.SMEM((1,), I32), pltpu.SemaphoreType.DMA((2, 2)),
            ],
        ),
        out_shape=jax.ShapeDtypeStruct((p_rows * SUBLANES, LANES), F32),
        compiler_params=pltpu.CompilerParams(dimension_semantics=("arbitrary",), vmem_limit_bytes=VMEM_LIMIT),
        name="experts",
    )(block_e, n_used, n_valid, x_buf, w_gate_up[0], b_gate_up[0].reshape(N_EXPERTS, 1, 2 * D), w_down[0],
      b_down[0].reshape(N_EXPERTS, 1, D))

    y_p, y_s = pl.pallas_call(
        functools.partial(_combine_kernel, nt, npt),
        grid=(nt,),
        in_specs=[
            smem_tile_spec,
            pl.BlockSpec((1, 1, TOP_K * TL), lambda i: (jnp.minimum(i + 1, nt - 1), 0, 0), memory_space=pltpu.SMEM),
            pl.BlockSpec((TL, D), lambda i: (i, 0)),
            pl.BlockSpec((TL, TOP_K), lambda i: (i, 0)),
            pl.BlockSpec((1, D), lambda i: (0, 0)),
            any_spec,
        ],
        out_specs=[
            pl.BlockSpec((TL, D), lambda i: (jnp.minimum(i, npt - 1), 0)),
            pl.BlockSpec((TL, D), lambda i: (jnp.maximum(i - npt, 0), 0)),
        ],
        out_shape=[jax.ShapeDtypeStruct((tp, D), F32), jax.ShapeDtypeStruct((ts, D), F32)],
        scratch_shapes=[pltpu.VMEM((TOP_K * TL * SUBLANES, LANES), F32), pltpu.VMEM((TOP_K * TL * SUBLANES, LANES), F32),
                        pltpu.SemaphoreType.DMA((2,))],
        compiler_params=pltpu.CompilerParams(dimension_semantics=("arbitrary",), vmem_limit_bytes=VMEM_LIMIT),
        name="combine",
    )(dest_sm, dest_sm, h, gw_t.T, row(g_final), out_buf)

    return (y_p.reshape(bp, lp, D), y_s.reshape(bs, ls, D), na_p[None], nb_p[None], na_s[None], nb_s[None])
```

```python
import functools

import jax
import jax.numpy as jnp
from jax import lax
from jax.experimental import pallas as pl
from jax.experimental.pallas import tpu as pltpu

F32 = jnp.float32
BF16 = jnp.bfloat16
I32 = jnp.int32

D = 1024
LANES = 128
SUBLANES = 8
NCHUNK = D // LANES
N_EXPERTS = 32
TOP_K = 4
TAPS_A = 3
TAPS_B = 31
EPS = 1e-5
SWIGLU_LIMIT = 7.0
SWIGLU_ALPHA = 1.702

TL = 256
SEG = 32
NSEG = TL // SEG
HIST = 32
HIST_A = 8
CONV_BLOCK = 4
ROWS = 512
ROW_GRANULE = 128
ISSUE_UNROLL = 8
ROUTE_CHUNK = 512
VMEM_LIMIT = 56 * 1024 * 1024


def _rows_to_slab(dst_ref, slot0, val):
    n = val.shape[0]
    for c in range(NCHUNK):
        dst_ref[pl.ds(SUBLANES * slot0 + c, n, stride=SUBLANES), :] = val[:, c * LANES:(c + 1) * LANES]


def _slab_to_rows(src_ref, slot0, n):
    return jnp.concatenate(
        [src_ref[pl.ds(SUBLANES * slot0 + c, n, stride=SUBLANES), :] for c in range(NCHUNK)], axis=1)


def _rmsnorm(x, g):
    return x * lax.rsqrt(jnp.mean(x * x, axis=-1, keepdims=True) + EPS) * g


def _conv_b_slab(vbuf, hist_rows, w_ref, dst_ref):
    first_tap = HIST - (TAPS_B - 1)
    for s in range(NSEG):
        cur_row0 = (HIST + s * SEG) * SUBLANES
        for t0 in range(0, SEG, CONV_BLOCK):
            ins = []
            for k in range(CONV_BLOCK + TAPS_B - 1):
                u = t0 + first_tap + k
                if u < HIST:
                    row = pl.multiple_of(hist_rows[s] + u * SUBLANES, SUBLANES)
                    ins.append(vbuf[pl.ds(row, SUBLANES), :])
                else:
                    row = cur_row0 + (u - HIST) * SUBLANES
                    ins.append(vbuf[row:row + SUBLANES, :])
            accs = [None] * CONV_BLOCK
            for j in range(TAPS_B):
                wj = w_ref[j]
                for o in range(CONV_BLOCK):
                    term = wj * ins[o + j]
                    accs[o] = term if accs[o] is None else accs[o] + term
            for o in range(CONV_BLOCK):
                row = (s * SEG + t0 + o) * SUBLANES
                dst_ref[row:row + SUBLANES, :] = accs[o]


def _mixer_kernel(n_prompt_tiles, tiles_per_seq, n_prompt_seq, n_sample_seq,
                  xp_ref, xs_ref, sta_ref, stb_ref, gmix_ref, win_ref, bin_ref, wca_ref, woa_ref,
                  wcb_ref, bcb_ref, lng_ref, lnb_ref, wob_ref, bob_ref, wo_ref, gffn_ref, wrt_ref, brt_ref,
                  h_ref, hn_ref, idx_ref, gw_ref, nap_ref, nbp_ref, nas_ref, nbs_ref,
                  cubuf, carry_a, vbuf, ybslab, sa_scr, sb_scr):
    i = pl.program_id(0)
    n_tiles = pl.num_programs(0)
    is_prompt = i < n_prompt_tiles
    seq_start = is_prompt & (lax.rem(i, tiles_per_seq) == 0)
    carry_rows = HIST * SUBLANES

    @pl.when(i == 0)
    def _():
        vbuf[0:carry_rows, :] = jnp.zeros((carry_rows, LANES), F32)
        carry_a[...] = jnp.zeros(carry_a.shape, F32)

    x = jnp.where(is_prompt, xp_ref[...], xs_ref[...])
    n_bf = _rmsnorm(x, gmix_ref[...]).astype(BF16)

    def proj(g):
        cols = slice(g * D, (g + 1) * D)
        return jnp.dot(n_bf, win_ref[:, cols], preferred_element_type=F32) + bin_ref[:, cols]

    cu = proj(1) * proj(2)
    v = proj(3) * jax.nn.sigmoid(proj(4))

    seg_rows = HIST_A + SEG
    prev_a = jnp.where(seq_start, 0.0, carry_a[HIST_A - (TAPS_A - 1):HIST_A, :])
    for s in range(NSEG):
        r0 = s * SEG
        prev = prev_a if s == 0 else cu[r0 - (TAPS_A - 1):r0, :]
        cubuf[s * seg_rows + HIST_A - (TAPS_A - 1):s * seg_rows + HIST_A, :] = jnp.where(is_prompt, prev, sta_ref[s])
        cubuf[s * seg_rows + HIST_A:(s + 1) * seg_rows, :] = cu[r0:r0 + SEG, :]
    carry_a[HIST_A - (TAPS_A - 1):HIST_A, :] = cu[TL - (TAPS_A - 1):TL, :]
    ya_parts = []
    for s in range(NSEG):
        acc = None
        for j in range(TAPS_A):
            lo = s * seg_rows + HIST_A - (TAPS_A - 1) + j
            term = wca_ref[j:j + 1, :] * cubuf[lo:lo + SEG, :]
            acc = term if acc is None else acc + term
        ya_parts.append(acc)
    ya = jnp.concatenate(ya_parts, axis=0)

    vbuf[0:carry_rows, :] = jnp.where(seq_start, 0.0, vbuf[0:carry_rows, :])
    _rows_to_slab(vbuf, HIST, v)
    state_slot0 = HIST + TL
    for s in range(NSEG):
        for c in range(NCHUNK):
            vbuf[pl.ds(SUBLANES * (state_slot0 + s * HIST + HIST - (TAPS_B - 1)) + c, TAPS_B - 1, stride=SUBLANES), :] = (
                stb_ref[s, :, c * LANES:(c + 1) * LANES])
    hist_rows = [jnp.where(is_prompt, s * SEG * SUBLANES, (state_slot0 + s * HIST) * SUBLANES) for s in range(NSEG)]
    _conv_b_slab(vbuf, hist_rows, wcb_ref, ybslab)
    vbuf[0:carry_rows, :] = vbuf[TL * SUBLANES:(TL + HIST) * SUBLANES, :]

    dummy = n_prompt_seq + n_sample_seq
    for s in range(NSEG):
        prompt_slot = i // tiles_per_seq if s == NSEG - 1 else dummy
        slot = jnp.where(is_prompt, prompt_slot, n_prompt_seq + (i - n_prompt_tiles) * NSEG + s)
        r1 = (s + 1) * SEG
        sa_scr[slot] = cu[r1 - (TAPS_A - 1):r1, :]
        sb_scr[slot] = v[r1 - (TAPS_B - 1):r1, :]

    @pl.when(i == n_tiles - 1)
    def _():
        nap_ref[...] = sa_scr[0:n_prompt_seq]
        nbp_ref[...] = sb_scr[0:n_prompt_seq]
        nas_ref[...] = sa_scr[n_prompt_seq:n_prompt_seq + n_sample_seq]
        nbs_ref[...] = sb_scr[n_prompt_seq:n_prompt_seq + n_sample_seq]

    yb = _slab_to_rows(ybslab, 0, TL) + bcb_ref[...]

    out_a = jnp.dot((proj(0) * ya).astype(BF16), woa_ref[...], preferred_element_type=F32)
    mu = jnp.mean(yb, axis=-1, keepdims=True)
    yc = yb - mu
    var = jnp.mean(yc * yc, axis=-1, keepdims=True)
    ln = yc * lax.rsqrt(var + EPS) * lng_ref[...] + lnb_ref[...]
    act = ln * jax.nn.sigmoid(ln)
    out_b = jnp.dot(act.astype(BF16), wob_ref[...], preferred_element_type=F32) + bob_ref[...]
    mix = jax.nn.sigmoid(proj(5)) * out_a + jax.nn.sigmoid(proj(6)) * out_b
    h = x + jnp.dot(mix.astype(BF16), wo_ref[...], preferred_element_type=F32)
    h_ref[...] = h
    hn = _rmsnorm(h, gffn_ref[...])
    _rows_to_slab(hn_ref, 0, hn)

    logits = lax.dot_general(wrt_ref[...], hn.astype(BF16), (((1,), (1,)), ((), ())),
                             preferred_element_type=F32) + brt_ref[...]
    iota_e = lax.broadcasted_iota(I32, (N_EXPERTS, TL), 0).astype(F32)
    vals, idxs = [], []
    for _ in range(TOP_K):
        m = jnp.max(logits, axis=0, keepdims=True)
        sel = jnp.min(jnp.where(logits == m, iota_e, float(N_EXPERTS)), axis=0, keepdims=True)
        vals.append(m)
        idxs.append(sel)
        logits = jnp.where(iota_e == sel, -jnp.inf, logits)
    ex = [jnp.exp(val - vals[0]) for val in vals]
    denom = ex[0] + ex[1] + ex[2] + ex[3]
    for k in range(TOP_K):
        idx_ref[k:k + 1, :] = idxs[k].astype(I32)
        gw_ref[k:k + 1, :] = ex[k] / denom


def _route_kernel(n_tokens, n_blocks_pad, idx_ref, dest_ref, be_ref, nu_ref, nv_ref, pad_start_ref, pad_len_ref, tri_scr):
    n_chunks = n_tokens // ROUTE_CHUNK
    iota_e = lax.broadcasted_iota(I32, (N_EXPERTS, ROUTE_CHUNK), 0)

    def masks(c):
        sl = pl.ds(pl.multiple_of(c * ROUTE_CHUNK, ROUTE_CHUNK), ROUTE_CHUNK)
        return sl, [idx_ref[k:k + 1, sl] == iota_e for k in range(TOP_K)]

    def onehot_sum(ms):
        tot = ms[0].astype(F32)
        for k in range(1, TOP_K):
            tot = tot + ms[k].astype(F32)
        return tot

    def count_body(c, acc):
        _, ms = masks(c)
        return acc + onehot_sum(ms)

    acc = lax.fori_loop(0, n_chunks, count_body, jnp.zeros((N_EXPERTS, ROUTE_CHUNK), F32))
    counts = jnp.sum(acc, axis=1, keepdims=True)
    nblk = jnp.floor((counts + (ROWS - 1)) * (1.0 / ROWS))
    r = lax.broadcasted_iota(I32, (N_EXPERTS, N_EXPERTS), 0)
    cidx = lax.broadcasted_iota(I32, (N_EXPERTS, N_EXPERTS), 1)
    lower = (cidx <= r).astype(F32)
    pend = jnp.dot(lower, jnp.broadcast_to(nblk, (N_EXPERTS, LANES)),
                   precision=lax.Precision.HIGHEST, preferred_element_type=F32)
    pend1 = pend[:, 0:1]
    pstart_rows = (pend1 - nblk) * ROWS
    pad_start_ref[...] = jnp.broadcast_to(pstart_rows + counts, (N_EXPERTS, LANES)).astype(I32)
    pad_len_ref[...] = jnp.broadcast_to(nblk * ROWS - counts, (N_EXPERTS, LANES)).astype(I32)

    bvec = lax.broadcasted_iota(I32, (N_EXPERTS, n_blocks_pad), 1).astype(F32)
    be = jnp.sum((pend1 <= bvec).astype(F32), axis=0, keepdims=True)
    be = jnp.minimum(be, N_EXPERTS - 1)
    be_ref[...] = be.astype(I32)
    of_block = lax.broadcasted_iota(I32, (N_EXPERTS, n_blocks_pad), 0).astype(F32) == be
    count_b = jnp.sum(jnp.where(of_block, counts, 0.0), axis=0, keepdims=True)
    first_b = jnp.sum(jnp.where(of_block, pend1 - nblk, 0.0), axis=0, keepdims=True)
    nv_ref[...] = jnp.clip(count_b - ROWS * (bvec[0:1, :] - first_b), 0.0, float(ROWS)).astype(I32)
    last = lax.broadcasted_iota(I32, (N_EXPERTS, LANES), 0) == N_EXPERTS - 1
    nu_ref[...] = jnp.sum(jnp.where(last, pend, 0.0), axis=0, keepdims=True).astype(I32)

    tr = lax.broadcasted_iota(I32, (ROUTE_CHUNK, ROUTE_CHUNK), 0)
    tc = lax.broadcasted_iota(I32, (ROUTE_CHUNK, ROUTE_CHUNK), 1)
    tri_scr[...] = (tr < tc).astype(BF16)

    def dest_body(c, carry):
        sl, ms = masks(c)
        tot = onehot_sum(ms)
        before = jnp.dot(tot.astype(BF16), tri_scr[...], preferred_element_type=F32)
        pos = pstart_rows + carry + before
        for k in range(TOP_K):
            dest_k = jnp.sum(jnp.where(ms[k], pos, 0.0), axis=0, keepdims=True).astype(I32)
            for part in range(ROUTE_CHUNK // TL):
                dest_ref[c * (ROUTE_CHUNK // TL) + part, :, k * TL:(k + 1) * TL] = dest_k[:, part * TL:(part + 1) * TL]
        return carry + jnp.sum(tot, axis=1, keepdims=True)

    lax.fori_loop(0, n_chunks, dest_body, jnp.zeros((N_EXPERTS, 1), F32))


def _dispatch_kernel(n_blocks, pad_start_ref, pad_len_ref, nu_ref, dest_ref, hn_ref, xb_hbm, ztile, sem, pad_sem):
    def row_copy(tt, k):
        src = pl.multiple_of(tt * SUBLANES, SUBLANES)
        dst = pl.multiple_of(dest_ref[0, 0, k * TL + tt] * SUBLANES, SUBLANES)
        return pltpu.make_async_copy(hn_ref.at[pl.ds(src, SUBLANES), :], xb_hbm.at[pl.ds(dst, SUBLANES), :], sem)

    def body(q, carry):
        for u in range(ISSUE_UNROLL):
            for k in range(TOP_K):
                row_copy(q * ISSUE_UNROLL + u, k).start(priority=k % 2)
        return carry

    lax.fori_loop(0, TL // ISSUE_UNROLL, body, 0)

    @pl.when(pl.program_id(0) == 0)
    def _():
        ztile[...] = jnp.zeros(ztile.shape, F32)

        def zero_copy(row, n_slabs):
            dst = pl.multiple_of(row * SUBLANES, SUBLANES)
            return pltpu.make_async_copy(ztile.at[pl.ds(0, n_slabs * SUBLANES), :],
                                         xb_hbm.at[pl.ds(dst, n_slabs * SUBLANES), :], pad_sem)

        def pad_pass(act):
            def per_expert(e, carry):
                row, length = pad_start_ref[e], pad_len_ref[e]
                for bit in reversed(range(ROWS.bit_length() - 1)):
                    take = (length & (1 << bit)) != 0

                    @pl.when(take)
                    def _():
                        act(zero_copy(row, 1 << bit))

                    row = row + jnp.where(take, 1 << bit, 0)
                return carry

            lax.fori_loop(0, N_EXPERTS, per_expert, 0)

        def tail_pass(act):
            def per_block(blk, carry):
                act(zero_copy(blk * ROWS, ROWS))
                return carry

            lax.fori_loop(nu_ref[0], n_blocks, per_block, 0)

        for act in (lambda c: c.start(), lambda c: c.wait()):
            pad_pass(act)
            tail_pass(act)

    n = TL * SUBLANES
    for k in range(TOP_K):
        pltpu.make_async_copy(hn_ref, xb_hbm.at[pl.ds(0, n), :], sem).wait()


def _expert_kernel(be_ref, nu_ref, nv_ref, xs_ref, wgu_hbm, bgu_ref, wd_hbm, bd_ref, o_ref,
                   wgu_f32, wd_f32, wgu_bf, wd_bf, slot_ref, sems):
    b = pl.program_id(0)
    n_used = nu_ref[0]
    e = be_ref[b]
    prev = be_ref[jnp.maximum(b - 1, 0)]
    active = b < n_used

    def weight_copies(expert, slot):
        return (pltpu.make_async_copy(wgu_hbm.at[expert], wgu_f32.at[slot], sems.at[0, slot]),
                pltpu.make_async_copy(wd_hbm.at[expert], wd_f32.at[slot], sems.at[1, slot]))

    @pl.when(active & (b == 0))
    def _():
        slot_ref[0] = 0
        for c in weight_copies(e, 0):
            c.start()

    @pl.when(active & ((b == 0) | (e != prev)))
    def _():
        slot = slot_ref[0]
        for c in weight_copies(e, slot):
            c.wait()
        last = n_used - 1
        nxt = lax.while_loop(lambda j: (j <= last) & (be_ref[jnp.minimum(j, last)] == e), lambda j: j + 1, b + 1)

        @pl.when(nxt <= last)
        def _():
            for c in weight_copies(be_ref[jnp.minimum(nxt, last)], 1 - slot):
                c.start(priority=1)

        wgu_bf[...] = wgu_f32[slot].astype(BF16)
        wd_bf[...] = wd_f32[slot].astype(BF16)
        slot_ref[0] = 1 - slot

    def mlp(n_rows):
        xb = _slab_to_rows(xs_ref, 0, n_rows).astype(BF16)
        gu = jnp.dot(xb, wgu_bf[...], preferred_element_type=F32) + bgu_ref[0]
        gate = jnp.minimum(gu[:, :D], SWIGLU_LIMIT)
        up = jnp.clip(gu[:, D:], -SWIGLU_LIMIT, SWIGLU_LIMIT)
        hidden = (up + 1.0) * gate * jax.nn.sigmoid(SWIGLU_ALPHA * gate)
        out = jnp.dot(hidden.astype(BF16), wd_bf[...], preferred_element_type=F32) + bd_ref[0]
        _rows_to_slab(o_ref, 0, out)

    n_valid = nv_ref[b]
    for n_rows in range(ROW_GRANULE, ROWS + 1, ROW_GRANULE):
        @pl.when(active & (n_valid > n_rows - ROW_GRANULE) & (n_valid <= n_rows))
        def _(n_rows=n_rows):
            mlp(n_rows)
            if n_rows < ROWS:
                o_ref[n_rows * SUBLANES:ROWS * SUBLANES, :] = jnp.zeros(((ROWS - n_rows) * SUBLANES, LANES), F32)

    @pl.when(jnp.logical_not(active))
    def _():
        o_ref[...] = jnp.zeros(o_ref.shape, F32)


def _combine_kernel(n_tiles, n_prompt_tiles, dest_ref, dest_next_ref, h_ref, gw_ref, gfin_ref, ob_hbm,
                    yp_ref, ys_ref, gath0, gath1, sems):
    i = pl.program_id(0)
    n = TOP_K * TL * SUBLANES

    def issue(table_ref, gath, sem):
        def body(q, carry):
            for u in range(ISSUE_UNROLL):
                tt = q * ISSUE_UNROLL + u
                for k in range(TOP_K):
                    src = pl.multiple_of(table_ref[0, 0, k * TL + tt] * SUBLANES, SUBLANES)
                    dst = pl.multiple_of((k * TL + tt) * SUBLANES, SUBLANES)
                    pltpu.make_async_copy(ob_hbm.at[pl.ds(src, SUBLANES), :], gath.at[pl.ds(dst, SUBLANES), :],
                                          sem).start(priority=k % 2)
            return carry

        lax.fori_loop(0, TL // ISSUE_UNROLL, body, 0)

    def step(cur, cur_sem, nxt, nxt_sem):
        @pl.when(i + 1 < n_tiles)
        def _():
            issue(dest_next_ref, nxt, nxt_sem)

        pltpu.make_async_copy(ob_hbm.at[pl.ds(0, n), :], cur, cur_sem).wait()
        acc = h_ref[...]
        gw = gw_ref[...]
        for k in range(TOP_K):
            acc = acc + gw[:, k:k + 1] * _slab_to_rows(cur, k * TL, TL)
        y = _rmsnorm(acc, gfin_ref[...])
        is_prompt = i < n_prompt_tiles

        @pl.when(is_prompt)
        def _():
            yp_ref[...] = y

        @pl.when(jnp.logical_not(is_prompt))
        def _():
            ys_ref[...] = y

    @pl.when(i == 0)
    def _():
        issue(dest_ref, gath0, sems.at[0])

    parity = lax.rem(i, 2)

    @pl.when(parity == 0)
    def _():
        step(gath0, sems.at[0], gath1, sems.at[1])

    @pl.when(parity == 1)
    def _():
        step(gath1, sems.at[1], gath0, sems.at[0])


def _const_spec(shape):
    nd = len(shape)
    return pl.BlockSpec(shape, lambda *_: (0,) * nd, pipeline_mode=pl.Buffered(1))


def kernel(x_prompt, x_sample, state_conv_a, state_conv_b, g_mix, w_in, b_in, w_conv_a, w_out_a, w_conv_b, b_conv_b, ln_g, ln_b, w_out_b, b_out_b, w_o, g_ffn, w_router, b_router, w_gate_up, b_gate_up, w_down, b_down, g_final):
    depth = g_mix.shape[0]
    assert depth == 1
    bp, lp, d = x_prompt.shape
    bs, ls, _ = x_sample.shape
    assert d == D and w_in.shape[2] == 7 * D
    tp, ts = bp * lp, bs * ls
    t = tp + ts
    assert lp % TL == 0 and ls == SEG and ts % TL == 0 and t % ROUTE_CHUNK == 0 and ROUTE_CHUNK % TL == 0
    assert TAPS_B - 1 <= HIST <= SEG and TAPS_A - 1 <= HIST_A
    npt, nst = tp // TL, ts // TL
    nt = npt + nst
    tiles_per_seq = lp // TL
    n_blocks = (t * TOP_K) // ROWS + N_EXPERTS
    n_blocks_pad = -(-n_blocks // LANES) * LANES
    p_rows = n_blocks * ROWS

    row = lambda a: a.reshape(1, -1)
    f32_spec = lambda: _const_spec((1, D))

    mixer = pl.pallas_call(
        functools.partial(_mixer_kernel, npt, tiles_per_seq, bp, bs),
        grid=(nt,),
        in_specs=[
            pl.BlockSpec((TL, D), lambda i: (jnp.minimum(i, npt - 1), 0)),
            pl.BlockSpec((TL, D), lambda i: (jnp.maximum(i - npt, 0), 0)),
            pl.BlockSpec((NSEG, TAPS_A - 1, D), lambda i: (jnp.maximum(i - npt, 0), 0, 0)),
            pl.BlockSpec((NSEG, TAPS_B - 1, D), lambda i: (jnp.maximum(i - npt, 0), 0, 0)),
            f32_spec(),
            _const_spec((D, 7 * D)),
            _const_spec((1, 7 * D)),
            _const_spec((TAPS_A, D)),
            _const_spec((D, D)),
            _const_spec((TAPS_B, SUBLANES, LANES)),
            f32_spec(), f32_spec(), f32_spec(),
            _const_spec((D, D)),
            f32_spec(),
            _const_spec((D, D)),
            f32_spec(),
            _const_spec((N_EXPERTS, D)),
            _const_spec((N_EXPERTS, 1)),
        ],
        out_specs=[
            pl.BlockSpec((TL, D), lambda i: (i, 0)),
            pl.BlockSpec((TL * SUBLANES, LANES), lambda i: (i, 0)),
            pl.BlockSpec((TOP_K, TL), lambda i: (0, i)),
            pl.BlockSpec((TOP_K, TL), lambda i: (0, i)),
            pl.BlockSpec((bp, TAPS_A - 1, D), lambda i: (0, 0, 0)),
            pl.BlockSpec((bp, TAPS_B - 1, D), lambda i: (0, 0, 0)),
            pl.BlockSpec((bs, TAPS_A - 1, D), lambda i: (0, 0, 0)),
            pl.BlockSpec((bs, TAPS_B - 1, D), lambda i: (0, 0, 0)),
        ],
        out_shape=[
            jax.ShapeDtypeStruct((t, D), F32),
            jax.ShapeDtypeStruct((t * SUBLANES, LANES), F32),
            jax.ShapeDtypeStruct((TOP_K, t), I32),
            jax.ShapeDtypeStruct((TOP_K, t), F32),
            jax.ShapeDtypeStruct((bp, TAPS_A - 1, D), F32),
            jax.ShapeDtypeStruct((bp, TAPS_B - 1, D), F32),
            jax.ShapeDtypeStruct((bs, TAPS_A - 1, D), F32),
            jax.ShapeDtypeStruct((bs, TAPS_B - 1, D), F32),
        ],
        scratch_shapes=[
            pltpu.VMEM((NSEG * (HIST_A + SEG), D), F32),
            pltpu.VMEM((HIST_A, D), F32),
            pltpu.VMEM(((HIST + TL + NSEG * HIST) * SUBLANES, LANES), F32),
            pltpu.VMEM((TL * SUBLANES, LANES), F32),
            pltpu.VMEM((bp + bs + 1, TAPS_A - 1, D), F32),
            pltpu.VMEM((bp + bs + 1, TAPS_B - 1, D), F32),
        ],
        compiler_params=pltpu.CompilerParams(dimension_semantics=("arbitrary",), vmem_limit_bytes=VMEM_LIMIT),
        name="mixer",
    )
    h, hn_slab, idx_t, gw_t, na_p, nb_p, na_s, nb_s = mixer(
        x_prompt.reshape(tp, D), x_sample.reshape(ts, D), state_conv_a[0], state_conv_b[0],
        row(g_mix), w_in[0].astype(BF16), row(b_in), w_conv_a[0],
        w_out_a[0].astype(BF16), w_conv_b[0].reshape(TAPS_B, SUBLANES, LANES), row(b_conv_b), row(ln_g), row(ln_b),
        w_out_b[0].astype(BF16), row(b_out_b), w_o[0].astype(BF16), row(g_ffn),
        w_router[0].T.astype(BF16), b_router[0].reshape(N_EXPERTS, 1))

    dest_sm, block_e, n_used, n_valid, pad_start, pad_len = pl.pallas_call(
        functools.partial(_route_kernel, t, n_blocks_pad),
        out_shape=[
            jax.ShapeDtypeStruct((nt, 1, TOP_K * TL), I32),
            jax.ShapeDtypeStruct((1, n_blocks_pad), I32),
            jax.ShapeDtypeStruct((1, LANES), I32),
            jax.ShapeDtypeStruct((1, n_blocks_pad), I32),
            jax.ShapeDtypeStruct((N_EXPERTS, LANES), I32),
            jax.ShapeDtypeStruct((N_EXPERTS, LANES), I32),
        ],
        scratch_shapes=[pltpu.VMEM((ROUTE_CHUNK, ROUTE_CHUNK), BF16)],
        compiler_params=pltpu.CompilerParams(vmem_limit_bytes=VMEM_LIMIT),
        name="route",
    )(idx_t)
    block_e = block_e[0, :n_blocks]
    n_valid = n_valid[0, :n_blocks]
    n_used = n_used[0, :1]
    pad_start, pad_len = pad_start[:, 0], pad_len[:, 0]

    smem_tile_spec = pl.BlockSpec((1, 1, TOP_K * TL), lambda i: (i, 0, 0), memory_space=pltpu.SMEM)
    any_spec = pl.BlockSpec(memory_space=pl.ANY)
    x_buf = pl.pallas_call(
        functools.partial(_dispatch_kernel, n_blocks),
        grid_spec=pltpu.PrefetchScalarGridSpec(
            num_scalar_prefetch=3,
            grid=(nt,),
            in_specs=[
                pl.BlockSpec((1, 1, TOP_K * TL), lambda i, *_: (i, 0, 0), memory_space=pltpu.SMEM),
                pl.BlockSpec((TL * SUBLANES, LANES), lambda i, *_: (i, 0)),
            ],
            out_specs=any_spec,
            scratch_shapes=[pltpu.VMEM((ROWS * SUBLANES, LANES), F32), pltpu.SemaphoreType.DMA,
                            pltpu.SemaphoreType.DMA],
        ),
        out_shape=jax.ShapeDtypeStruct((p_rows * SUBLANES, LANES), F32),
        compiler_params=pltpu.CompilerParams(dimension_semantics=("arbitrary",)),
        name="dispatch",
    )(pad_start, pad_len, n_used, dest_sm, hn_slab)

    def blk(b, be, nu):
        return jnp.maximum(jnp.minimum(b, nu[0] - 1), 0)

    out_buf = pl.pallas_call(
        _expert_kernel,
        grid_spec=pltpu.PrefetchScalarGridSpec(
            num_scalar_prefetch=3,
            grid=(n_blocks,),
            in_specs=[
                pl.BlockSpec((ROWS * SUBLANES, LANES), lambda b, be, nu, nv: (blk(b, be, nu), 0)),
                any_spec,
                pl.BlockSpec((1, 1, 2 * D), lambda b, be, nu, nv: (be[blk(b, be, nu)], 0, 0)),
                any_spec,
                pl.BlockSpec((1, 1, D), lambda b, be, nu, nv: (be[blk(b, be, nu)], 0, 0)),
            ],
            out_specs=pl.BlockSpec((ROWS * SUBLANES, LANES), lambda b, be, nu, nv: (b, 0)),
            scratch_shapes=[
                pltpu.VMEM((2, D, 2 * D), F32), pltpu.VMEM((2, D, D), F32),
                pltpu.VMEM((D, 2 * D), BF16), pltpu.VMEM((D, D), BF16),
                pltpu.SMEM((1,), I32), pltpu.SemaphoreType.DMA((2, 2)),
            ],
        ),
        out_shape=jax.ShapeDtypeStruct((p_rows * SUBLANES, LANES), F32),
        compiler_params=pltpu.CompilerParams(dimension_semantics=("arbitrary",), vmem_limit_bytes=VMEM_LIMIT),
        name="experts",
    )(block_e, n_used, n_valid, x_buf, w_gate_up[0], b_gate_up[0].reshape(N_EXPERTS, 1, 2 * D), w_down[0],
      b_down[0].reshape(N_EXPERTS, 1, D))

    y_p, y_s = pl.pallas_call(
        functools.partial(_combine_kernel, nt, npt),
        grid=(nt,),
        in_specs=[
            smem_tile_spec,
            pl.BlockSpec((1, 1, TOP_K * TL), lambda i: (jnp.minimum(i + 1, nt - 1), 0, 0), memory_space=pltpu.SMEM),
            pl.BlockSpec((TL, D), lambda i: (i, 0)),
            pl.BlockSpec((TL, TOP_K), lambda i: (i, 0)),
            pl.BlockSpec((1, D), lambda i: (0, 0)),
            any_spec,
        ],
        out_specs=[
            pl.BlockSpec((TL, D), lambda i: (jnp.minimum(i, npt - 1), 0)),
            pl.BlockSpec((TL, D), lambda i: (jnp.maximum(i - npt, 0), 0)),
        ],
        out_shape=[jax.ShapeDtypeStruct((tp, D), F32), jax.ShapeDtypeStruct((ts, D), F32)],
        scratch_shapes=[pltpu.VMEM((TOP_K * TL * SUBLANES, LANES), F32), pltpu.VMEM((TOP_K * TL * SUBLANES, LANES), F32),
                        pltpu.SemaphoreType.DMA((2,))],
        compiler_params=pltpu.CompilerParams(dimension_semantics=("arbitrary",), vmem_limit_bytes=VMEM_LIMIT),
        name="combine",
    )(dest_sm, dest_sm, h, gw_t.T, row(g_final), out_buf)

    return (y_p.reshape(bp, lp, D), y_s.reshape(bs, ls, D), na_p[None], nb_p[None], na_s[None], nb_s[None])
```

```python
import functools

import jax
import jax.numpy as jnp
from jax import lax
from jax.experimental import pallas as pl
from jax.experimental.pallas import tpu as pltpu

F32 = jnp.float32
BF16 = jnp.bfloat16
I32 = jnp.int32

D = 1024
LANES = 128
SUBLANES = 8
NCHUNK = D // LANES
N_EXPERTS = 32
TOP_K = 4
TAPS_A = 3
TAPS_B = 31
EPS = 1e-5
SWIGLU_LIMIT = 7.0
SWIGLU_ALPHA = 1.702

TL = 256
SEG = 32
NSEG = TL // SEG
HIST = 32
HIST_A = 8
CONV_BLOCK = 4
ROWS = 512
ROW_GRANULE = 128
ISSUE_UNROLL = 8
ROUTE_CHUNK = 512
VMEM_LIMIT = 56 * 1024 * 1024


def _rows_to_slab(dst_ref, slot0, val):
    n = val.shape[0]
    for c in range(NCHUNK):
        dst_ref[pl.ds(SUBLANES * slot0 + c, n, stride=SUBLANES), :] = val[:, c * LANES:(c + 1) * LANES]


def _slab_to_rows(src_ref, slot0, n):
    return jnp.concatenate(
        [src_ref[pl.ds(SUBLANES * slot0 + c, n, stride=SUBLANES), :] for c in range(NCHUNK)], axis=1)


def _rmsnorm(x, g):
    return x * lax.rsqrt(jnp.mean(x * x, axis=-1, keepdims=True) + EPS) * g


def _conv_b_slab(vbuf, hist_rows, w_ref, dst_ref):
    first_tap = HIST - (TAPS_B - 1)
    for s in range(NSEG):
        cur_row0 = (HIST + s * SEG) * SUBLANES
        for t0 in range(0, SEG, CONV_BLOCK):
            ins = []
            for k in range(CONV_BLOCK + TAPS_B - 1):
                u = t0 + first_tap + k
                if u < HIST:
                    row = pl.multiple_of(hist_rows[s] + u * SUBLANES, SUBLANES)
                    ins.append(vbuf[pl.ds(row, SUBLANES), :])
                else:
                    row = cur_row0 + (u - HIST) * SUBLANES
                    ins.append(vbuf[row:row + SUBLANES, :])
            accs = [None] * CONV_BLOCK
            for j in range(TAPS_B):
                wj = w_ref[j]
                for o in range(CONV_BLOCK):
                    term = wj * ins[o + j]
                    accs[o] = term if accs[o] is None else accs[o] + term
            for o in range(CONV_BLOCK):
                row = (s * SEG + t0 + o) * SUBLANES
                dst_ref[row:row + SUBLANES, :] = accs[o]


def _mixer_kernel(n_prompt_tiles, tiles_per_seq, n_prompt_seq, n_sample_seq,
                  xp_ref, xs_ref, sta_ref, stb_ref, gmix_ref, win_ref, bin_ref, wca_ref, woa_ref,
                  wcb_ref, bcb_ref, lng_ref, lnb_ref, wob_ref, bob_ref, wo_ref, gffn_ref, wrt_ref, brt_ref,
                  h_ref, hn_ref, idx_ref, gw_ref, nap_ref, nbp_ref, nas_ref, nbs_ref,
                  cubuf, carry_a, vbuf, ybslab, sa_scr, sb_scr):
    i = pl.program_id(0)
    n_tiles = pl.num_programs(0)
    is_prompt = i < n_prompt_tiles
    seq_start = is_prompt & (lax.rem(i, tiles_per_seq) == 0)
    carry_rows = HIST * SUBLANES

    @pl.when(i == 0)
    def _():
        vbuf[0:carry_rows, :] = jnp.zeros((carry_rows, LANES), F32)
        carry_a[...] = jnp.zeros(carry_a.shape, F32)

    x = jnp.where(is_prompt, xp_ref[...], xs_ref[...])
    n_bf = _rmsnorm(x, gmix_ref[...]).astype(BF16)

    def proj(g):
        cols = slice(g * D, (g + 1) * D)
        return jnp.dot(n_bf, win_ref[:, cols], preferred_element_type=F32) + bin_ref[:, cols]

    cu = proj(1) * proj(2)
    v = proj(3) * jax.nn.sigmoid(proj(4))

    seg_rows = HIST_A + SEG
    prev_a = jnp.where(seq_start, 0.0, carry_a[HIST_A - (TAPS_A - 1):HIST_A, :])
    for s in range(NSEG):
        r0 = s * SEG
        prev = prev_a if s == 0 else cu[r0 - (TAPS_A - 1):r0, :]
        cubuf[s * seg_rows + HIST_A - (TAPS_A - 1):s * seg_rows + HIST_A, :] = jnp.where(is_prompt, prev, sta_ref[s])
        cubuf[s * seg_rows + HIST_A:(s + 1) * seg_rows, :] = cu[r0:r0 + SEG, :]
    carry_a[HIST_A - (TAPS_A - 1):HIST_A, :] = cu[TL - (TAPS_A - 1):TL, :]
    ya_parts = []
    for s in range(NSEG):
        acc = None
        for j in range(TAPS_A):
            lo = s * seg_rows + HIST_A - (TAPS_A - 1) + j
            term = wca_ref[j:j + 1, :] * cubuf[lo:lo + SEG, :]
            acc = term if acc is None else acc + term
        ya_parts.append(acc)
    ya = jnp.concatenate(ya_parts, axis=0)

    vbuf[0:carry_rows, :] = jnp.where(seq_start, 0.0, vbuf[0:carry_rows, :])
    _rows_to_slab(vbuf, HIST, v)
    state_slot0 = HIST + TL
    for s in range(NSEG):
        for c in range(NCHUNK):
            vbuf[pl.ds(SUBLANES * (state_slot0 + s * HIST + HIST - (TAPS_B - 1)) + c, TAPS_B - 1, stride=SUBLANES), :] = (
                stb_ref[s, :, c * LANES:(c + 1) * LANES])
    hist_rows = [jnp.where(is_prompt, s * SEG * SUBLANES, (state_slot0 + s * HIST) * SUBLANES) for s in range(NSEG)]
    _conv_b_slab(vbuf, hist_rows, wcb_ref, ybslab)
    vbuf[0:carry_rows, :] = vbuf[TL * SUBLANES:(TL + HIST) * SUBLANES, :]

    dummy = n_prompt_seq + n_sample_seq
    for s in range(NSEG):
        prompt_slot = i // tiles_per_seq if s == NSEG - 1 else dummy
        slot = jnp.where(is_prompt, prompt_slot, n_prompt_seq + (i - n_prompt_tiles) * NSEG + s)
        r1 = (s + 1) * SEG
        sa_scr[slot] = cu[r1 - (TAPS_A - 1):r1, :]
        sb_scr[slot] = v[r1 - (TAPS_B - 1):r1, :]

    @pl.when(i == n_tiles - 1)
    def _():
        nap_ref[...] = sa_scr[0:n_prompt_seq]
        nbp_ref[...] = sb_scr[0:n_prompt_seq]
        nas_ref[...] = sa_scr[n_prompt_seq:n_prompt_seq + n_sample_seq]
        nbs_ref[...] = sb_scr[n_prompt_seq:n_prompt_seq + n_sample_seq]

    yb = _slab_to_rows(ybslab, 0, TL) + bcb_ref[...]

    out_a = jnp.dot((proj(0) * ya).astype(BF16), woa_ref[...], preferred_element_type=F32)
    mu = jnp.mean(yb, axis=-1, keepdims=True)
    yc = yb - mu
    var = jnp.mean(yc * yc, axis=-1, keepdims=True)
    ln = yc * lax.rsqrt(var + EPS) * lng_ref[...] + lnb_ref[...]
    act = ln * jax.nn.sigmoid(ln)
    out_b = jnp.dot(act.astype(BF16), wob_ref[...], preferred_element_type=F32) + bob_ref[...]
    mix = jax.nn.sigmoid(proj(5)) * out_a + jax.nn.sigmoid(proj(6)) * out_b
    h = x + jnp.dot(mix.astype(BF16), wo_ref[...], preferred_element_type=F32)
    h_ref[...] = h
    hn = _rmsnorm(h, gffn_ref[...])
    _rows_to_slab(hn_ref, 0, hn)

    logits = lax.dot_general(wrt_ref[...], hn.astype(BF16), (((1,), (1,)), ((), ())),
                             preferred_element_type=F32) + brt_ref[...]
    iota_e = lax.broadcasted_iota(I32, (N_EXPERTS, TL), 0).astype(F32)
    vals, idxs = [], []
    for _ in range(TOP_K):
        m = jnp.max(logits, axis=0, keepdims=True)
        sel = jnp.min(jnp.where(logits == m, iota_e, float(N_EXPERTS)), axis=0, keepdims=True)
        vals.append(m)
        idxs.append(sel)
        logits = jnp.where(iota_e == sel, -jnp.inf, logits)
    ex = [jnp.exp(val - vals[0]) for val in vals]
    denom = ex[0] + ex[1] + ex[2] + ex[3]
    for k in range(TOP_K):
        idx_ref[k:k + 1, :] = idxs[k].astype(I32)
        gw_ref[k:k + 1, :] = ex[k] / denom


def _route_kernel(n_tokens, n_blocks_pad, idx_ref, dest_ref, be_ref, nu_ref, nv_ref, pad_start_ref, pad_len_ref, tri_scr):
    n_chunks = n_tokens // ROUTE_CHUNK
    iota_e = lax.broadcasted_iota(I32, (N_EXPERTS, ROUTE_CHUNK), 0)

    def masks(c):
        sl = pl.ds(pl.multiple_of(c * ROUTE_CHUNK, ROUTE_CHUNK), ROUTE_CHUNK)
        return sl, [idx_ref[k:k + 1, sl] == iota_e for k in range(TOP_K)]

    def onehot_sum(ms):
        tot = ms[0].astype(F32)
        for k in range(1, TOP_K):
            tot = tot + ms[k].astype(F32)
        return tot

    def count_body(c, acc):
        _, ms = masks(c)
        return acc + onehot_sum(ms)

    acc = lax.fori_loop(0, n_chunks, count_body, jnp.zeros((N_EXPERTS, ROUTE_CHUNK), F32))
    counts = jnp.sum(acc, axis=1, keepdims=True)
    nblk = jnp.floor((counts + (ROWS - 1)) * (1.0 / ROWS))
    r = lax.broadcasted_iota(I32, (N_EXPERTS, N_EXPERTS), 0)
    cidx = lax.broadcasted_iota(I32, (N_EXPERTS, N_EXPERTS), 1)
    lower = (cidx <= r).astype(F32)
    pend = jnp.dot(lower, jnp.broadcast_to(nblk, (N_EXPERTS, LANES)),
                   precision=lax.Precision.HIGHEST, preferred_element_type=F32)
    pend1 = pend[:, 0:1]
    pstart_rows = (pend1 - nblk) * ROWS
    pad_start_ref[...] = jnp.broadcast_to(pstart_rows + counts, (N_EXPERTS, LANES)).astype(I32)
    pad_len_ref[...] = jnp.broadcast_to(nblk * ROWS - counts, (N_EXPERTS, LANES)).astype(I32)

    bvec = lax.broadcasted_iota(I32, (N_EXPERTS, n_blocks_pad), 1).astype(F32)
    be = jnp.sum((pend1 <= bvec).astype(F32), axis=0, keepdims=True)
    be = jnp.minimum(be, N_EXPERTS - 1)
    be_ref[...] = be.astype(I32)
    of_block = lax.broadcasted_iota(I32, (N_EXPERTS, n_blocks_pad), 0).astype(F32) == be
    count_b = jnp.sum(jnp.where(of_block, counts, 0.0), axis=0, keepdims=True)
    first_b = jnp.sum(jnp.where(of_block, pend1 - nblk, 0.0), axis=0, keepdims=True)
    nv_ref[...] = jnp.clip(count_b - ROWS * (bvec[0:1, :] - first_b), 0.0, float(ROWS)).astype(I32)
    last = lax.broadcasted_iota(I32, (N_EXPERTS, LANES), 0) == N_EXPERTS - 1
    nu_ref[...] = jnp.sum(jnp.where(last, pend, 0.0), axis=0, keepdims=True).astype(I32)

    tr = lax.broadcasted_iota(I32, (ROUTE_CHUNK, ROUTE_CHUNK), 0)
    tc = lax.broadcasted_iota(I32, (ROUTE_CHUNK, ROUTE_CHUNK), 1)
    tri_scr[...] = (tr < tc).astype(BF16)

    def dest_body(c, carry):
        sl, ms = masks(c)
        tot = onehot_sum(ms)
        before = jnp.dot(tot.astype(BF16), tri_scr[...], preferred_element_type=F32)
        pos = pstart_rows + carry + before
        for k in range(TOP_K):
            dest_k = jnp.sum(jnp.where(ms[k], pos, 0.0), axis=0, keepdims=True).astype(I32)
            for part in range(ROUTE_CHUNK // TL):
                dest_ref[c * (ROUTE_CHUNK // TL) + part, :, k * TL:(k + 1) * TL] = dest_k[:, part * TL:(part + 1) * TL]
        return carry + jnp.sum(tot, axis=1, keepdims=True)

    lax.fori_loop(0, n_chunks, dest_body, jnp.zeros((N_EXPERTS, 1), F32))


def _dispatch_kernel(n_blocks, pad_start_ref, pad_len_ref, nu_ref, dest_ref, hn_ref, xb_hbm, ztile, sem, pad_sem):
    def row_copy(tt, k):
        src = pl.multiple_of(tt * SUBLANES, SUBLANES)
        dst = pl.multiple_of(dest_ref[0, 0, k * TL + tt] * SUBLANES, SUBLANES)
        return pltpu.make_async_copy(hn_ref.at[pl.ds(src, SUBLANES), :], xb_hbm.at[pl.ds(dst, SUBLANES), :], sem)

    def body(q, carry):
        for u in range(ISSUE_UNROLL):
            for k in range(TOP_K):
                row_copy(q * ISSUE_UNROLL + u, k).start(priority=k % 2)
        return carry

    lax.fori_loop(0, TL // ISSUE_UNROLL, body, 0)

    @pl.when(pl.program_id(0) == 0)
    def _():
        ztile[...] = jnp.zeros(ztile.shape, F32)

        def zero_copy(row, n_slabs):
            dst = pl.multiple_of(row * SUBLANES, SUBLANES)
            return pltpu.make_async_copy(ztile.at[pl.ds(0, n_slabs * SUBLANES), :],
                                         xb_hbm.at[pl.ds(dst, n_slabs * SUBLANES), :], pad_sem)

        def pad_pass(act):
            def per_expert(e, carry):
                row, length = pad_start_ref[e], pad_len_ref[e]
                for bit in reversed(range(ROWS.bit_length() - 1)):
                    take = (length & (1 << bit)) != 0

                    @pl.when(take)
                    def _():
                        act(zero_copy(row, 1 << bit))

                    row = row + jnp.where(take, 1 << bit, 0)
                return carry

            lax.fori_loop(0, N_EXPERTS, per_expert, 0)

        def tail_pass(act):
            def per_block(blk, carry):
                act(zero_copy(blk * ROWS, ROWS))
                return carry

            lax.fori_loop(nu_ref[0], n_blocks, per_block, 0)

        for act in (lambda c: c.start(), lambda c: c.wait()):
            pad_pass(act)
            tail_pass(act)

    n = TL * SUBLANES
    for k in range(TOP_K):
        pltpu.make_async_copy(hn_ref, xb_hbm.at[pl.ds(0, n), :], sem).wait()


def _expert_kernel(be_ref, nu_ref, nv_ref, xs_ref, wgu_hbm, bgu_ref, wd_hbm, bd_ref, o_ref,
                   wgu_f32, wd_f32, wgu_bf, wd_bf, slot_ref, sems):
    b = pl.program_id(0)
    n_used = nu_ref[0]
    e = be_ref[b]
    prev = be_ref[jnp.maximum(b - 1, 0)]
    active = b < n_used

    def weight_copies(expert, slot):
        return (pltpu.make_async_copy(wgu_hbm.at[expert], wgu_f32.at[slot], sems.at[0, slot]),
                pltpu.make_async_copy(wd_hbm.at[expert], wd_f32.at[slot], sems.at[1, slot]))

    @pl.when(active & (b == 0))
    def _():
        slot_ref[0] = 0
        for c in weight_copies(e, 0):
            c.start()

    @pl.when(active & ((b == 0) | (e != prev)))
    def _():
        slot = slot_ref[0]
        for c in weight_copies(e, slot):
            c.wait()
        last = n_used - 1
        nxt = lax.while_loop(lambda j: (j <= last) & (be_ref[jnp.minimum(j, last)] == e), lambda j: j + 1, b + 1)

        @pl.when(nxt <= last)
        def _():
            for c in weight_copies(be_ref[jnp.minimum(nxt, last)], 1 - slot):
                c.start(priority=1)

        wgu_bf[...] = wgu_f32[slot].astype(BF16)
        wd_bf[...] = wd_f32[slot].astype(BF16)
        slot_ref[0] = 1 - slot

    def mlp(n_rows):
        xb = _slab_to_rows(xs_ref, 0, n_rows).astype(BF16)
        gu = jnp.dot(xb, wgu_bf[...], preferred_element_type=F32) + bgu_ref[0]
        gate = jnp.minimum(gu[:, :D], SWIGLU_LIMIT)
        up = jnp.clip(gu[:, D:], -SWIGLU_LIMIT, SWIGLU_LIMIT)
        hidden = (up + 1.0) * gate * jax.nn.sigmoid(SWIGLU_ALPHA * gate)
        out = jnp.dot(hidden.astype(BF16), wd_bf[...], preferred_element_type=F32) + bd_ref[0]
        _rows_to_slab(o_ref, 0, out)

    n_valid = nv_ref[b]
    for n_rows in range(ROW_GRANULE, ROWS + 1, ROW_GRANULE):
        @pl.when(active & (n_valid > n_rows - ROW_GRANULE) & (n_valid <= n_rows))
        def _(n_rows=n_rows):
            mlp(n_rows)
            if n_rows < ROWS:
                o_ref[n_rows * SUBLANES:ROWS * SUBLANES, :] = jnp.zeros(((ROWS - n_rows) * SUBLANES, LANES), F32)

    @pl.when(jnp.logical_not(active))
    def _():
        o_ref[...] = jnp.zeros(o_ref.shape, F32)


def _combine_kernel(n_tiles, n_prompt_tiles, dest_ref, dest_next_ref, h_ref, gw_ref, gfin_ref, ob_hbm,
                    yp_ref, ys_ref, gath0, gath1, sems):
    i = pl.program_id(0)
    n = TOP_K * TL * SUBLANES

    def row_copy(table_ref, gath, sem, tt, k):
        src = pl.multiple_of(table_ref[0, 0, k * TL + tt] * SUBLANES, SUBLANES)
        dst = pl.multiple_of((k * TL + tt) * SUBLANES, SUBLANES)
        return pltpu.make_async_copy(ob_hbm.at[pl.ds(src, SUBLANES), :], gath.at[pl.ds(dst, SUBLANES), :], sem)

    def issue(table_ref, gath, sem):
        def body(q, carry):
            for u in range(ISSUE_UNROLL):
                for k in range(TOP_K):
                    row_copy(table_ref, gath, sem, q * ISSUE_UNROLL + u, k).start(priority=k % 2)
            return carry

        lax.fori_loop(0, TL // ISSUE_UNROLL, body, 0)

    def wait_tile(gath, sem):
        pltpu.make_async_copy(ob_hbm.at[pl.ds(0, n), :], gath, sem).wait()

    def step(cur, cur_sem, nxt, nxt_sem):
        wait_tile(cur, cur_sem)
        for tt in range(TL):
            for k in range(TOP_K):
                row_copy(dest_next_ref, nxt, nxt_sem, tt, k).start(priority=k % 2)
        acc = h_ref[...]
        gw = gw_ref[...]
        for k in range(TOP_K):
            acc = acc + gw[:, k:k + 1] * _slab_to_rows(cur, k * TL, TL)
        y = _rmsnorm(acc, gfin_ref[...])
        is_prompt = i < n_prompt_tiles

        @pl.when(is_prompt)
        def _():
            yp_ref[...] = y

        @pl.when(jnp.logical_not(is_prompt))
        def _():
            ys_ref[...] = y

        @pl.when(i == n_tiles - 1)
        def _():
            wait_tile(nxt, nxt_sem)

    @pl.when(i == 0)
    def _():
        issue(dest_ref, gath0, sems.at[0])

    parity = lax.rem(i, 2)

    @pl.when(parity == 0)
    def _():
        step(gath0, sems.at[0], gath1, sems.at[1])

    @pl.when(parity == 1)
    def _():
        step(gath1, sems.at[1], gath0, sems.at[0])


def _const_spec(shape):
    nd = len(shape)
    return pl.BlockSpec(shape, lambda *_: (0,) * nd, pipeline_mode=pl.Buffered(1))


def kernel(x_prompt, x_sample, state_conv_a, state_conv_b, g_mix, w_in, b_in, w_conv_a, w_out_a, w_conv_b, b_conv_b, ln_g, ln_b, w_out_b, b_out_b, w_o, g_ffn, w_router, b_router, w_gate_up, b_gate_up, w_down, b_down, g_final):
    depth = g_mix.shape[0]
    assert depth == 1
    bp, lp, d = x_prompt.shape
    bs, ls, _ = x_sample.shape
    assert d == D and w_in.shape[2] == 7 * D
    tp, ts = bp * lp, bs * ls
    t = tp + ts
    assert lp % TL == 0 and ls == SEG and ts % TL == 0 and t % ROUTE_CHUNK == 0 and ROUTE_CHUNK % TL == 0
    assert TAPS_B - 1 <= HIST <= SEG and TAPS_A - 1 <= HIST_A
    npt, nst = tp // TL, ts // TL
    nt = npt + nst
    tiles_per_seq = lp // TL
    n_blocks = (t * TOP_K) // ROWS + N_EXPERTS
    n_blocks_pad = -(-n_blocks // LANES) * LANES
    p_rows = n_blocks * ROWS

    row = lambda a: a.reshape(1, -1)
    f32_spec = lambda: _const_spec((1, D))

    mixer = pl.pallas_call(
        functools.partial(_mixer_kernel, npt, tiles_per_seq, bp, bs),
        grid=(nt,),
        in_specs=[
            pl.BlockSpec((TL, D), lambda i: (jnp.minimum(i, npt - 1), 0)),
            pl.BlockSpec((TL, D), lambda i: (jnp.maximum(i - npt, 0), 0)),
            pl.BlockSpec((NSEG, TAPS_A - 1, D), lambda i: (jnp.maximum(i - npt, 0), 0, 0)),
            pl.BlockSpec((NSEG, TAPS_B - 1, D), lambda i: (jnp.maximum(i - npt, 0), 0, 0)),
            f32_spec(),
            _const_spec((D, 7 * D)),
            _const_spec((1, 7 * D)),
            _const_spec((TAPS_A, D)),
            _const_spec((D, D)),
            _const_spec((TAPS_B, SUBLANES, LANES)),
            f32_spec(), f32_spec(), f32_spec(),
            _const_spec((D, D)),
            f32_spec(),
            _const_spec((D, D)),
            f32_spec(),
            _const_spec((N_EXPERTS, D)),
            _const_spec((N_EXPERTS, 1)),
        ],
        out_specs=[
            pl.BlockSpec((TL, D), lambda i: (i, 0)),
            pl.BlockSpec((TL * SUBLANES, LANES), lambda i: (i, 0)),
            pl.BlockSpec((TOP_K, TL), lambda i: (0, i)),
            pl.BlockSpec((TOP_K, TL), lambda i: (0, i)),
            pl.BlockSpec((bp, TAPS_A - 1, D), lambda i: (0, 0, 0)),
            pl.BlockSpec((bp, TAPS_B - 1, D), lambda i: (0, 0, 0)),
            pl.BlockSpec((bs, TAPS_A - 1, D), lambda i: (0, 0, 0)),
            pl.BlockSpec((bs, TAPS_B - 1, D), lambda i: (0, 0, 0)),
        ],
        out_shape=[
            jax.ShapeDtypeStruct((t, D), F32),
            jax.ShapeDtypeStruct((t * SUBLANES, LANES), F32),
            jax.ShapeDtypeStruct((TOP_K, t), I32),
            jax.ShapeDtypeStruct((TOP_K, t), F32),
            jax.ShapeDtypeStruct((bp, TAPS_A - 1, D), F32),
            jax.ShapeDtypeStruct((bp, TAPS_B - 1, D), F32),
            jax.ShapeDtypeStruct((bs, TAPS_A - 1, D), F32),
            jax.ShapeDtypeStruct((bs, TAPS_B - 1, D), F32),
        ],
        scratch_shapes=[
            pltpu.VMEM((NSEG * (HIST_A + SEG), D), F32),
            pltpu.VMEM((HIST_A, D), F32),
            pltpu.VMEM(((HIST + TL + NSEG * HIST) * SUBLANES, LANES), F32),
            pltpu.VMEM((TL * SUBLANES, LANES), F32),
            pltpu.VMEM((bp + bs + 1, TAPS_A - 1, D), F32),
            pltpu.VMEM((bp + bs + 1, TAPS_B - 1, D), F32),
        ],
        compiler_params=pltpu.CompilerParams(dimension_semantics=("arbitrary",), vmem_limit_bytes=VMEM_LIMIT),
        name="mixer",
    )
    h, hn_slab, idx_t, gw_t, na_p, nb_p, na_s, nb_s = mixer(
        x_prompt.reshape(tp, D), x_sample.reshape(ts, D), state_conv_a[0], state_conv_b[0],
        row(g_mix), w_in[0].astype(BF16), row(b_in), w_conv_a[0],
        w_out_a[0].astype(BF16), w_conv_b[0].reshape(TAPS_B, SUBLANES, LANES), row(b_conv_b), row(ln_g), row(ln_b),
        w_out_b[0].astype(BF16), row(b_out_b), w_o[0].astype(BF16), row(g_ffn),
        w_router[0].T.astype(BF16), b_router[0].reshape(N_EXPERTS, 1))

    dest_sm, block_e, n_used, n_valid, pad_start, pad_len = pl.pallas_call(
        functools.partial(_route_kernel, t, n_blocks_pad),
        out_shape=[
            jax.ShapeDtypeStruct((nt, 1, TOP_K * TL), I32),
            jax.ShapeDtypeStruct((1, n_blocks_pad), I32),
            jax.ShapeDtypeStruct((1, LANES), I32),
            jax.ShapeDtypeStruct((1, n_blocks_pad), I32),
            jax.ShapeDtypeStruct((N_EXPERTS, LANES), I32),
            jax.ShapeDtypeStruct((N_EXPERTS, LANES), I32),
        ],
        scratch_shapes=[pltpu.VMEM((ROUTE_CHUNK, ROUTE_CHUNK), BF16)],
        compiler_params=pltpu.CompilerParams(vmem_limit_bytes=VMEM_LIMIT),
        name="route",
    )(idx_t)
    block_e = block_e[0, :n_blocks]
    n_valid = n_valid[0, :n_blocks]
    n_used = n_used[0, :1]
    pad_start, pad_len = pad_start[:, 0], pad_len[:, 0]

    smem_tile_spec = pl.BlockSpec((1, 1, TOP_K * TL), lambda i: (i, 0, 0), memory_space=pltpu.SMEM)
    any_spec = pl.BlockSpec(memory_space=pl.ANY)
    x_buf = pl.pallas_call(
        functools.partial(_dispatch_kernel, n_blocks),
        grid_spec=pltpu.PrefetchScalarGridSpec(
            num_scalar_prefetch=3,
            grid=(nt,),
            in_specs=[
                pl.BlockSpec((1, 1, TOP_K * TL), lambda i, *_: (i, 0, 0), memory_space=pltpu.SMEM),
                pl.BlockSpec((TL * SUBLANES, LANES), lambda i, *_: (i, 0)),
            ],
            out_specs=any_spec,
            scratch_shapes=[pltpu.VMEM((ROWS * SUBLANES, LANES), F32), pltpu.SemaphoreType.DMA,
                            pltpu.SemaphoreType.DMA],
        ),
        out_shape=jax.ShapeDtypeStruct((p_rows * SUBLANES, LANES), F32),
        compiler_params=pltpu.CompilerParams(dimension_semantics=("arbitrary",)),
        name="dispatch",
    )(pad_start, pad_len, n_used, dest_sm, hn_slab)

    def blk(b, be, nu):
        return jnp.maximum(jnp.minimum(b, nu[0] - 1), 0)

    out_buf = pl.pallas_call(
        _expert_kernel,
        grid_spec=pltpu.PrefetchScalarGridSpec(
            num_scalar_prefetch=3,
            grid=(n_blocks,),
            in_specs=[
                pl.BlockSpec((ROWS * SUBLANES, LANES), lambda b, be, nu, nv: (blk(b, be, nu), 0)),
                any_spec,
                pl.BlockSpec((1, 1, 2 * D), lambda b, be, nu, nv: (be[blk(b, be, nu)], 0, 0)),
                any_spec,
                pl.BlockSpec((1, 1, D), lambda b, be, nu, nv: (be[blk(b, be, nu)], 0, 0)),
            ],
            out_specs=pl.BlockSpec((ROWS * SUBLANES, LANES), lambda b, be, nu, nv: (b, 0)),
            scratch_shapes=[
                pltpu.VMEM((2, D, 2 * D), F32), pltpu.VMEM((2, D, D), F32),
                pltpu.VMEM((D, 2 * D), BF16), pltpu.VMEM((D, D), BF16),
                pltpu.SMEM((1,), I32), pltpu.SemaphoreType.DMA((2, 2)),
            ],
        ),
        out_shape=jax.ShapeDtypeStruct((p_rows * SUBLANES, LANES), F32),
        compiler_params=pltpu.CompilerParams(dimension_semantics=("arbitrary",), vmem_limit_bytes=VMEM_LIMIT),
        name="experts",
    )(block_e, n_used, n_valid, x_buf, w_gate_up[0], b_gate_up[0].reshape(N_EXPERTS, 1, 2 * D), w_down[0],
      b_down[0].reshape(N_EXPERTS, 1, D))

    y_p, y_s = pl.pallas_call(
        functools.partial(_combine_kernel, nt, npt),
        grid=(nt,),
        in_specs=[
            smem_tile_spec,
            pl.BlockSpec((1, 1, TOP_K * TL), lambda i: (jnp.minimum(i + 1, nt - 1), 0, 0), memory_space=pltpu.SMEM),
            pl.BlockSpec((TL, D), lambda i: (i, 0)),
            pl.BlockSpec((TL, TOP_K), lambda i: (i, 0)),
            pl.BlockSpec((1, D), lambda i: (0, 0)),
            any_spec,
        ],
        out_specs=[
            pl.BlockSpec((TL, D), lambda i: (jnp.minimum(i, npt - 1), 0)),
            pl.BlockSpec((TL, D), lambda i: (jnp.maximum(i - npt, 0), 0)),
        ],
        out_shape=[jax.ShapeDtypeStruct((tp, D), F32), jax.ShapeDtypeStruct((ts, D), F32)],
        scratch_shapes=[pltpu.VMEM((TOP_K * TL * SUBLANES, LANES), F32), pltpu.VMEM((TOP_K * TL * SUBLANES, LANES), F32),
                        pltpu.SemaphoreType.DMA((2,))],
        compiler_params=pltpu.CompilerParams(dimension_semantics=("arbitrary",), vmem_limit_bytes=VMEM_LIMIT),
        name="combine",
    )(dest_sm, dest_sm, h, gw_t.T, row(g_final), out_buf)

    return (y_p.reshape(bp, lp, D), y_s.reshape(bs, ls, D), na_p[None], nb_p[None], na_s[None], nb_s[None])
```

```python
import functools

import jax
import jax.numpy as jnp
from jax import lax
from jax.experimental import pallas as pl
from jax.experimental.pallas import tpu as pltpu

F32 = jnp.float32
BF16 = jnp.bfloat16
I32 = jnp.int32

D = 1024
LANES = 128
SUBLANES = 8
NCHUNK = D // LANES
N_EXPERTS = 32
TOP_K = 4
TAPS_A = 3
TAPS_B = 31
EPS = 1e-5
SWIGLU_LIMIT = 7.0
SWIGLU_ALPHA = 1.702

TL = 256
SEG = 32
NSEG = TL // SEG
HIST = 32
HIST_A = 8
CONV_BLOCK = 4
ROWS = 512
ROW_GRANULE = 128
ISSUE_UNROLL = 8
ROUTE_CHUNK = 512
VMEM_LIMIT = 56 * 1024 * 1024


def _rows_to_slab(dst_ref, slot0, val):
    n = val.shape[0]
    for c in range(NCHUNK):
        dst_ref[pl.ds(SUBLANES * slot0 + c, n, stride=SUBLANES), :] = val[:, c * LANES:(c + 1) * LANES]


def _slab_to_rows(src_ref, slot0, n):
    return jnp.concatenate(
        [src_ref[pl.ds(SUBLANES * slot0 + c, n, stride=SUBLANES), :] for c in range(NCHUNK)], axis=1)


def _rmsnorm(x, g):
    return x * lax.rsqrt(jnp.mean(x * x, axis=-1, keepdims=True) + EPS) * g


def _conv_b_slab(vbuf, hist_rows, w_ref, dst_ref):
    first_tap = HIST - (TAPS_B - 1)
    for s in range(NSEG):
        cur_row0 = (HIST + s * SEG) * SUBLANES
        for t0 in range(0, SEG, CONV_BLOCK):
            ins = []
            for k in range(CONV_BLOCK + TAPS_B - 1):
                u = t0 + first_tap + k
                if u < HIST:
                    row = pl.multiple_of(hist_rows[s] + u * SUBLANES, SUBLANES)
                    ins.append(vbuf[pl.ds(row, SUBLANES), :])
                else:
                    row = cur_row0 + (u - HIST) * SUBLANES
                    ins.append(vbuf[row:row + SUBLANES, :])
            accs = [None] * CONV_BLOCK
            for j in range(TAPS_B):
                wj = w_ref[j]
                for o in range(CONV_BLOCK):
                    term = wj * ins[o + j]
                    accs[o] = term if accs[o] is None else accs[o] + term
            for o in range(CONV_BLOCK):
                row = (s * SEG + t0 + o) * SUBLANES
                dst_ref[row:row + SUBLANES, :] = accs[o]


def _mixer_kernel(n_prompt_tiles, tiles_per_seq, n_prompt_seq, n_sample_seq,
                  xp_ref, xs_ref, sta_ref, stb_ref, gmix_ref, win_ref, bin_ref, wca_ref, woa_ref,
                  wcb_ref, bcb_ref, lng_ref, lnb_ref, wob_ref, bob_ref, wo_ref, gffn_ref, wrt_ref, brt_ref,
                  h_ref, hn_ref, idx_ref, gw_ref, nap_ref, nbp_ref, nas_ref, nbs_ref,
                  cubuf, carry_a, vbuf, ybslab, sa_scr, sb_scr):
    i = pl.program_id(0)
    n_tiles = pl.num_programs(0)
    is_prompt = i < n_prompt_tiles
    seq_start = is_prompt & (lax.rem(i, tiles_per_seq) == 0)
    carry_rows = HIST * SUBLANES

    @pl.when(i == 0)
    def _():
        vbuf[0:carry_rows, :] = jnp.zeros((carry_rows, LANES), F32)
        carry_a[...] = jnp.zeros(carry_a.shape, F32)

    x = jnp.where(is_prompt, xp_ref[...], xs_ref[...])
    n_bf = _rmsnorm(x, gmix_ref[...]).astype(BF16)

    def proj(g):
        cols = slice(g * D, (g + 1) * D)
        return jnp.dot(n_bf, win_ref[:, cols], preferred_element_type=F32) + bin_ref[:, cols]

    cu = proj(1) * proj(2)
    v = proj(3) * jax.nn.sigmoid(proj(4))

    seg_rows = HIST_A + SEG
    prev_a = jnp.where(seq_start, 0.0, carry_a[HIST_A - (TAPS_A - 1):HIST_A, :])
    for s in range(NSEG):
        r0 = s * SEG
        prev = prev_a if s == 0 else cu[r0 - (TAPS_A - 1):r0, :]
        cubuf[s * seg_rows + HIST_A - (TAPS_A - 1):s * seg_rows + HIST_A, :] = jnp.where(is_prompt, prev, sta_ref[s])
        cubuf[s * seg_rows + HIST_A:(s + 1) * seg_rows, :] = cu[r0:r0 + SEG, :]
    carry_a[HIST_A - (TAPS_A - 1):HIST_A, :] = cu[TL - (TAPS_A - 1):TL, :]
    ya_parts = []
    for s in range(NSEG):
        acc = None
        for j in range(TAPS_A):
            lo = s * seg_rows + HIST_A - (TAPS_A - 1) + j
            term = wca_ref[j:j + 1, :] * cubuf[lo:lo + SEG, :]
            acc = term if acc is None else acc + term
        ya_parts.append(acc)
    ya = jnp.concatenate(ya_parts, axis=0)

    vbuf[0:carry_rows, :] = jnp.where(seq_start, 0.0, vbuf[0:carry_rows, :])
    _rows_to_slab(vbuf, HIST, v)
    state_slot0 = HIST + TL
    for s in range(NSEG):
        for c in range(NCHUNK):
            vbuf[pl.ds(SUBLANES * (state_slot0 + s * HIST + HIST - (TAPS_B - 1)) + c, TAPS_B - 1, stride=SUBLANES), :] = (
                stb_ref[s, :, c * LANES:(c + 1) * LANES])
    hist_rows = [jnp.where(is_prompt, s * SEG * SUBLANES, (state_slot0 + s * HIST) * SUBLANES) for s in range(NSEG)]
    _conv_b_slab(vbuf, hist_rows, wcb_ref, ybslab)
    vbuf[0:carry_rows, :] = vbuf[TL * SUBLANES:(TL + HIST) * SUBLANES, :]

    dummy = n_prompt_seq + n_sample_seq
    for s in range(NSEG):
        prompt_slot = i // tiles_per_seq if s == NSEG - 1 else dummy
        slot = jnp.where(is_prompt, prompt_slot, n_prompt_seq + (i - n_prompt_tiles) * NSEG + s)
        r1 = (s + 1) * SEG
        sa_scr[slot] = cu[r1 - (TAPS_A - 1):r1, :]
        sb_scr[slot] = v[r1 - (TAPS_B - 1):r1, :]

    @pl.when(i == n_tiles - 1)
    def _():
        nap_ref[...] = sa_scr[0:n_prompt_seq]
        nbp_ref[...] = sb_scr[0:n_prompt_seq]
        nas_ref[...] = sa_scr[n_prompt_seq:n_prompt_seq + n_sample_seq]
        nbs_ref[...] = sb_scr[n_prompt_seq:n_prompt_seq + n_sample_seq]

    yb = _slab_to_rows(ybslab, 0, TL) + bcb_ref[...]

    out_a = jnp.dot((proj(0) * ya).astype(BF16), woa_ref[...], preferred_element_type=F32)
    mu = jnp.mean(yb, axis=-1, keepdims=True)
    yc = yb - mu
    var = jnp.mean(yc * yc, axis=-1, keepdims=True)
    ln = yc * lax.rsqrt(var + EPS) * lng_ref[...] + lnb_ref[...]
    act = ln * jax.nn.sigmoid(ln)
    out_b = jnp.dot(act.astype(BF16), wob_ref[...], preferred_element_type=F32) + bob_ref[...]
    mix = jax.nn.sigmoid(proj(5)) * out_a + jax.nn.sigmoid(proj(6)) * out_b
    h = x + jnp.dot(mix.astype(BF16), wo_ref[...], preferred_element_type=F32)
    h_ref[...] = h
    hn = _rmsnorm(h, gffn_ref[...])
    _rows_to_slab(hn_ref, 0, hn)

    logits = lax.dot_general(wrt_ref[...], hn.astype(BF16), (((1,), (1,)), ((), ())),
                             preferred_element_type=F32) + brt_ref[...]
    iota_e = lax.broadcasted_iota(I32, (N_EXPERTS, TL), 0).astype(F32)
    vals, idxs = [], []
    for _ in range(TOP_K):
        m = jnp.max(logits, axis=0, keepdims=True)
        sel = jnp.min(jnp.where(logits == m, iota_e, float(N_EXPERTS)), axis=0, keepdims=True)
        vals.append(m)
        idxs.append(sel)
        logits = jnp.where(iota_e == sel, -jnp.inf, logits)
    ex = [jnp.exp(val - vals[0]) for val in vals]
    denom = ex[0] + ex[1] + ex[2] + ex[3]
    for k in range(TOP_K):
        idx_ref[k:k + 1, :] = idxs[k].astype(I32)
        gw_ref[k:k + 1, :] = ex[k] / denom


def _route_kernel(n_tokens, n_blocks_pad, idx_ref, dest_ref, be_ref, nu_ref, nv_ref, pad_start_ref, pad_len_ref, tri_scr):
    n_chunks = n_tokens // ROUTE_CHUNK
    iota_e = lax.broadcasted_iota(I32, (N_EXPERTS, ROUTE_CHUNK), 0)

    def masks(c):
        sl = pl.ds(pl.multiple_of(c * ROUTE_CHUNK, ROUTE_CHUNK), ROUTE_CHUNK)
        return sl, [idx_ref[k:k + 1, sl] == iota_e for k in range(TOP_K)]

    def onehot_sum(ms):
        tot = ms[0].astype(F32)
        for k in range(1, TOP_K):
            tot = tot + ms[k].astype(F32)
        return tot

    def count_body(c, acc):
        _, ms = masks(c)
        return acc + onehot_sum(ms)

    acc = lax.fori_loop(0, n_chunks, count_body, jnp.zeros((N_EXPERTS, ROUTE_CHUNK), F32))
    counts = jnp.sum(acc, axis=1, keepdims=True)
    nblk = jnp.floor((counts + (ROWS - 1)) * (1.0 / ROWS))
    r = lax.broadcasted_iota(I32, (N_EXPERTS, N_EXPERTS), 0)
    cidx = lax.broadcasted_iota(I32, (N_EXPERTS, N_EXPERTS), 1)
    lower = (cidx <= r).astype(F32)
    pend = jnp.dot(lower, jnp.broadcast_to(nblk, (N_EXPERTS, LANES)),
                   precision=lax.Precision.HIGHEST, preferred_element_type=F32)
    pend1 = pend[:, 0:1]
    pstart_rows = (pend1 - nblk) * ROWS
    pad_start_ref[...] = jnp.broadcast_to(pstart_rows + counts, (N_EXPERTS, LANES)).astype(I32)
    pad_len_ref[...] = jnp.broadcast_to(nblk * ROWS - counts, (N_EXPERTS, LANES)).astype(I32)

    bvec = lax.broadcasted_iota(I32, (N_EXPERTS, n_blocks_pad), 1).astype(F32)
    be = jnp.sum((pend1 <= bvec).astype(F32), axis=0, keepdims=True)
    be = jnp.minimum(be, N_EXPERTS - 1)
    be_ref[...] = be.astype(I32)
    of_block = lax.broadcasted_iota(I32, (N_EXPERTS, n_blocks_pad), 0).astype(F32) == be
    count_b = jnp.sum(jnp.where(of_block, counts, 0.0), axis=0, keepdims=True)
    first_b = jnp.sum(jnp.where(of_block, pend1 - nblk, 0.0), axis=0, keepdims=True)
    nv_ref[...] = jnp.clip(count_b - ROWS * (bvec[0:1, :] - first_b), 0.0, float(ROWS)).astype(I32)
    last = lax.broadcasted_iota(I32, (N_EXPERTS, LANES), 0) == N_EXPERTS - 1
    nu_ref[...] = jnp.sum(jnp.where(last, pend, 0.0), axis=0, keepdims=True).astype(I32)

    tr = lax.broadcasted_iota(I32, (ROUTE_CHUNK, ROUTE_CHUNK), 0)
    tc = lax.broadcasted_iota(I32, (ROUTE_CHUNK, ROUTE_CHUNK), 1)
    tri_scr[...] = (tr < tc).astype(BF16)

    def dest_body(c, carry):
        sl, ms = masks(c)
        tot = onehot_sum(ms)
        before = jnp.dot(tot.astype(BF16), tri_scr[...], preferred_element_type=F32)
        pos = pstart_rows + carry + before
        for k in range(TOP_K):
            dest_k = jnp.sum(jnp.where(ms[k], pos, 0.0), axis=0, keepdims=True).astype(I32)
            for part in range(ROUTE_CHUNK // TL):
                dest_ref[c * (ROUTE_CHUNK // TL) + part, :, k * TL:(k + 1) * TL] = dest_k[:, part * TL:(part + 1) * TL]
        return carry + jnp.sum(tot, axis=1, keepdims=True)

    lax.fori_loop(0, n_chunks, dest_body, jnp.zeros((N_EXPERTS, 1), F32))


def _dispatch_kernel(n_blocks, pad_start_ref, pad_len_ref, nu_ref, dest_ref, hn_ref, xb_hbm, ztile, sem, pad_sem):
    def row_copy(tt, k):
        src = pl.multiple_of(tt * SUBLANES, SUBLANES)
        dst = pl.multiple_of(dest_ref[0, 0, k * TL + tt] * SUBLANES, SUBLANES)
        return pltpu.make_async_copy(hn_ref.at[pl.ds(src, SUBLANES), :], xb_hbm.at[pl.ds(dst, SUBLANES), :], sem)

    def body(q, carry):
        for u in range(ISSUE_UNROLL):
            for k in range(TOP_K):
                row_copy(q * ISSUE_UNROLL + u, k).start(priority=k % 2)
        return carry

    lax.fori_loop(0, TL // ISSUE_UNROLL, body, 0)

    def zero_copy(row, n_slabs):
        dst = pl.multiple_of(row * SUBLANES, SUBLANES)
        return pltpu.make_async_copy(ztile.at[pl.ds(0, n_slabs * SUBLANES), :],
                                     xb_hbm.at[pl.ds(dst, n_slabs * SUBLANES), :], pad_sem)

    def pad_pass(act):
        def per_expert(e, carry):
            row, length = pad_start_ref[e], pad_len_ref[e]
            for bit in reversed(range(ROWS.bit_length() - 1)):
                take = (length & (1 << bit)) != 0

                @pl.when(take)
                def _():
                    act(zero_copy(row, 1 << bit))

                row = row + jnp.where(take, 1 << bit, 0)
            return carry

        lax.fori_loop(0, N_EXPERTS, per_expert, 0)

    def tail_pass(act):
        def per_block(blk, carry):
            act(zero_copy(blk * ROWS, ROWS))
            return carry

        lax.fori_loop(nu_ref[0], n_blocks, per_block, 0)

    @pl.when(pl.program_id(0) == 0)
    def _():
        ztile[...] = jnp.zeros(ztile.shape, F32)
        pad_pass(lambda c: c.start())
        tail_pass(lambda c: c.start())

    @pl.when(pl.program_id(0) == pl.num_programs(0) - 1)
    def _():
        pad_pass(lambda c: c.wait())
        tail_pass(lambda c: c.wait())

    n = TL * SUBLANES
    for k in range(TOP_K):
        pltpu.make_async_copy(hn_ref, xb_hbm.at[pl.ds(0, n), :], sem).wait()


def _expert_kernel(be_ref, nu_ref, nv_ref, xs_ref, wgu_hbm, bgu_ref, wd_hbm, bd_ref, o_ref,
                   wgu_f32, wd_f32, wgu_bf, wd_bf, slot_ref, sems):
    b = pl.program_id(0)
    n_used = nu_ref[0]
    e = be_ref[b]
    prev = be_ref[jnp.maximum(b - 1, 0)]
    active = b < n_used

    def weight_copies(expert, slot):
        return (pltpu.make_async_copy(wgu_hbm.at[expert], wgu_f32.at[slot], sems.at[0, slot]),
                pltpu.make_async_copy(wd_hbm.at[expert], wd_f32.at[slot], sems.at[1, slot]))

    @pl.when(active & (b == 0))
    def _():
        slot_ref[0] = 0
        for c in weight_copies(e, 0):
            c.start()

    @pl.when(active & ((b == 0) | (e != prev)))
    def _():
        slot = slot_ref[0]
        for c in weight_copies(e, slot):
            c.wait()
        last = n_used - 1
        nxt = lax.while_loop(lambda j: (j <= last) & (be_ref[jnp.minimum(j, last)] == e), lambda j: j + 1, b + 1)

        @pl.when(nxt <= last)
        def _():
            for c in weight_copies(be_ref[jnp.minimum(nxt, last)], 1 - slot):
                c.start(priority=1)

        wgu_bf[...] = wgu_f32[slot].astype(BF16)
        wd_bf[...] = wd_f32[slot].astype(BF16)
        slot_ref[0] = 1 - slot

    def mlp(n_rows):
        xb = _slab_to_rows(xs_ref, 0, n_rows).astype(BF16)
        gu = jnp.dot(xb, wgu_bf[...], preferred_element_type=F32) + bgu_ref[0]
        gate = jnp.minimum(gu[:, :D], SWIGLU_LIMIT)
        up = jnp.clip(gu[:, D:], -SWIGLU_LIMIT, SWIGLU_LIMIT)
        hidden = (up + 1.0) * gate * jax.nn.sigmoid(SWIGLU_ALPHA * gate)
        out = jnp.dot(hidden.astype(BF16), wd_bf[...], preferred_element_type=F32) + bd_ref[0]
        _rows_to_slab(o_ref, 0, out)

    n_valid = nv_ref[b]
    for n_rows in range(ROW_GRANULE, ROWS + 1, ROW_GRANULE):
        @pl.when(active & (n_valid > n_rows - ROW_GRANULE) & (n_valid <= n_rows))
        def _(n_rows=n_rows):
            mlp(n_rows)
            if n_rows < ROWS:
                o_ref[n_rows * SUBLANES:ROWS * SUBLANES, :] = jnp.zeros(((ROWS - n_rows) * SUBLANES, LANES), F32)

    @pl.when(jnp.logical_not(active))
    def _():
        o_ref[...] = jnp.zeros(o_ref.shape, F32)


def _combine_kernel(n_tiles, n_prompt_tiles, dest_ref, dest_next_ref, h_ref, gw_ref, gfin_ref, ob_hbm,
                    yp_ref, ys_ref, gath0, gath1, sems):
    i = pl.program_id(0)
    n = TOP_K * TL * SUBLANES

    def row_copy(table_ref, gath, sem, tt, k):
        src = pl.multiple_of(table_ref[0, 0, k * TL + tt] * SUBLANES, SUBLANES)
        dst = pl.multiple_of((k * TL + tt) * SUBLANES, SUBLANES)
        return pltpu.make_async_copy(ob_hbm.at[pl.ds(src, SUBLANES), :], gath.at[pl.ds(dst, SUBLANES), :], sem)

    def issue(table_ref, gath, sem):
        def body(q, carry):
            for u in range(ISSUE_UNROLL):
                for k in range(TOP_K):
                    row_copy(table_ref, gath, sem, q * ISSUE_UNROLL + u, k).start(priority=k % 2)
            return carry

        lax.fori_loop(0, TL // ISSUE_UNROLL, body, 0)

    def wait_tile(gath, sem):
        pltpu.make_async_copy(ob_hbm.at[pl.ds(0, n), :], gath, sem).wait()

    def step(cur, cur_sem, nxt, nxt_sem):
        wait_tile(cur, cur_sem)
        for tt in range(TL):
            for k in range(TOP_K):
                row_copy(dest_next_ref, nxt, nxt_sem, tt, k).start(priority=k % 2)
        acc = h_ref[...]
        gw = gw_ref[...]
        for k in range(TOP_K):
            acc = acc + gw[:, k:k + 1] * _slab_to_rows(cur, k * TL, TL)
        y = _rmsnorm(acc, gfin_ref[...])
        is_prompt = i < n_prompt_tiles

        @pl.when(is_prompt)
        def _():
            yp_ref[...] = y

        @pl.when(jnp.logical_not(is_prompt))
        def _():
            ys_ref[...] = y

        @pl.when(i == n_tiles - 1)
        def _():
            wait_tile(nxt, nxt_sem)

    @pl.when(i == 0)
    def _():
        issue(dest_ref, gath0, sems.at[0])

    parity = lax.rem(i, 2)

    @pl.when(parity == 0)
    def _():
        step(gath0, sems.at[0], gath1, sems.at[1])

    @pl.when(parity == 1)
    def _():
        step(gath1, sems.at[1], gath0, sems.at[0])


def _const_spec(shape):
    nd = len(shape)
    return pl.BlockSpec(shape, lambda *_: (0,) * nd, pipeline_mode=pl.Buffered(1))


def kernel(x_prompt, x_sample, state_conv_a, state_conv_b, g_mix, w_in, b_in, w_conv_a, w_out_a, w_conv_b, b_conv_b, ln_g, ln_b, w_out_b, b_out_b, w_o, g_ffn, w_router, b_router, w_gate_up, b_gate_up, w_down, b_down, g_final):
    depth = g_mix.shape[0]
    assert depth == 1
    bp, lp, d = x_prompt.shape
    bs, ls, _ = x_sample.shape
    assert d == D and w_in.shape[2] == 7 * D
    tp, ts = bp * lp, bs * ls
    t = tp + ts
    assert lp % TL == 0 and ls == SEG and ts % TL == 0 and t % ROUTE_CHUNK == 0 and ROUTE_CHUNK % TL == 0
    assert TAPS_B - 1 <= HIST <= SEG and TAPS_A - 1 <= HIST_A
    npt, nst = tp // TL, ts // TL
    nt = npt + nst
    tiles_per_seq = lp // TL
    n_blocks = (t * TOP_K) // ROWS + N_EXPERTS
    n_blocks_pad = -(-n_blocks // LANES) * LANES
    p_rows = n_blocks * ROWS

    row = lambda a: a.reshape(1, -1)
    f32_spec = lambda: _const_spec((1, D))

    mixer = pl.pallas_call(
        functools.partial(_mixer_kernel, npt, tiles_per_seq, bp, bs),
        grid=(nt,),
        in_specs=[
            pl.BlockSpec((TL, D), lambda i: (jnp.minimum(i, npt - 1), 0)),
            pl.BlockSpec((TL, D), lambda i: (jnp.maximum(i - npt, 0), 0)),
            pl.BlockSpec((NSEG, TAPS_A - 1, D), lambda i: (jnp.maximum(i - npt, 0), 0, 0)),
            pl.BlockSpec((NSEG, TAPS_B - 1, D), lambda i: (jnp.maximum(i - npt, 0), 0, 0)),
            f32_spec(),
            _const_spec((D, 7 * D)),
            _const_spec((1, 7 * D)),
            _const_spec((TAPS_A, D)),
            _const_spec((D, D)),
            _const_spec((TAPS_B, SUBLANES, LANES)),
            f32_spec(), f32_spec(), f32_spec(),
            _const_spec((D, D)),
            f32_spec(),
            _const_spec((D, D)),
            f32_spec(),
            _const_spec((N_EXPERTS, D)),
            _const_spec((N_EXPERTS, 1)),
        ],
        out_specs=[
            pl.BlockSpec((TL, D), lambda i: (i, 0)),
            pl.BlockSpec((TL * SUBLANES, LANES), lambda i: (i, 0)),
            pl.BlockSpec((TOP_K, TL), lambda i: (0, i)),
            pl.BlockSpec((TOP_K, TL), lambda i: (0, i)),
            pl.BlockSpec((bp, TAPS_A - 1, D), lambda i: (0, 0, 0)),
            pl.BlockSpec((bp, TAPS_B - 1, D), lambda i: (0, 0, 0)),
            pl.BlockSpec((bs, TAPS_A - 1, D), lambda i: (0, 0, 0)),
            pl.BlockSpec((bs, TAPS_B - 1, D), lambda i: (0, 0, 0)),
        ],
        out_shape=[
            jax.ShapeDtypeStruct((t, D), F32),
            jax.ShapeDtypeStruct((t * SUBLANES, LANES), F32),
            jax.ShapeDtypeStruct((TOP_K, t), I32),
            jax.ShapeDtypeStruct((TOP_K, t), F32),
            jax.ShapeDtypeStruct((bp, TAPS_A - 1, D), F32),
            jax.ShapeDtypeStruct((bp, TAPS_B - 1, D), F32),
            jax.ShapeDtypeStruct((bs, TAPS_A - 1, D), F32),
            jax.ShapeDtypeStruct((bs, TAPS_B - 1, D), F32),
        ],
        scratch_shapes=[
            pltpu.VMEM((NSEG * (HIST_A + SEG), D), F32),
            pltpu.VMEM((HIST_A, D), F32),
            pltpu.VMEM(((HIST + TL + NSEG * HIST) * SUBLANES, LANES), F32),
            pltpu.VMEM((TL * SUBLANES, LANES), F32),
            pltpu.VMEM((bp + bs + 1, TAPS_A - 1, D), F32),
            pltpu.VMEM((bp + bs + 1, TAPS_B - 1, D), F32),
        ],
        compiler_params=pltpu.CompilerParams(dimension_semantics=("arbitrary",), vmem_limit_bytes=VMEM_LIMIT),
        name="mixer",
    )
    h, hn_slab, idx_t, gw_t, na_p, nb_p, na_s, nb_s = mixer(
        x_prompt.reshape(tp, D), x_sample.reshape(ts, D), state_conv_a[0], state_conv_b[0],
        row(g_mix), w_in[0].astype(BF16), row(b_in), w_conv_a[0],
        w_out_a[0].astype(BF16), w_conv_b[0].reshape(TAPS_B, SUBLANES, LANES), row(b_conv_b), row(ln_g), row(ln_b),
        w_out_b[0].astype(BF16), row(b_out_b), w_o[0].astype(BF16), row(g_ffn),
        w_router[0].T.astype(BF16), b_router[0].reshape(N_EXPERTS, 1))

    dest_sm, block_e, n_used, n_valid, pad_start, pad_len = pl.pallas_call(
        functools.partial(_route_kernel, t, n_blocks_pad),
        out_shape=[
            jax.ShapeDtypeStruct((nt, 1, TOP_K * TL), I32),
            jax.ShapeDtypeStruct((1, n_blocks_pad), I32),
            jax.ShapeDtypeStruct((1, LANES), I32),
            jax.ShapeDtypeStruct((1, n_blocks_pad), I32),
            jax.ShapeDtypeStruct((N_EXPERTS, LANES), I32),
            jax.ShapeDtypeStruct((N_EXPERTS, LANES), I32),
        ],
        scratch_shapes=[pltpu.VMEM((ROUTE_CHUNK, ROUTE_CHUNK), BF16)],
        compiler_params=pltpu.CompilerParams(vmem_limit_bytes=VMEM_LIMIT),
        name="route",
    )(idx_t)
    block_e = block_e[0, :n_blocks]
    n_valid = n_valid[0, :n_blocks]
    n_used = n_used[0, :1]
    pad_start, pad_len = pad_start[:, 0], pad_len[:, 0]

    smem_tile_spec = pl.BlockSpec((1, 1, TOP_K * TL), lambda i: (i, 0, 0), memory_space=pltpu.SMEM)
    any_spec = pl.BlockSpec(memory_space=pl.ANY)
    x_buf = pl.pallas_call(
        functools.partial(_dispatch_kernel, n_blocks),
        grid_spec=pltpu.PrefetchScalarGridSpec(
            num_scalar_prefetch=3,
            grid=(nt,),
            in_specs=[
                pl.BlockSpec((1, 1, TOP_K * TL), lambda i, *_: (i, 0, 0), memory_space=pltpu.SMEM),
                pl.BlockSpec((TL * SUBLANES, LANES), lambda i, *_: (i, 0)),
            ],
            out_specs=any_spec,
            scratch_shapes=[pltpu.VMEM((ROWS * SUBLANES, LANES), F32), pltpu.SemaphoreType.DMA,
                            pltpu.SemaphoreType.DMA],
        ),
        out_shape=jax.ShapeDtypeStruct((p_rows * SUBLANES, LANES), F32),
        compiler_params=pltpu.CompilerParams(dimension_semantics=("arbitrary",)),
        name="dispatch",
    )(pad_start, pad_len, n_used, dest_sm, hn_slab)

    def blk(b, be, nu):
        return jnp.maximum(jnp.minimum(b, nu[0] - 1), 0)

    out_buf = pl.pallas_call(
        _expert_kernel,
        grid_spec=pltpu.PrefetchScalarGridSpec(
            num_scalar_prefetch=3,
            grid=(n_blocks,),
            in_specs=[
                pl.BlockSpec((ROWS * SUBLANES, LANES), lambda b, be, nu, nv: (blk(b, be, nu), 0)),
                any_spec,
                pl.BlockSpec((1, 1, 2 * D), lambda b, be, nu, nv: (be[blk(b, be, nu)], 0, 0)),
                any_spec,
                pl.BlockSpec((1, 1, D), lambda b, be, nu, nv: (be[blk(b, be, nu)], 0, 0)),
            ],
            out_specs=pl.BlockSpec((ROWS * SUBLANES, LANES), lambda b, be, nu, nv: (b, 0)),
            scratch_shapes=[
                pltpu.VMEM((2, D, 2 * D), F32), pltpu.VMEM((2, D, D), F32),
                pltpu.VMEM((D, 2 * D), BF16), pltpu.VMEM((D, D), BF16),
                pltpu.SMEM((1,), I32), pltpu.SemaphoreType.DMA((2, 2)),
            ],
        ),
        out_shape=jax.ShapeDtypeStruct((p_rows * SUBLANES, LANES), F32),
        compiler_params=pltpu.CompilerParams(dimension_semantics=("arbitrary",), vmem_limit_bytes=VMEM_LIMIT),
        name="experts",
    )(block_e, n_used, n_valid, x_buf, w_gate_up[0], b_gate_up[0].reshape(N_EXPERTS, 1, 2 * D), w_down[0],
      b_down[0].reshape(N_EXPERTS, 1, D))

    y_p, y_s = pl.pallas_call(
        functools.partial(_combine_kernel, nt, npt),
        grid=(nt,),
        in_specs=[
            smem_tile_spec,
            pl.BlockSpec((1, 1, TOP_K * TL), lambda i: (jnp.minimum(i + 1, nt - 1), 0, 0), memory_space=pltpu.SMEM),
            pl.BlockSpec((TL, D), lambda i: (i, 0)),
            pl.BlockSpec((TL, TOP_K), lambda i: (i, 0)),
            pl.BlockSpec((1, D), lambda i: (0, 0)),
            any_spec,
        ],
        out_specs=[
            pl.BlockSpec((TL, D), lambda i: (jnp.minimum(i, npt - 1), 0)),
            pl.BlockSpec((TL, D), lambda i: (jnp.maximum(i - npt, 0), 0)),
        ],
        out_shape=[jax.ShapeDtypeStruct((tp, D), F32), jax.ShapeDtypeStruct((ts, D), F32)],
        scratch_shapes=[pltpu.VMEM((TOP_K * TL * SUBLANES, LANES), F32), pltpu.VMEM((TOP_K * TL * SUBLANES, LANES), F32),
                        pltpu.SemaphoreType.DMA((2,))],
        compiler_params=pltpu.CompilerParams(dimension_semantics=("arbitrary",), vmem_limit_bytes=VMEM_LIMIT),
        name="combine",
    )(dest_sm, dest_sm, h, gw_t.T, row(g_final), out_buf)

    return (y_p.reshape(bp, lp, D), y_s.reshape(bs, ls, D), na_p[None], nb_p[None], na_s[None], nb_s[None])
```

```python
import functools

import jax
import jax.numpy as jnp
from jax import lax
from jax.experimental import pallas as pl
from jax.experimental.pallas import tpu as pltpu

F32 = jnp.float32
BF16 = jnp.bfloat16
I32 = jnp.int32

D = 1024
LANES = 128
SUBLANES = 8
NCHUNK = D // LANES
N_EXPERTS = 32
TOP_K = 4
TAPS_A = 3
TAPS_B = 31
EPS = 1e-5
SWIGLU_LIMIT = 7.0
SWIGLU_ALPHA = 1.702

TL = 256
SEG = 32
NSEG = TL // SEG
HIST = 32
HIST_A = 8
CONV_BLOCK = 4
ROWS = 512
ROW_GRANULE = 128
ISSUE_UNROLL = 8
ROUTE_CHUNK = 512
VMEM_LIMIT = 56 * 1024 * 1024


def _rows_to_slab(dst_ref, slot0, val):
    n = val.shape[0]
    for c in range(NCHUNK):
        dst_ref[pl.ds(SUBLANES * slot0 + c, n, stride=SUBLANES), :] = val[:, c * LANES:(c + 1) * LANES]


def _slab_to_rows(src_ref, slot0, n):
    return jnp.concatenate(
        [src_ref[pl.ds(SUBLANES * slot0 + c, n, stride=SUBLANES), :] for c in range(NCHUNK)], axis=1)


def _rmsnorm(x, g):
    return x * lax.rsqrt(jnp.mean(x * x, axis=-1, keepdims=True) + EPS) * g


def _conv_b_slab(vbuf, hist_rows, w_ref, dst_ref):
    first_tap = HIST - (TAPS_B - 1)
    for s in range(NSEG):
        cur_row0 = (HIST + s * SEG) * SUBLANES
        for t0 in range(0, SEG, CONV_BLOCK):
            ins = []
            for k in range(CONV_BLOCK + TAPS_B - 1):
                u = t0 + first_tap + k
                if u < HIST:
                    row = pl.multiple_of(hist_rows[s] + u * SUBLANES, SUBLANES)
                    ins.append(vbuf[pl.ds(row, SUBLANES), :])
                else:
                    row = cur_row0 + (u - HIST) * SUBLANES
                    ins.append(vbuf[row:row + SUBLANES, :])
            accs = [None] * CONV_BLOCK
            for j in range(TAPS_B):
                wj = w_ref[j]
                for o in range(CONV_BLOCK):
                    term = wj * ins[o + j]
                    accs[o] = term if accs[o] is None else accs[o] + term
            for o in range(CONV_BLOCK):
                row = (s * SEG + t0 + o) * SUBLANES
                dst_ref[row:row + SUBLANES, :] = accs[o]


def _mixer_kernel(n_prompt_tiles, tiles_per_seq, n_prompt_seq, n_sample_seq,
                  xp_ref, xs_ref, sta_ref, stb_ref, gmix_ref, win_ref, bin_ref, wca_ref, woa_ref,
                  wcb_ref, bcb_ref, lng_ref, lnb_ref, wob_ref, bob_ref, wo_ref, gffn_ref, wrt_ref, brt_ref,
                  h_ref, hn_ref, idx_ref, gw_ref, nap_ref, nbp_ref, nas_ref, nbs_ref,
                  cubuf, carry_a, vbuf, ybslab, sa_scr, sb_scr):
    i = pl.program_id(0)
    n_tiles = pl.num_programs(0)
    is_prompt = i < n_prompt_tiles
    seq_start = is_prompt & (lax.rem(i, tiles_per_seq) == 0)
    carry_rows = HIST * SUBLANES

    @pl.when(i == 0)
    def _():
        vbuf[0:carry_rows, :] = jnp.zeros((carry_rows, LANES), F32)
        carry_a[...] = jnp.zeros(carry_a.shape, F32)

    x = jnp.where(is_prompt, xp_ref[...], xs_ref[...])
    n_bf = _rmsnorm(x, gmix_ref[...]).astype(BF16)

    def proj(g):
        cols = slice(g * D, (g + 1) * D)
        return jnp.dot(n_bf, win_ref[:, cols], preferred_element_type=F32) + bin_ref[:, cols]

    cu = proj(1) * proj(2)
    v = proj(3) * jax.nn.sigmoid(proj(4))

    seg_rows = HIST_A + SEG
    prev_a = jnp.where(seq_start, 0.0, carry_a[HIST_A - (TAPS_A - 1):HIST_A, :])
    for s in range(NSEG):
        r0 = s * SEG
        prev = prev_a if s == 0 else cu[r0 - (TAPS_A - 1):r0, :]
        cubuf[s * seg_rows + HIST_A - (TAPS_A - 1):s * seg_rows + HIST_A, :] = jnp.where(is_prompt, prev, sta_ref[s])
        cubuf[s * seg_rows + HIST_A:(s + 1) * seg_rows, :] = cu[r0:r0 + SEG, :]
    carry_a[HIST_A - (TAPS_A - 1):HIST_A, :] = cu[TL - (TAPS_A - 1):TL, :]
    ya_parts = []
    for s in range(NSEG):
        acc = None
        for j in range(TAPS_A):
            lo = s * seg_rows + HIST_A - (TAPS_A - 1) + j
            term = wca_ref[j:j + 1, :] * cubuf[lo:lo + SEG, :]
            acc = term if acc is None else acc + term
        ya_parts.append(acc)
    ya = jnp.concatenate(ya_parts, axis=0)

    vbuf[0:carry_rows, :] = jnp.where(seq_start, 0.0, vbuf[0:carry_rows, :])
    _rows_to_slab(vbuf, HIST, v)
    state_slot0 = HIST + TL
    for s in range(NSEG):
        for c in range(NCHUNK):
            vbuf[pl.ds(SUBLANES * (state_slot0 + s * HIST + HIST - (TAPS_B - 1)) + c, TAPS_B - 1, stride=SUBLANES), :] = (
                stb_ref[s, :, c * LANES:(c + 1) * LANES])
    hist_rows = [jnp.where(is_prompt, s * SEG * SUBLANES, (state_slot0 + s * HIST) * SUBLANES) for s in range(NSEG)]
    _conv_b_slab(vbuf, hist_rows, wcb_ref, ybslab)
    vbuf[0:carry_rows, :] = vbuf[TL * SUBLANES:(TL + HIST) * SUBLANES, :]

    dummy = n_prompt_seq + n_sample_seq
    for s in range(NSEG):
        prompt_slot = i // tiles_per_seq if s == NSEG - 1 else dummy
        slot = jnp.where(is_prompt, prompt_slot, n_prompt_seq + (i - n_prompt_tiles) * NSEG + s)
        r1 = (s + 1) * SEG
        sa_scr[slot] = cu[r1 - (TAPS_A - 1):r1, :]
        sb_scr[slot] = v[r1 - (TAPS_B - 1):r1, :]

    @pl.when(i == n_tiles - 1)
    def _():
        nap_ref[...] = sa_scr[0:n_prompt_seq]
        nbp_ref[...] = sb_scr[0:n_prompt_seq]
        nas_ref[...] = sa_scr[n_prompt_seq:n_prompt_seq + n_sample_seq]
        nbs_ref[...] = sb_scr[n_prompt_seq:n_prompt_seq + n_sample_seq]

    yb = _slab_to_rows(ybslab, 0, TL) + bcb_ref[...]

    out_a = jnp.dot((proj(0) * ya).astype(BF16), woa_ref[...], preferred_element_type=F32)
    mu = jnp.mean(yb, axis=-1, keepdims=True)
    yc = yb - mu
    var = jnp.mean(yc * yc, axis=-1, keepdims=True)
    ln = yc * lax.rsqrt(var + EPS) * lng_ref[...] + lnb_ref[...]
    act = ln * jax.nn.sigmoid(ln)
    out_b = jnp.dot(act.astype(BF16), wob_ref[...], preferred_element_type=F32) + bob_ref[...]
    mix = jax.nn.sigmoid(proj(5)) * out_a + jax.nn.sigmoid(proj(6)) * out_b
    h = x + jnp.dot(mix.astype(BF16), wo_ref[...], preferred_element_type=F32)
    h_ref[...] = h
    hn = _rmsnorm(h, gffn_ref[...])
    _rows_to_slab(hn_ref, 0, hn)

    logits = lax.dot_general(wrt_ref[...], hn.astype(BF16), (((1,), (1,)), ((), ())),
                             preferred_element_type=F32) + brt_ref[...]
    iota_e = lax.broadcasted_iota(I32, (N_EXPERTS, TL), 0).astype(F32)
    vals, idxs = [], []
    for _ in range(TOP_K):
        m = jnp.max(logits, axis=0, keepdims=True)
        sel = jnp.min(jnp.where(logits == m, iota_e, float(N_EXPERTS)), axis=0, keepdims=True)
        vals.append(m)
        idxs.append(sel)
        logits = jnp.where(iota_e == sel, -jnp.inf, logits)
    ex = [jnp.exp(val - vals[0]) for val in vals]
    denom = ex[0] + ex[1] + ex[2] + ex[3]
    for k in range(TOP_K):
        idx_ref[k:k + 1, :] = idxs[k].astype(I32)
        gw_ref[k:k + 1, :] = ex[k] / denom


def _route_kernel(n_tokens, n_blocks_pad, idx_ref, dest_ref, be_ref, nu_ref, nv_ref, pad_start_ref, pad_len_ref, tri_scr):
    n_chunks = n_tokens // ROUTE_CHUNK
    iota_e = lax.broadcasted_iota(I32, (N_EXPERTS, ROUTE_CHUNK), 0)

    def masks(c):
        sl = pl.ds(pl.multiple_of(c * ROUTE_CHUNK, ROUTE_CHUNK), ROUTE_CHUNK)
        return sl, [idx_ref[k:k + 1, sl] == iota_e for k in range(TOP_K)]

    def onehot_sum(ms):
        tot = ms[0].astype(F32)
        for k in range(1, TOP_K):
            tot = tot + ms[k].astype(F32)
        return tot

    def count_body(c, acc):
        _, ms = masks(c)
        return acc + onehot_sum(ms)

    acc = lax.fori_loop(0, n_chunks, count_body, jnp.zeros((N_EXPERTS, ROUTE_CHUNK), F32))
    counts = jnp.sum(acc, axis=1, keepdims=True)
    nblk = jnp.floor((counts + (ROWS - 1)) * (1.0 / ROWS))
    r = lax.broadcasted_iota(I32, (N_EXPERTS, N_EXPERTS), 0)
    cidx = lax.broadcasted_iota(I32, (N_EXPERTS, N_EXPERTS), 1)
    lower = (cidx <= r).astype(F32)
    pend = jnp.dot(lower, jnp.broadcast_to(nblk, (N_EXPERTS, LANES)),
                   precision=lax.Precision.HIGHEST, preferred_element_type=F32)
    pend1 = pend[:, 0:1]
    pstart_rows = (pend1 - nblk) * ROWS
    pad_start_ref[...] = jnp.broadcast_to(pstart_rows + counts, (N_EXPERTS, LANES)).astype(I32)
    pad_len_ref[...] = jnp.broadcast_to(nblk * ROWS - counts, (N_EXPERTS, LANES)).astype(I32)

    bvec = lax.broadcasted_iota(I32, (N_EXPERTS, n_blocks_pad), 1).astype(F32)
    be = jnp.sum((pend1 <= bvec).astype(F32), axis=0, keepdims=True)
    be = jnp.minimum(be, N_EXPERTS - 1)
    be_ref[...] = be.astype(I32)
    of_block = lax.broadcasted_iota(I32, (N_EXPERTS, n_blocks_pad), 0).astype(F32) == be
    count_b = jnp.sum(jnp.where(of_block, counts, 0.0), axis=0, keepdims=True)
    first_b = jnp.sum(jnp.where(of_block, pend1 - nblk, 0.0), axis=0, keepdims=True)
    nv_ref[...] = jnp.clip(count_b - ROWS * (bvec[0:1, :] - first_b), 0.0, float(ROWS)).astype(I32)
    last = lax.broadcasted_iota(I32, (N_EXPERTS, LANES), 0) == N_EXPERTS - 1
    nu_ref[...] = jnp.sum(jnp.where(last, pend, 0.0), axis=0, keepdims=True).astype(I32)

    tr = lax.broadcasted_iota(I32, (ROUTE_CHUNK, ROUTE_CHUNK), 0)
    tc = lax.broadcasted_iota(I32, (ROUTE_CHUNK, ROUTE_CHUNK), 1)
    tri_scr[...] = (tr < tc).astype(BF16)

    def dest_body(c, carry):
        sl, ms = masks(c)
        tot = onehot_sum(ms)
        before = jnp.dot(tot.astype(BF16), tri_scr[...], preferred_element_type=F32)
        pos = pstart_rows + carry + before
        for k in range(TOP_K):
            dest_k = jnp.sum(jnp.where(ms[k], pos, 0.0), axis=0, keepdims=True).astype(I32)
            for part in range(ROUTE_CHUNK // TL):
                dest_ref[c * (ROUTE_CHUNK // TL) + part, :, k * TL:(k + 1) * TL] = dest_k[:, part * TL:(part + 1) * TL]
        return carry + jnp.sum(tot, axis=1, keepdims=True)

    lax.fori_loop(0, n_chunks, dest_body, jnp.zeros((N_EXPERTS, 1), F32))


def _dispatch_kernel(n_blocks, pad_start_ref, pad_len_ref, nu_ref, dest_ref, hn_ref, xb_hbm, ztile, sem, pad_sem):
    def row_copy(tt, k):
        src = pl.multiple_of(tt * SUBLANES, SUBLANES)
        dst = pl.multiple_of(dest_ref[0, 0, k * TL + tt] * SUBLANES, SUBLANES)
        return pltpu.make_async_copy(hn_ref.at[pl.ds(src, SUBLANES), :], xb_hbm.at[pl.ds(dst, SUBLANES), :], sem)

    def body(q, carry):
        for u in range(ISSUE_UNROLL):
            for k in range(TOP_K):
                row_copy(q * ISSUE_UNROLL + u, k).start(priority=k % 2)
        return carry

    lax.fori_loop(0, TL // ISSUE_UNROLL, body, 0)

    @pl.when(pl.program_id(0) == 0)
    def _():
        ztile[...] = jnp.zeros(ztile.shape, F32)

        def zero_copy(row, n_slabs):
            dst = pl.multiple_of(row * SUBLANES, SUBLANES)
            return pltpu.make_async_copy(ztile.at[pl.ds(0, n_slabs * SUBLANES), :],
                                         xb_hbm.at[pl.ds(dst, n_slabs * SUBLANES), :], pad_sem)

        def pad_pass(act):
            def per_expert(e, carry):
                row, length = pad_start_ref[e, 0], pad_len_ref[e, 0]
                for bit in reversed(range(ROWS.bit_length() - 1)):
                    take = (length & (1 << bit)) != 0

                    @pl.when(take)
                    def _():
                        act(zero_copy(row, 1 << bit))

                    row = row + jnp.where(take, 1 << bit, 0)
                return carry

            lax.fori_loop(0, N_EXPERTS, per_expert, 0)

        def tail_pass(act):
            def per_block(blk, carry):
                act(zero_copy(blk * ROWS, ROWS))
                return carry

            lax.fori_loop(nu_ref[0, 0], n_blocks, per_block, 0)

        for act in (lambda c: c.start(), lambda c: c.wait()):
            pad_pass(act)
            tail_pass(act)

    n = TL * SUBLANES
    for k in range(TOP_K):
        pltpu.make_async_copy(hn_ref, xb_hbm.at[pl.ds(0, n), :], sem).wait()


def _expert_kernel(be_ref, nu_ref, nv_ref, xs_ref, wgu_hbm, bgu_ref, wd_hbm, bd_ref, o_ref,
                   wgu_f32, wd_f32, wgu_bf, wd_bf, slot_ref, sems):
    b = pl.program_id(0)
    n_used = nu_ref[0, 0]
    e = be_ref[0, b]
    prev = be_ref[0, jnp.maximum(b - 1, 0)]
    active = b < n_used

    def weight_copies(expert, slot):
        return (pltpu.make_async_copy(wgu_hbm.at[expert], wgu_f32.at[slot], sems.at[0, slot]),
                pltpu.make_async_copy(wd_hbm.at[expert], wd_f32.at[slot], sems.at[1, slot]))

    @pl.when(active & (b == 0))
    def _():
        slot_ref[0] = 0
        for c in weight_copies(e, 0):
            c.start()

    @pl.when(active & ((b == 0) | (e != prev)))
    def _():
        slot = slot_ref[0]
        for c in weight_copies(e, slot):
            c.wait()
        last = n_used - 1
        nxt = lax.while_loop(lambda j: (j <= last) & (be_ref[0, jnp.minimum(j, last)] == e), lambda j: j + 1, b + 1)

        @pl.when(nxt <= last)
        def _():
            for c in weight_copies(be_ref[0, jnp.minimum(nxt, last)], 1 - slot):
                c.start(priority=1)

        wgu_bf[...] = wgu_f32[slot].astype(BF16)
        wd_bf[...] = wd_f32[slot].astype(BF16)
        slot_ref[0] = 1 - slot

    def mlp(n_rows):
        xb = _slab_to_rows(xs_ref, 0, n_rows).astype(BF16)
        gu = jnp.dot(xb, wgu_bf[...], preferred_element_type=F32) + bgu_ref[0]
        gate = jnp.minimum(gu[:, :D], SWIGLU_LIMIT)
        up = jnp.clip(gu[:, D:], -SWIGLU_LIMIT, SWIGLU_LIMIT)
        hidden = (up + 1.0) * gate * jax.nn.sigmoid(SWIGLU_ALPHA * gate)
        out = jnp.dot(hidden.astype(BF16), wd_bf[...], preferred_element_type=F32) + bd_ref[0]
        _rows_to_slab(o_ref, 0, out)

    n_valid = nv_ref[0, b]
    for n_rows in range(ROW_GRANULE, ROWS + 1, ROW_GRANULE):
        @pl.when(active & (n_valid > n_rows - ROW_GRANULE) & (n_valid <= n_rows))
        def _(n_rows=n_rows):
            mlp(n_rows)
            if n_rows < ROWS:
                o_ref[n_rows * SUBLANES:ROWS * SUBLANES, :] = jnp.zeros(((ROWS - n_rows) * SUBLANES, LANES), F32)

    @pl.when(jnp.logical_not(active))
    def _():
        o_ref[...] = jnp.zeros(o_ref.shape, F32)


def _combine_kernel(n_tiles, n_prompt_tiles, dest_ref, dest_next_ref, h_ref, gw_ref, gfin_ref, ob_hbm,
                    yp_ref, ys_ref, gath0, gath1, sems):
    i = pl.program_id(0)
    n = TOP_K * TL * SUBLANES

    def row_copy(table_ref, gath, sem, tt, k):
        src = pl.multiple_of(table_ref[0, 0, k * TL + tt] * SUBLANES, SUBLANES)
        dst = pl.multiple_of((k * TL + tt) * SUBLANES, SUBLANES)
        return pltpu.make_async_copy(ob_hbm.at[pl.ds(src, SUBLANES), :], gath.at[pl.ds(dst, SUBLANES), :], sem)

    def issue(table_ref, gath, sem):
        def body(q, carry):
            for u in range(ISSUE_UNROLL):
                for k in range(TOP_K):
                    row_copy(table_ref, gath, sem, q * ISSUE_UNROLL + u, k).start(priority=k % 2)
            return carry

        lax.fori_loop(0, TL // ISSUE_UNROLL, body, 0)

    def wait_tile(gath, sem):
        pltpu.make_async_copy(ob_hbm.at[pl.ds(0, n), :], gath, sem).wait()

    def step(cur, cur_sem, nxt, nxt_sem):
        wait_tile(cur, cur_sem)
        for tt in range(TL):
            for k in range(TOP_K):
                row_copy(dest_next_ref, nxt, nxt_sem, tt, k).start(priority=k % 2)
        acc = h_ref[...]
        gw = gw_ref[...]
        for k in range(TOP_K):
            acc = acc + gw[:, k:k + 1] * _slab_to_rows(cur, k * TL, TL)
        y = _rmsnorm(acc, gfin_ref[...])
        is_prompt = i < n_prompt_tiles

        @pl.when(is_prompt)
        def _():
            yp_ref[...] = y

        @pl.when(jnp.logical_not(is_prompt))
        def _():
            ys_ref[...] = y

        @pl.when(i == n_tiles - 1)
        def _():
            wait_tile(nxt, nxt_sem)

    @pl.when(i == 0)
    def _():
        issue(dest_ref, gath0, sems.at[0])

    parity = lax.rem(i, 2)

    @pl.when(parity == 0)
    def _():
        step(gath0, sems.at[0], gath1, sems.at[1])

    @pl.when(parity == 1)
    def _():
        step(gath1, sems.at[1], gath0, sems.at[0])


def _const_spec(shape):
    nd = len(shape)
    return pl.BlockSpec(shape, lambda *_: (0,) * nd, pipeline_mode=pl.Buffered(1))


def kernel(x_prompt, x_sample, state_conv_a, state_conv_b, g_mix, w_in, b_in, w_conv_a, w_out_a, w_conv_b, b_conv_b, ln_g, ln_b, w_out_b, b_out_b, w_o, g_ffn, w_router, b_router, w_gate_up, b_gate_up, w_down, b_down, g_final):
    depth = g_mix.shape[0]
    assert depth == 1
    bp, lp, d = x_prompt.shape
    bs, ls, _ = x_sample.shape
    assert d == D and w_in.shape[2] == 7 * D
    tp, ts = bp * lp, bs * ls
    t = tp + ts
    assert lp % TL == 0 and ls == SEG and ts % TL == 0 and t % ROUTE_CHUNK == 0 and ROUTE_CHUNK % TL == 0
    assert TAPS_B - 1 <= HIST <= SEG and TAPS_A - 1 <= HIST_A
    npt, nst = tp // TL, ts // TL
    nt = npt + nst
    tiles_per_seq = lp // TL
    n_blocks = (t * TOP_K) // ROWS + N_EXPERTS
    n_blocks_pad = -(-n_blocks // LANES) * LANES
    p_rows = n_blocks * ROWS

    row = lambda a: a.reshape(1, -1)
    f32_spec = lambda: _const_spec((1, D))

    mixer = pl.pallas_call(
        functools.partial(_mixer_kernel, npt, tiles_per_seq, bp, bs),
        grid=(nt,),
        in_specs=[
            pl.BlockSpec((TL, D), lambda i: (jnp.minimum(i, npt - 1), 0)),
            pl.BlockSpec((TL, D), lambda i: (jnp.maximum(i - npt, 0), 0)),
            pl.BlockSpec((NSEG, TAPS_A - 1, D), lambda i: (jnp.maximum(i - npt, 0), 0, 0)),
            pl.BlockSpec((NSEG, TAPS_B - 1, D), lambda i: (jnp.maximum(i - npt, 0), 0, 0)),
            f32_spec(),
            _const_spec((D, 7 * D)),
            _const_spec((1, 7 * D)),
            _const_spec((TAPS_A, D)),
            _const_spec((D, D)),
            _const_spec((TAPS_B, SUBLANES, LANES)),
            f32_spec(), f32_spec(), f32_spec(),
            _const_spec((D, D)),
            f32_spec(),
            _const_spec((D, D)),
            f32_spec(),
            _const_spec((N_EXPERTS, D)),
            _const_spec((N_EXPERTS, 1)),
        ],
        out_specs=[
            pl.BlockSpec((TL, D), lambda i: (i, 0)),
            pl.BlockSpec((TL * SUBLANES, LANES), lambda i: (i, 0)),
            pl.BlockSpec((TOP_K, TL), lambda i: (0, i)),
            pl.BlockSpec((TOP_K, TL), lambda i: (0, i)),
            pl.BlockSpec((bp, TAPS_A - 1, D), lambda i: (0, 0, 0)),
            pl.BlockSpec((bp, TAPS_B - 1, D), lambda i: (0, 0, 0)),
            pl.BlockSpec((bs, TAPS_A - 1, D), lambda i: (0, 0, 0)),
            pl.BlockSpec((bs, TAPS_B - 1, D), lambda i: (0, 0, 0)),
        ],
        out_shape=[
            jax.ShapeDtypeStruct((t, D), F32),
            jax.ShapeDtypeStruct((t * SUBLANES, LANES), F32),
            jax.ShapeDtypeStruct((TOP_K, t), I32),
            jax.ShapeDtypeStruct((TOP_K, t), F32),
            jax.ShapeDtypeStruct((bp, TAPS_A - 1, D), F32),
            jax.ShapeDtypeStruct((bp, TAPS_B - 1, D), F32),
            jax.ShapeDtypeStruct((bs, TAPS_A - 1, D), F32),
            jax.ShapeDtypeStruct((bs, TAPS_B - 1, D), F32),
        ],
        scratch_shapes=[
            pltpu.VMEM((NSEG * (HIST_A + SEG), D), F32),
            pltpu.VMEM((HIST_A, D), F32),
            pltpu.VMEM(((HIST + TL + NSEG * HIST) * SUBLANES, LANES), F32),
            pltpu.VMEM((TL * SUBLANES, LANES), F32),
            pltpu.VMEM((bp + bs + 1, TAPS_A - 1, D), F32),
            pltpu.VMEM((bp + bs + 1, TAPS_B - 1, D), F32),
        ],
        compiler_params=pltpu.CompilerParams(dimension_semantics=("arbitrary",), vmem_limit_bytes=VMEM_LIMIT),
        name="mixer",
    )
    h, hn_slab, idx_t, gw_t, na_p, nb_p, na_s, nb_s = mixer(
        x_prompt.reshape(tp, D), x_sample.reshape(ts, D), state_conv_a[0], state_conv_b[0],
        row(g_mix), w_in[0].astype(BF16), row(b_in), w_conv_a[0],
        w_out_a[0].astype(BF16), w_conv_b[0].reshape(TAPS_B, SUBLANES, LANES), row(b_conv_b), row(ln_g), row(ln_b),
        w_out_b[0].astype(BF16), row(b_out_b), w_o[0].astype(BF16), row(g_ffn),
        w_router[0].T.astype(BF16), b_router[0].reshape(N_EXPERTS, 1))

    dest_sm, block_e, n_used, n_valid, pad_start, pad_len = pl.pallas_call(
        functools.partial(_route_kernel, t, n_blocks_pad),
        out_shape=[
            jax.ShapeDtypeStruct((nt, 1, TOP_K * TL), I32),
            jax.ShapeDtypeStruct((1, n_blocks_pad), I32),
            jax.ShapeDtypeStruct((1, LANES), I32),
            jax.ShapeDtypeStruct((1, n_blocks_pad), I32),
            jax.ShapeDtypeStruct((N_EXPERTS, LANES), I32),
            jax.ShapeDtypeStruct((N_EXPERTS, LANES), I32),
        ],
        scratch_shapes=[pltpu.VMEM((ROUTE_CHUNK, ROUTE_CHUNK), BF16)],
        compiler_params=pltpu.CompilerParams(vmem_limit_bytes=VMEM_LIMIT),
        name="route",
    )(idx_t)

    smem_tile_spec = pl.BlockSpec((1, 1, TOP_K * TL), lambda i: (i, 0, 0), memory_space=pltpu.SMEM)
    any_spec = pl.BlockSpec(memory_space=pl.ANY)
    x_buf = pl.pallas_call(
        functools.partial(_dispatch_kernel, n_blocks),
        grid_spec=pltpu.PrefetchScalarGridSpec(
            num_scalar_prefetch=3,
            grid=(nt,),
            in_specs=[
                pl.BlockSpec((1, 1, TOP_K * TL), lambda i, *_: (i, 0, 0), memory_space=pltpu.SMEM),
                pl.BlockSpec((TL * SUBLANES, LANES), lambda i, *_: (i, 0)),
            ],
            out_specs=any_spec,
            scratch_shapes=[pltpu.VMEM((ROWS * SUBLANES, LANES), F32), pltpu.SemaphoreType.DMA,
                            pltpu.SemaphoreType.DMA],
        ),
        out_shape=jax.ShapeDtypeStruct((p_rows * SUBLANES, LANES), F32),
        compiler_params=pltpu.CompilerParams(dimension_semantics=("arbitrary",)),
        name="dispatch",
    )(pad_start, pad_len, n_used, dest_sm, hn_slab)

    def blk(b, be, nu):
        return jnp.maximum(jnp.minimum(b, nu[0, 0] - 1), 0)

    out_buf = pl.pallas_call(
        _expert_kernel,
        grid_spec=pltpu.PrefetchScalarGridSpec(
            num_scalar_prefetch=3,
            grid=(n_blocks,),
            in_specs=[
                pl.BlockSpec((ROWS * SUBLANES, LANES), lambda b, be, nu, nv: (blk(b, be, nu), 0)),
                any_spec,
                pl.BlockSpec((1, 1, 2 * D), lambda b, be, nu, nv: (be[0, blk(b, be, nu)], 0, 0)),
                any_spec,
                pl.BlockSpec((1, 1, D), lambda b, be, nu, nv: (be[0, blk(b, be, nu)], 0, 0)),
            ],
            out_specs=pl.BlockSpec((ROWS * SUBLANES, LANES), lambda b, be, nu, nv: (b, 0)),
            scratch_shapes=[
                pltpu.VMEM((2, D, 2 * D), F32), pltpu.VMEM((2, D, D), F32),
                pltpu.VMEM((D, 2 * D), BF16), pltpu.VMEM((D, D), BF16),
                pltpu.SMEM((1,), I32), pltpu.SemaphoreType.DMA((2, 2)),
            ],
        ),
        out_shape=jax.ShapeDtypeStruct((p_rows * SUBLANES, LANES), F32),
        compiler_params=pltpu.CompilerParams(dimension_semantics=("arbitrary",), vmem_limit_bytes=VMEM_LIMIT),
        name="experts",
    )(block_e, n_used, n_valid, x_buf, w_gate_up[0], b_gate_up[0].reshape(N_EXPERTS, 1, 2 * D), w_down[0],
      b_down[0].reshape(N_EXPERTS, 1, D))

    y_p, y_s = pl.pallas_call(
        functools.partial(_combine_kernel, nt, npt),
        grid=(nt,),
        in_specs=[
            smem_tile_spec,
            pl.BlockSpec((1, 1, TOP_K * TL), lambda i: (jnp.minimum(i + 1, nt - 1), 0, 0), memory_space=pltpu.SMEM),
            pl.BlockSpec((TL, D), lambda i: (i, 0)),
            pl.BlockSpec((TL, TOP_K), lambda i: (i, 0)),
            pl.BlockSpec((1, D), lambda i: (0, 0)),
            any_spec,
        ],
        out_specs=[
            pl.BlockSpec((TL, D), lambda i: (jnp.minimum(i, npt - 1), 0)),
            pl.BlockSpec((TL, D), lambda i: (jnp.maximum(i - npt, 0), 0)),
        ],
        out_shape=[jax.ShapeDtypeStruct((tp, D), F32), jax.ShapeDtypeStruct((ts, D), F32)],
        scratch_shapes=[pltpu.VMEM((TOP_K * TL * SUBLANES, LANES), F32), pltpu.VMEM((TOP_K * TL * SUBLANES, LANES), F32),
                        pltpu.SemaphoreType.DMA((2,))],
        compiler_params=pltpu.CompilerParams(dimension_semantics=("arbitrary",), vmem_limit_bytes=VMEM_LIMIT),
        name="combine",
    )(dest_sm, dest_sm, h, gw_t.T, row(g_final), out_buf)

    return (y_p.reshape(bp, lp, D), y_s.reshape(bs, ls, D), na_p[None], nb_p[None], na_s[None], nb_s[None])
```

```python
import functools

import jax
import jax.numpy as jnp
from jax import lax
from jax.experimental import pallas as pl
from jax.experimental.pallas import tpu as pltpu

F32 = jnp.float32
BF16 = jnp.bfloat16
I32 = jnp.int32

D = 1024
LANES = 128
SUBLANES = 8
NCHUNK = D // LANES
N_EXPERTS = 32
TOP_K = 4
TAPS_A = 3
TAPS_B = 31
EPS = 1e-5
SWIGLU_LIMIT = 7.0
SWIGLU_ALPHA = 1.702

TL = 256
SEG = 32
NSEG = TL // SEG
HIST = 32
HIST_A = 8
CONV_BLOCK = 4
ROWS = 512
ROW_GRANULE = 128
ISSUE_UNROLL = 8
ROUTE_CHUNK = 512
VMEM_LIMIT = 56 * 1024 * 1024


def _rows_to_slab(dst_ref, slot0, val):
    n = val.shape[0]
    for c in range(NCHUNK):
        dst_ref[pl.ds(SUBLANES * slot0 + c, n, stride=SUBLANES), :] = val[:, c * LANES:(c + 1) * LANES]


def _slab_to_rows(src_ref, slot0, n):
    return jnp.concatenate(
        [src_ref[pl.ds(SUBLANES * slot0 + c, n, stride=SUBLANES), :] for c in range(NCHUNK)], axis=1)


def _rmsnorm(x, g):
    return x * lax.rsqrt(jnp.mean(x * x, axis=-1, keepdims=True) + EPS) * g


def _conv_b_slab(vbuf, hist_rows, w_ref, dst_ref):
    first_tap = HIST - (TAPS_B - 1)
    for s in range(NSEG):
        cur_row0 = (HIST + s * SEG) * SUBLANES
        for t0 in range(0, SEG, CONV_BLOCK):
            ins = []
            for k in range(CONV_BLOCK + TAPS_B - 1):
                u = t0 + first_tap + k
                if u < HIST:
                    row = pl.multiple_of(hist_rows[s] + u * SUBLANES, SUBLANES)
                    ins.append(vbuf[pl.ds(row, SUBLANES), :])
                else:
                    row = cur_row0 + (u - HIST) * SUBLANES
                    ins.append(vbuf[row:row + SUBLANES, :])
            accs = [None] * CONV_BLOCK
            for j in range(TAPS_B):
                wj = w_ref[j]
                for o in range(CONV_BLOCK):
                    term = wj * ins[o + j]
                    accs[o] = term if accs[o] is None else accs[o] + term
            for o in range(CONV_BLOCK):
                row = (s * SEG + t0 + o) * SUBLANES
                dst_ref[row:row + SUBLANES, :] = accs[o]


def _mixer_kernel(n_prompt_tiles, tiles_per_seq, n_prompt_seq, n_sample_seq,
                  xp_ref, xs_ref, sta_ref, stb_ref, gmix_ref, win_ref, bin_ref, wca_ref, woa_ref,
                  wcb_ref, bcb_ref, lng_ref, lnb_ref, wob_ref, bob_ref, wo_ref, gffn_ref, wrt_ref, brt_ref,
                  h_ref, hn_ref, idx_ref, gw_ref, nap_ref, nbp_ref, nas_ref, nbs_ref,
                  cubuf, carry_a, vbuf, ybslab, sa_scr, sb_scr):
    i = pl.program_id(0)
    n_tiles = pl.num_programs(0)
    is_prompt = i < n_prompt_tiles
    seq_start = is_prompt & (lax.rem(i, tiles_per_seq) == 0)
    carry_rows = HIST * SUBLANES

    @pl.when(i == 0)
    def _():
        vbuf[0:carry_rows, :] = jnp.zeros((carry_rows, LANES), F32)
        carry_a[...] = jnp.zeros(carry_a.shape, F32)

    x = jnp.where(is_prompt, xp_ref[...], xs_ref[...])
    n_bf = _rmsnorm(x, gmix_ref[...]).astype(BF16)

    def proj(g):
        cols = slice(g * D, (g + 1) * D)
        return jnp.dot(n_bf, win_ref[:, cols], preferred_element_type=F32) + bin_ref[:, cols]

    cu = proj(1) * proj(2)
    v = proj(3) * jax.nn.sigmoid(proj(4))

    seg_rows = HIST_A + SEG
    prev_a = jnp.where(seq_start, 0.0, carry_a[HIST_A - (TAPS_A - 1):HIST_A, :])
    for s in range(NSEG):
        r0 = s * SEG
        prev = prev_a if s == 0 else cu[r0 - (TAPS_A - 1):r0, :]
        cubuf[s * seg_rows + HIST_A - (TAPS_A - 1):s * seg_rows + HIST_A, :] = jnp.where(is_prompt, prev, sta_ref[s])
        cubuf[s * seg_rows + HIST_A:(s + 1) * seg_rows, :] = cu[r0:r0 + SEG, :]
    carry_a[HIST_A - (TAPS_A - 1):HIST_A, :] = cu[TL - (TAPS_A - 1):TL, :]
    ya_parts = []
    for s in range(NSEG):
        acc = None
        for j in range(TAPS_A):
            lo = s * seg_rows + HIST_A - (TAPS_A - 1) + j
            term = wca_ref[j:j + 1, :] * cubuf[lo:lo + SEG, :]
            acc = term if acc is None else acc + term
        ya_parts.append(acc)
    ya = jnp.concatenate(ya_parts, axis=0)

    vbuf[0:carry_rows, :] = jnp.where(seq_start, 0.0, vbuf[0:carry_rows, :])
    _rows_to_slab(vbuf, HIST, v)
    state_slot0 = HIST + TL
    for s in range(NSEG):
        for c in range(NCHUNK):
            vbuf[pl.ds(SUBLANES * (state_slot0 + s * HIST + HIST - (TAPS_B - 1)) + c, TAPS_B - 1, stride=SUBLANES), :] = (
                stb_ref[s, :, c * LANES:(c + 1) * LANES])
    hist_rows = [jnp.where(is_prompt, s * SEG * SUBLANES, (state_slot0 + s * HIST) * SUBLANES) for s in range(NSEG)]
    _conv_b_slab(vbuf, hist_rows, wcb_ref, ybslab)
    vbuf[0:carry_rows, :] = vbuf[TL * SUBLANES:(TL + HIST) * SUBLANES, :]

    dummy = n_prompt_seq + n_sample_seq
    for s in range(NSEG):
        prompt_slot = i // tiles_per_seq if s == NSEG - 1 else dummy
        slot = jnp.where(is_prompt, prompt_slot, n_prompt_seq + (i - n_prompt_tiles) * NSEG + s)
        r1 = (s + 1) * SEG
        sa_scr[slot] = cu[r1 - (TAPS_A - 1):r1, :]
        sb_scr[slot] = v[r1 - (TAPS_B - 1):r1, :]

    @pl.when(i == n_tiles - 1)
    def _():
        nap_ref[...] = sa_scr[0:n_prompt_seq]
        nbp_ref[...] = sb_scr[0:n_prompt_seq]
        nas_ref[...] = sa_scr[n_prompt_seq:n_prompt_seq + n_sample_seq]
        nbs_ref[...] = sb_scr[n_prompt_seq:n_prompt_seq + n_sample_seq]

    yb = _slab_to_rows(ybslab, 0, TL) + bcb_ref[...]

    out_a = jnp.dot((proj(0) * ya).astype(BF16), woa_ref[...], preferred_element_type=F32)
    mu = jnp.mean(yb, axis=-1, keepdims=True)
    yc = yb - mu
    var = jnp.mean(yc * yc, axis=-1, keepdims=True)
    ln = yc * lax.rsqrt(var + EPS) * lng_ref[...] + lnb_ref[...]
    act = ln * jax.nn.sigmoid(ln)
    out_b = jnp.dot(act.astype(BF16), wob_ref[...], preferred_element_type=F32) + bob_ref[...]
    mix = jax.nn.sigmoid(proj(5)) * out_a + jax.nn.sigmoid(proj(6)) * out_b
    h = x + jnp.dot(mix.astype(BF16), wo_ref[...], preferred_element_type=F32)
    h_ref[...] = h
    hn = _rmsnorm(h, gffn_ref[...])
    _rows_to_slab(hn_ref, 0, hn)

    logits = lax.dot_general(wrt_ref[...], hn.astype(BF16), (((1,), (1,)), ((), ())),
                             preferred_element_type=F32) + brt_ref[...]
    iota_e = lax.broadcasted_iota(I32, (N_EXPERTS, TL), 0).astype(F32)
    vals, idxs = [], []
    for _ in range(TOP_K):
        m = jnp.max(logits, axis=0, keepdims=True)
        sel = jnp.min(jnp.where(logits == m, iota_e, float(N_EXPERTS)), axis=0, keepdims=True)
        vals.append(m)
        idxs.append(sel)
        logits = jnp.where(iota_e == sel, -jnp.inf, logits)
    ex = [jnp.exp(val - vals[0]) for val in vals]
    denom = ex[0] + ex[1] + ex[2] + ex[3]
    for k in range(TOP_K):
        idx_ref[k:k + 1, :] = idxs[k].astype(I32)
        gw_ref[k:k + 1, :] = ex[k] / denom


def _route_kernel(n_tokens, n_blocks_pad, idx_ref, dest_ref, be_ref, nu_ref, nv_ref, pad_start_ref, pad_len_ref, tri_scr):
    n_chunks = n_tokens // ROUTE_CHUNK
    iota_e = lax.broadcasted_iota(I32, (N_EXPERTS, ROUTE_CHUNK), 0)

    def masks(c):
        sl = pl.ds(pl.multiple_of(c * ROUTE_CHUNK, ROUTE_CHUNK), ROUTE_CHUNK)
        return sl, [idx_ref[k:k + 1, sl] == iota_e for k in range(TOP_K)]

    def onehot_sum(ms):
        tot = ms[0].astype(F32)
        for k in range(1, TOP_K):
            tot = tot + ms[k].astype(F32)
        return tot

    def count_body(c, acc):
        _, ms = masks(c)
        return acc + onehot_sum(ms)

    acc = lax.fori_loop(0, n_chunks, count_body, jnp.zeros((N_EXPERTS, ROUTE_CHUNK), F32))
    counts = jnp.sum(acc, axis=1, keepdims=True)
    nblk = jnp.floor((counts + (ROWS - 1)) * (1.0 / ROWS))
    r = lax.broadcasted_iota(I32, (N_EXPERTS, N_EXPERTS), 0)
    cidx = lax.broadcasted_iota(I32, (N_EXPERTS, N_EXPERTS), 1)
    lower = (cidx <= r).astype(F32)
    pend = jnp.dot(lower, jnp.broadcast_to(nblk, (N_EXPERTS, LANES)),
                   precision=lax.Precision.HIGHEST, preferred_element_type=F32)
    pend1 = pend[:, 0:1]
    pstart_rows = (pend1 - nblk) * ROWS
    pad_start_ref[...] = jnp.broadcast_to(pstart_rows + counts, (N_EXPERTS, LANES)).astype(I32)
    pad_len_ref[...] = jnp.broadcast_to(nblk * ROWS - counts, (N_EXPERTS, LANES)).astype(I32)

    bvec = lax.broadcasted_iota(I32, (N_EXPERTS, n_blocks_pad), 1).astype(F32)
    be = jnp.sum((pend1 <= bvec).astype(F32), axis=0, keepdims=True)
    be = jnp.minimum(be, N_EXPERTS - 1)
    be_ref[...] = be.astype(I32)
    of_block = lax.broadcasted_iota(I32, (N_EXPERTS, n_blocks_pad), 0).astype(F32) == be
    count_b = jnp.sum(jnp.where(of_block, counts, 0.0), axis=0, keepdims=True)
    first_b = jnp.sum(jnp.where(of_block, pend1 - nblk, 0.0), axis=0, keepdims=True)
    nv_ref[...] = jnp.clip(count_b - ROWS * (bvec[0:1, :] - first_b), 0.0, float(ROWS)).astype(I32)
    last = lax.broadcasted_iota(I32, (N_EXPERTS, LANES), 0) == N_EXPERTS - 1
    nu_ref[...] = jnp.sum(jnp.where(last, pend, 0.0), axis=0, keepdims=True).astype(I32)

    tr = lax.broadcasted_iota(I32, (ROUTE_CHUNK, ROUTE_CHUNK), 0)
    tc = lax.broadcasted_iota(I32, (ROUTE_CHUNK, ROUTE_CHUNK), 1)
    tri_scr[...] = (tr < tc).astype(BF16)

    def dest_body(c, carry):
        sl, ms = masks(c)
        tot = onehot_sum(ms)
        before = jnp.dot(tot.astype(BF16), tri_scr[...], preferred_element_type=F32)
        pos = pstart_rows + carry + before
        for k in range(TOP_K):
            dest_k = jnp.sum(jnp.where(ms[k], pos, 0.0), axis=0, keepdims=True).astype(I32)
            for part in range(ROUTE_CHUNK // TL):
                dest_ref[c * (ROUTE_CHUNK // TL) + part, :, k * TL:(k + 1) * TL] = dest_k[:, part * TL:(part + 1) * TL]
        return carry + jnp.sum(tot, axis=1, keepdims=True)

    lax.fori_loop(0, n_chunks, dest_body, jnp.zeros((N_EXPERTS, 1), F32))


def _dispatch_kernel(n_blocks, pad_start_ref, pad_len_ref, nu_ref, dest_ref, hn_ref, xb_hbm, ztile, sem, pad_sem):
    def row_copy(tt, k):
        src = pl.multiple_of(tt * SUBLANES, SUBLANES)
        dst = pl.multiple_of(dest_ref[0, 0, k * TL + tt] * SUBLANES, SUBLANES)
        return pltpu.make_async_copy(hn_ref.at[pl.ds(src, SUBLANES), :], xb_hbm.at[pl.ds(dst, SUBLANES), :], sem)

    def body(q, carry):
        for u in range(ISSUE_UNROLL):
            for k in range(TOP_K):
                row_copy(q * ISSUE_UNROLL + u, k).start(priority=k % 2)
        return carry

    lax.fori_loop(0, TL // ISSUE_UNROLL, body, 0)

    @pl.when(pl.program_id(0) == 0)
    def _():
        ztile[...] = jnp.zeros(ztile.shape, F32)

        def zero_copy(row, n_slabs):
            dst = pl.multiple_of(row * SUBLANES, SUBLANES)
            return pltpu.make_async_copy(ztile.at[pl.ds(0, n_slabs * SUBLANES), :],
                                         xb_hbm.at[pl.ds(dst, n_slabs * SUBLANES), :], pad_sem)

        def pad_pass(act):
            def per_expert(e, carry):
                row, length = pad_start_ref[e, 0], pad_len_ref[e, 0]
                for bit in reversed(range(ROWS.bit_length() - 1)):
                    take = (length & (1 << bit)) != 0

                    @pl.when(take)
                    def _():
                        act(zero_copy(row, 1 << bit))

                    row = row + jnp.where(take, 1 << bit, 0)
                return carry

            lax.fori_loop(0, N_EXPERTS, per_expert, 0)

        def tail_pass(act):
            def per_block(blk, carry):
                act(zero_copy(blk * ROWS, ROWS))
                return carry

            lax.fori_loop(nu_ref[0, 0], n_blocks, per_block, 0)

        for act in (lambda c: c.start(), lambda c: c.wait()):
            pad_pass(act)
            tail_pass(act)

    n = TL * SUBLANES
    for k in range(TOP_K):
        pltpu.make_async_copy(hn_ref, xb_hbm.at[pl.ds(0, n), :], sem).wait()


def _expert_kernel(be_ref, nu_ref, nv_ref, xs_ref, wgu_hbm, bgu_ref, wd_hbm, bd_ref, o_ref,
                   wgu_f32, wd_f32, wgu_bf, wd_bf, slot_ref, sems):
    b = pl.program_id(0)
    n_used = nu_ref[0, 0]
    e = be_ref[0, b]
    prev = be_ref[0, jnp.maximum(b - 1, 0)]
    active = b < n_used

    def weight_copies(expert, slot):
        return (pltpu.make_async_copy(wgu_hbm.at[expert], wgu_f32.at[slot], sems.at[0, slot]),
                pltpu.make_async_copy(wd_hbm.at[expert], wd_f32.at[slot], sems.at[1, slot]))

    @pl.when(active & (b == 0))
    def _():
        slot_ref[0] = 0
        for c in weight_copies(e, 0):
            c.start()

    @pl.when(active & ((b == 0) | (e != prev)))
    def _():
        slot = slot_ref[0]
        for c in weight_copies(e, slot):
            c.wait()
        last = n_used - 1
        nxt = lax.while_loop(lambda j: (j <= last) & (be_ref[0, jnp.minimum(j, last)] == e), lambda j: j + 1, b + 1)

        @pl.when(nxt <= last)
        def _():
            for c in weight_copies(be_ref[0, jnp.minimum(nxt, last)], 1 - slot):
                c.start(priority=1)

        wgu_bf[...] = wgu_f32[slot].astype(BF16)
        wd_bf[...] = wd_f32[slot].astype(BF16)
        slot_ref[0] = 1 - slot

    def mlp(n_rows):
        xb = _slab_to_rows(xs_ref, 0, n_rows).astype(BF16)
        gu = jnp.dot(xb, wgu_bf[...], preferred_element_type=F32) + bgu_ref[0]
        gate = jnp.minimum(gu[:, :D], SWIGLU_LIMIT)
        up = jnp.clip(gu[:, D:], -SWIGLU_LIMIT, SWIGLU_LIMIT)
        hidden = (up + 1.0) * gate * jax.nn.sigmoid(SWIGLU_ALPHA * gate)
        out = jnp.dot(hidden.astype(BF16), wd_bf[...], preferred_element_type=F32) + bd_ref[0]
        _rows_to_slab(o_ref, 0, out)

    n_valid = nv_ref[0, b]
    for n_rows in range(ROW_GRANULE, ROWS + 1, ROW_GRANULE):
        @pl.when(active & (n_valid > n_rows - ROW_GRANULE) & (n_valid <= n_rows))
        def _(n_rows=n_rows):
            mlp(n_rows)
            if n_rows < ROWS:
                o_ref[n_rows * SUBLANES:ROWS * SUBLANES, :] = jnp.zeros(((ROWS - n_rows) * SUBLANES, LANES), F32)

    @pl.when(jnp.logical_not(active))
    def _():
        o_ref[...] = jnp.zeros(o_ref.shape, F32)


def _combine_kernel(n_tiles, n_prompt_tiles, dest_ref, dest_next_ref, h_ref, gw_ref, gfin_ref, ob_hbm,
                    yp_ref, ys_ref, gath0, gath1, sems):
    i = pl.program_id(0)
    n = TOP_K * TL * SUBLANES

    def row_copy(table_ref, gath, sem, tt, k):
        src = pl.multiple_of(table_ref[0, 0, k * TL + tt] * SUBLANES, SUBLANES)
        dst = pl.multiple_of((k * TL + tt) * SUBLANES, SUBLANES)
        return pltpu.make_async_copy(ob_hbm.at[pl.ds(src, SUBLANES), :], gath.at[pl.ds(dst, SUBLANES), :], sem)

    def issue(table_ref, gath, sem):
        def body(q, carry):
            for u in range(ISSUE_UNROLL):
                for k in range(TOP_K):
                    row_copy(table_ref, gath, sem, q * ISSUE_UNROLL + u, k).start(priority=k % 2)
            return carry

        lax.fori_loop(0, TL // ISSUE_UNROLL, body, 0)

    def wait_tile(gath, sem):
        pltpu.make_async_copy(ob_hbm.at[pl.ds(0, n), :], gath, sem).wait()

    def step(cur, cur_sem, nxt, nxt_sem):
        wait_tile(cur, cur_sem)
        for tt in range(TL):
            for k in range(TOP_K):
                row_copy(dest_next_ref, nxt, nxt_sem, tt, k).start(priority=k % 2)
        acc = h_ref[...]
        gw = jnp.transpose(jnp.concatenate([gw_ref[...], jnp.zeros((SUBLANES - TOP_K, TL), F32)], axis=0))
        for k in range(TOP_K):
            acc = acc + gw[:, k:k + 1] * _slab_to_rows(cur, k * TL, TL)
        y = _rmsnorm(acc, gfin_ref[...])
        is_prompt = i < n_prompt_tiles

        @pl.when(is_prompt)
        def _():
            yp_ref[...] = y

        @pl.when(jnp.logical_not(is_prompt))
        def _():
            ys_ref[...] = y

        @pl.when(i == n_tiles - 1)
        def _():
            wait_tile(nxt, nxt_sem)

    @pl.when(i == 0)
    def _():
        issue(dest_ref, gath0, sems.at[0])

    parity = lax.rem(i, 2)

    @pl.when(parity == 0)
    def _():
        step(gath0, sems.at[0], gath1, sems.at[1])

    @pl.when(parity == 1)
    def _():
        step(gath1, sems.at[1], gath0, sems.at[0])


def _const_spec(shape):
    nd = len(shape)
    return pl.BlockSpec(shape, lambda *_: (0,) * nd, pipeline_mode=pl.Buffered(1))


def kernel(x_prompt, x_sample, state_conv_a, state_conv_b, g_mix, w_in, b_in, w_conv_a, w_out_a, w_conv_b, b_conv_b, ln_g, ln_b, w_out_b, b_out_b, w_o, g_ffn, w_router, b_router, w_gate_up, b_gate_up, w_down, b_down, g_final):
    depth = g_mix.shape[0]
    assert depth == 1
    bp, lp, d = x_prompt.shape
    bs, ls, _ = x_sample.shape
    assert d == D and w_in.shape[2] == 7 * D
    tp, ts = bp * lp, bs * ls
    t = tp + ts
    assert lp % TL == 0 and ls == SEG and ts % TL == 0 and t % ROUTE_CHUNK == 0 and ROUTE_CHUNK % TL == 0
    assert TAPS_B - 1 <= HIST <= SEG and TAPS_A - 1 <= HIST_A
    npt, nst = tp // TL, ts // TL
    nt = npt + nst
    tiles_per_seq = lp // TL
    n_blocks = (t * TOP_K) // ROWS + N_EXPERTS
    n_blocks_pad = -(-n_blocks // LANES) * LANES
    p_rows = n_blocks * ROWS

    row = lambda a: a.reshape(1, -1)
    f32_spec = lambda: _const_spec((1, D))

    mixer = pl.pallas_call(
        functools.partial(_mixer_kernel, npt, tiles_per_seq, bp, bs),
        grid=(nt,),
        in_specs=[
            pl.BlockSpec((TL, D), lambda i: (jnp.minimum(i, npt - 1), 0)),
            pl.BlockSpec((TL, D), lambda i: (jnp.maximum(i - npt, 0), 0)),
            pl.BlockSpec((NSEG, TAPS_A - 1, D), lambda i: (jnp.maximum(i - npt, 0), 0, 0)),
            pl.BlockSpec((NSEG, TAPS_B - 1, D), lambda i: (jnp.maximum(i - npt, 0), 0, 0)),
            f32_spec(),
            _const_spec((D, 7 * D)),
            _const_spec((1, 7 * D)),
            _const_spec((TAPS_A, D)),
            _const_spec((D, D)),
            _const_spec((TAPS_B, SUBLANES, LANES)),
            f32_spec(), f32_spec(), f32_spec(),
            _const_spec((D, D)),
            f32_spec(),
            _const_spec((D, D)),
            f32_spec(),
            _const_spec((N_EXPERTS, D)),
            _const_spec((N_EXPERTS, 1)),
        ],
        out_specs=[
            pl.BlockSpec((TL, D), lambda i: (i, 0)),
            pl.BlockSpec((TL * SUBLANES, LANES), lambda i: (i, 0)),
            pl.BlockSpec((TOP_K, TL), lambda i: (0, i)),
            pl.BlockSpec((TOP_K, TL), lambda i: (0, i)),
            pl.BlockSpec((bp, TAPS_A - 1, D), lambda i: (0, 0, 0)),
            pl.BlockSpec((bp, TAPS_B - 1, D), lambda i: (0, 0, 0)),
            pl.BlockSpec((bs, TAPS_A - 1, D), lambda i: (0, 0, 0)),
            pl.BlockSpec((bs, TAPS_B - 1, D), lambda i: (0, 0, 0)),
        ],
        out_shape=[
            jax.ShapeDtypeStruct((t, D), F32),
            jax.ShapeDtypeStruct((t * SUBLANES, LANES), F32),
            jax.ShapeDtypeStruct((TOP_K, t), I32),
            jax.ShapeDtypeStruct((TOP_K, t), F32),
            jax.ShapeDtypeStruct((bp, TAPS_A - 1, D), F32),
            jax.ShapeDtypeStruct((bp, TAPS_B - 1, D), F32),
            jax.ShapeDtypeStruct((bs, TAPS_A - 1, D), F32),
            jax.ShapeDtypeStruct((bs, TAPS_B - 1, D), F32),
        ],
        scratch_shapes=[
            pltpu.VMEM((NSEG * (HIST_A + SEG), D), F32),
            pltpu.VMEM((HIST_A, D), F32),
            pltpu.VMEM(((HIST + TL + NSEG * HIST) * SUBLANES, LANES), F32),
            pltpu.VMEM((TL * SUBLANES, LANES), F32),
            pltpu.VMEM((bp + bs + 1, TAPS_A - 1, D), F32),
            pltpu.VMEM((bp + bs + 1, TAPS_B - 1, D), F32),
        ],
        compiler_params=pltpu.CompilerParams(dimension_semantics=("arbitrary",), vmem_limit_bytes=VMEM_LIMIT),
        name="mixer",
    )
    h, hn_slab, idx_t, gw_t, na_p, nb_p, na_s, nb_s = mixer(
        x_prompt.reshape(tp, D), x_sample.reshape(ts, D), state_conv_a[0], state_conv_b[0],
        row(g_mix), w_in[0].astype(BF16), row(b_in), w_conv_a[0],
        w_out_a[0].astype(BF16), w_conv_b[0].reshape(TAPS_B, SUBLANES, LANES), row(b_conv_b), row(ln_g), row(ln_b),
        w_out_b[0].astype(BF16), row(b_out_b), w_o[0].astype(BF16), row(g_ffn),
        w_router[0].T.astype(BF16), b_router[0].reshape(N_EXPERTS, 1))

    dest_sm, block_e, n_used, n_valid, pad_start, pad_len = pl.pallas_call(
        functools.partial(_route_kernel, t, n_blocks_pad),
        out_shape=[
            jax.ShapeDtypeStruct((nt, 1, TOP_K * TL), I32),
            jax.ShapeDtypeStruct((1, n_blocks_pad), I32),
            jax.ShapeDtypeStruct((1, LANES), I32),
            jax.ShapeDtypeStruct((1, n_blocks_pad), I32),
            jax.ShapeDtypeStruct((N_EXPERTS, LANES), I32),
            jax.ShapeDtypeStruct((N_EXPERTS, LANES), I32),
        ],
        scratch_shapes=[pltpu.VMEM((ROUTE_CHUNK, ROUTE_CHUNK), BF16)],
        compiler_params=pltpu.CompilerParams(vmem_limit_bytes=VMEM_LIMIT),
        name="route",
    )(idx_t)

    smem_tile_spec = pl.BlockSpec((1, 1, TOP_K * TL), lambda i: (i, 0, 0), memory_space=pltpu.SMEM)
    any_spec = pl.BlockSpec(memory_space=pl.ANY)
    x_buf = pl.pallas_call(
        functools.partial(_dispatch_kernel, n_blocks),
        grid_spec=pltpu.PrefetchScalarGridSpec(
            num_scalar_prefetch=3,
            grid=(nt,),
            in_specs=[
                pl.BlockSpec((1, 1, TOP_K * TL), lambda i, *_: (i, 0, 0), memory_space=pltpu.SMEM),
                pl.BlockSpec((TL * SUBLANES, LANES), lambda i, *_: (i, 0)),
            ],
            out_specs=any_spec,
            scratch_shapes=[pltpu.VMEM((ROWS * SUBLANES, LANES), F32), pltpu.SemaphoreType.DMA,
                            pltpu.SemaphoreType.DMA],
        ),
        out_shape=jax.ShapeDtypeStruct((p_rows * SUBLANES, LANES), F32),
        compiler_params=pltpu.CompilerParams(dimension_semantics=("arbitrary",)),
        name="dispatch",
    )(pad_start, pad_len, n_used, dest_sm, hn_slab)

    def blk(b, be, nu):
        return jnp.maximum(jnp.minimum(b, nu[0, 0] - 1), 0)

    out_buf = pl.pallas_call(
        _expert_kernel,
        grid_spec=pltpu.PrefetchScalarGridSpec(
            num_scalar_prefetch=3,
            grid=(n_blocks,),
            in_specs=[
                pl.BlockSpec((ROWS * SUBLANES, LANES), lambda b, be, nu, nv: (blk(b, be, nu), 0)),
                any_spec,
                pl.BlockSpec((1, 1, 2 * D), lambda b, be, nu, nv: (be[0, blk(b, be, nu)], 0, 0)),
                any_spec,
                pl.BlockSpec((1, 1, D), lambda b, be, nu, nv: (be[0, blk(b, be, nu)], 0, 0)),
            ],
            out_specs=pl.BlockSpec((ROWS * SUBLANES, LANES), lambda b, be, nu, nv: (b, 0)),
            scratch_shapes=[
                pltpu.VMEM((2, D, 2 * D), F32), pltpu.VMEM((2, D, D), F32),
                pltpu.VMEM((D, 2 * D), BF16), pltpu.VMEM((D, D), BF16),
                pltpu.SMEM((1,), I32), pltpu.SemaphoreType.DMA((2, 2)),
            ],
        ),
        out_shape=jax.ShapeDtypeStruct((p_rows * SUBLANES, LANES), F32),
        compiler_params=pltpu.CompilerParams(dimension_semantics=("arbitrary",), vmem_limit_bytes=VMEM_LIMIT),
        name="experts",
    )(block_e, n_used, n_valid, x_buf, w_gate_up[0], b_gate_up[0].reshape(N_EXPERTS, 1, 2 * D), w_down[0],
      b_down[0].reshape(N_EXPERTS, 1, D))

    y_p, y_s = pl.pallas_call(
        functools.partial(_combine_kernel, nt, npt),
        grid=(nt,),
        in_specs=[
            smem_tile_spec,
            pl.BlockSpec((1, 1, TOP_K * TL), lambda i: (jnp.minimum(i + 1, nt - 1), 0, 0), memory_space=pltpu.SMEM),
            pl.BlockSpec((TL, D), lambda i: (i, 0)),
            pl.BlockSpec((TOP_K, TL), lambda i: (0, i)),
            pl.BlockSpec((1, D), lambda i: (0, 0)),
            any_spec,
        ],
        out_specs=[
            pl.BlockSpec((TL, D), lambda i: (jnp.minimum(i, npt - 1), 0)),
            pl.BlockSpec((TL, D), lambda i: (jnp.maximum(i - npt, 0), 0)),
        ],
        out_shape=[jax.ShapeDtypeStruct((tp, D), F32), jax.ShapeDtypeStruct((ts, D), F32)],
        scratch_shapes=[pltpu.VMEM((TOP_K * TL * SUBLANES, LANES), F32), pltpu.VMEM((TOP_K * TL * SUBLANES, LANES), F32),
                        pltpu.SemaphoreType.DMA((2,))],
        compiler_params=pltpu.CompilerParams(dimension_semantics=("arbitrary",), vmem_limit_bytes=VMEM_LIMIT),
        name="combine",
    )(dest_sm, dest_sm, h, gw_t, row(g_final), out_buf)

    return (y_p.reshape(bp, lp, D), y_s.reshape(bs, ls, D), na_p[None], nb_p[None], na_s[None], nb_s[None])
```

```python
import functools

import jax
import jax.numpy as jnp
from jax import lax
from jax.experimental import pallas as pl
from jax.experimental.pallas import tpu as pltpu

F32 = jnp.float32
BF16 = jnp.bfloat16
I32 = jnp.int32

D = 1024
LANES = 128
SUBLANES = 8
NCHUNK = D // LANES
N_EXPERTS = 32
TOP_K = 4
TAPS_A = 3
TAPS_B = 31
EPS = 1e-5
SWIGLU_LIMIT = 7.0
SWIGLU_ALPHA = 1.702

TL = 256
SEG = 32
NSEG = TL // SEG
HIST = 32
HIST_A = 8
CONV_BLOCK = 4
ROWS = 512
ROW_GRANULE = 128
ISSUE_UNROLL = 8
ROUTE_CHUNK = 512
VMEM_LIMIT = 56 * 1024 * 1024


def _rows_to_slab(dst_ref, slot0, val):
    n = val.shape[0]
    for c in range(NCHUNK):
        dst_ref[pl.ds(SUBLANES * slot0 + c, n, stride=SUBLANES), :] = val[:, c * LANES:(c + 1) * LANES]


def _slab_to_rows(src_ref, slot0, n):
    return jnp.concatenate(
        [src_ref[pl.ds(SUBLANES * slot0 + c, n, stride=SUBLANES), :] for c in range(NCHUNK)], axis=1)


def _rmsnorm(x, g):
    return x * lax.rsqrt(jnp.mean(x * x, axis=-1, keepdims=True) + EPS) * g


def _conv_b_slab(vbuf, hist_rows, w_ref, dst_ref):
    first_tap = HIST - (TAPS_B - 1)
    for s in range(NSEG):
        cur_row0 = (HIST + s * SEG) * SUBLANES
        for t0 in range(0, SEG, CONV_BLOCK):
            ins = []
            for k in range(CONV_BLOCK + TAPS_B - 1):
                u = t0 + first_tap + k
                if u < HIST:
                    row = pl.multiple_of(hist_rows[s] + u * SUBLANES, SUBLANES)
                    ins.append(vbuf[pl.ds(row, SUBLANES), :])
                else:
                    row = cur_row0 + (u - HIST) * SUBLANES
                    ins.append(vbuf[row:row + SUBLANES, :])
            accs = [None] * CONV_BLOCK
            for j in range(TAPS_B):
                wj = w_ref[j]
                for o in range(CONV_BLOCK):
                    term = wj * ins[o + j]
                    accs[o] = term if accs[o] is None else accs[o] + term
            for o in range(CONV_BLOCK):
                row = (s * SEG + t0 + o) * SUBLANES
                dst_ref[row:row + SUBLANES, :] = accs[o]


def _mixer_kernel(n_prompt_tiles, tiles_per_seq, n_prompt_seq, n_sample_seq,
                  xp_ref, xs_ref, sta_ref, stb_ref, gmix_ref, win_ref, bin_ref, wca_ref, woa_ref,
                  wcb_ref, bcb_ref, lng_ref, lnb_ref, wob_ref, bob_ref, wo_ref, gffn_ref, wrt_ref, brt_ref,
                  h_ref, hn_ref, idx_ref, gw_ref, nap_ref, nbp_ref, nas_ref, nbs_ref,
                  cubuf, carry_a, vbuf, ybslab, sa_scr, sb_scr):
    i = pl.program_id(0)
    n_tiles = pl.num_programs(0)
    is_prompt = i < n_prompt_tiles
    seq_start = is_prompt & (lax.rem(i, tiles_per_seq) == 0)
    carry_rows = HIST * SUBLANES

    @pl.when(i == 0)
    def _():
        vbuf[0:carry_rows, :] = jnp.zeros((carry_rows, LANES), F32)
        carry_a[...] = jnp.zeros(carry_a.shape, F32)

    x = jnp.where(is_prompt, xp_ref[...], xs_ref[...])
    n_bf = _rmsnorm(x, gmix_ref[...]).astype(BF16)

    def proj(g):
        cols = slice(g * D, (g + 1) * D)
        return jnp.dot(n_bf, win_ref[:, cols], preferred_element_type=F32) + bin_ref[:, cols]

    cu = proj(1) * proj(2)
    v = proj(3) * jax.nn.sigmoid(proj(4))

    seg_rows = HIST_A + SEG
    prev_a = jnp.where(seq_start, 0.0, carry_a[HIST_A - (TAPS_A - 1):HIST_A, :])
    for s in range(NSEG):
        r0 = s * SEG
        prev = prev_a if s == 0 else cu[r0 - (TAPS_A - 1):r0, :]
        cubuf[s * seg_rows + HIST_A - (TAPS_A - 1):s * seg_rows + HIST_A, :] = jnp.where(is_prompt, prev, sta_ref[s])
        cubuf[s * seg_rows + HIST_A:(s + 1) * seg_rows, :] = cu[r0:r0 + SEG, :]
    carry_a[HIST_A - (TAPS_A - 1):HIST_A, :] = cu[TL - (TAPS_A - 1):TL, :]
    ya_parts = []
    for s in range(NSEG):
        acc = None
        for j in range(TAPS_A):
            lo = s * seg_rows + HIST_A - (TAPS_A - 1) + j
            term = wca_ref[j:j + 1, :] * cubuf[lo:lo + SEG, :]
            acc = term if acc is None else acc + term
        ya_parts.append(acc)
    ya = jnp.concatenate(ya_parts, axis=0)

    vbuf[0:carry_rows, :] = jnp.where(seq_start, 0.0, vbuf[0:carry_rows, :])
    _rows_to_slab(vbuf, HIST, v)
    state_slot0 = HIST + TL
    for s in range(NSEG):
        for c in range(NCHUNK):
            vbuf[pl.ds(SUBLANES * (state_slot0 + s * HIST + HIST - (TAPS_B - 1)) + c, TAPS_B - 1, stride=SUBLANES), :] = (
                stb_ref[s, :, c * LANES:(c + 1) * LANES])
    hist_rows = [jnp.where(is_prompt, s * SEG * SUBLANES, (state_slot0 + s * HIST) * SUBLANES) for s in range(NSEG)]
    _conv_b_slab(vbuf, hist_rows, wcb_ref, ybslab)
    vbuf[0:carry_rows, :] = vbuf[TL * SUBLANES:(TL + HIST) * SUBLANES, :]

    dummy = n_prompt_seq + n_sample_seq
    for s in range(NSEG):
        prompt_slot = i // tiles_per_seq if s == NSEG - 1 else dummy
        slot = jnp.where(is_prompt, prompt_slot, n_prompt_seq + (i - n_prompt_tiles) * NSEG + s)
        r1 = (s + 1) * SEG
        sa_scr[slot] = cu[r1 - (TAPS_A - 1):r1, :]
        sb_scr[slot] = v[r1 - (TAPS_B - 1):r1, :]

    @pl.when(i == n_tiles - 1)
    def _():
        nap_ref[...] = sa_scr[0:n_prompt_seq]
        nbp_ref[...] = sb_scr[0:n_prompt_seq]
        nas_ref[...] = sa_scr[n_prompt_seq:n_prompt_seq + n_sample_seq]
        nbs_ref[...] = sb_scr[n_prompt_seq:n_prompt_seq + n_sample_seq]

    yb = _slab_to_rows(ybslab, 0, TL) + bcb_ref[...]

    out_a = jnp.dot((proj(0) * ya).astype(BF16), woa_ref[...], preferred_element_type=F32)
    mu = jnp.mean(yb, axis=-1, keepdims=True)
    yc = yb - mu
    var = jnp.mean(yc * yc, axis=-1, keepdims=True)
    ln = yc * lax.rsqrt(var + EPS) * lng_ref[...] + lnb_ref[...]
    act = ln * jax.nn.sigmoid(ln)
    out_b = jnp.dot(act.astype(BF16), wob_ref[...], preferred_element_type=F32) + bob_ref[...]
    mix = jax.nn.sigmoid(proj(5)) * out_a + jax.nn.sigmoid(proj(6)) * out_b
    h = x + jnp.dot(mix.astype(BF16), wo_ref[...], preferred_element_type=F32)
    h_ref[...] = h
    inv_rms = lax.rsqrt(jnp.mean(h * h, axis=-1, keepdims=True) + EPS)
    hn = h * inv_rms * gffn_ref[...]
    _rows_to_slab(hn_ref, 0, hn)

    w_scaled = (wrt_ref[...] * gffn_ref[...]).astype(BF16)
    raw = lax.dot_general(w_scaled, h.astype(BF16), (((1,), (1,)), ((), ())), preferred_element_type=F32)
    inv_rms_lanes = jnp.transpose(jnp.broadcast_to(inv_rms, (TL, LANES)))[0:1, :]
    logits = raw * inv_rms_lanes + brt_ref[...]
    iota_e = lax.broadcasted_iota(I32, (N_EXPERTS, TL), 0).astype(F32)
    vals, idxs = [], []
    for _ in range(TOP_K):
        m = jnp.max(logits, axis=0, keepdims=True)
        sel = jnp.min(jnp.where(logits == m, iota_e, float(N_EXPERTS)), axis=0, keepdims=True)
        vals.append(m)
        idxs.append(sel)
        logits = jnp.where(iota_e == sel, -jnp.inf, logits)
    ex = [jnp.exp(val - vals[0]) for val in vals]
    denom = ex[0] + ex[1] + ex[2] + ex[3]
    for k in range(TOP_K):
        idx_ref[k:k + 1, :] = idxs[k].astype(I32)
        gw_ref[k:k + 1, :] = ex[k] / denom


def _route_kernel(n_tokens, n_blocks_pad, idx_ref, dest_ref, be_ref, nu_ref, nv_ref, pad_start_ref, pad_len_ref, tri_scr):
    n_chunks = n_tokens // ROUTE_CHUNK
    iota_e = lax.broadcasted_iota(I32, (N_EXPERTS, ROUTE_CHUNK), 0)

    def masks(c):
        sl = pl.ds(pl.multiple_of(c * ROUTE_CHUNK, ROUTE_CHUNK), ROUTE_CHUNK)
        return sl, [idx_ref[k:k + 1, sl] == iota_e for k in range(TOP_K)]

    def onehot_sum(ms):
        tot = ms[0].astype(F32)
        for k in range(1, TOP_K):
            tot = tot + ms[k].astype(F32)
        return tot

    def count_body(c, acc):
        _, ms = masks(c)
        return acc + onehot_sum(ms)

    acc = lax.fori_loop(0, n_chunks, count_body, jnp.zeros((N_EXPERTS, ROUTE_CHUNK), F32))
    counts = jnp.sum(acc, axis=1, keepdims=True)
    nblk = jnp.floor((counts + (ROWS - 1)) * (1.0 / ROWS))
    r = lax.broadcasted_iota(I32, (N_EXPERTS, N_EXPERTS), 0)
    cidx = lax.broadcasted_iota(I32, (N_EXPERTS, N_EXPERTS), 1)
    lower = (cidx <= r).astype(F32)
    pend = jnp.dot(lower, jnp.broadcast_to(nblk, (N_EXPERTS, LANES)),
                   precision=lax.Precision.HIGHEST, preferred_element_type=F32)
    pend1 = pend[:, 0:1]
    pstart_rows = (pend1 - nblk) * ROWS
    pad_start_ref[...] = jnp.broadcast_to(pstart_rows + counts, (N_EXPERTS, LANES)).astype(I32)
    pad_len_ref[...] = jnp.broadcast_to(nblk * ROWS - counts, (N_EXPERTS, LANES)).astype(I32)

    bvec = lax.broadcasted_iota(I32, (N_EXPERTS, n_blocks_pad), 1).astype(F32)
    be = jnp.sum((pend1 <= bvec).astype(F32), axis=0, keepdims=True)
    be = jnp.minimum(be, N_EXPERTS - 1)
    be_ref[...] = be.astype(I32)
    of_block = lax.broadcasted_iota(I32, (N_EXPERTS, n_blocks_pad), 0).astype(F32) == be
    count_b = jnp.sum(jnp.where(of_block, counts, 0.0), axis=0, keepdims=True)
    first_b = jnp.sum(jnp.where(of_block, pend1 - nblk, 0.0), axis=0, keepdims=True)
    nv_ref[...] = jnp.clip(count_b - ROWS * (bvec[0:1, :] - first_b), 0.0, float(ROWS)).astype(I32)
    last = lax.broadcasted_iota(I32, (N_EXPERTS, LANES), 0) == N_EXPERTS - 1
    nu_ref[...] = jnp.sum(jnp.where(last, pend, 0.0), axis=0, keepdims=True).astype(I32)

    tr = lax.broadcasted_iota(I32, (ROUTE_CHUNK, ROUTE_CHUNK), 0)
    tc = lax.broadcasted_iota(I32, (ROUTE_CHUNK, ROUTE_CHUNK), 1)
    tri_scr[...] = (tr < tc).astype(BF16)

    def dest_body(c, carry):
        sl, ms = masks(c)
        tot = onehot_sum(ms)
        before = jnp.dot(tot.astype(BF16), tri_scr[...], preferred_element_type=F32)
        pos = pstart_rows + carry + before
        for k in range(TOP_K):
            dest_k = jnp.sum(jnp.where(ms[k], pos, 0.0), axis=0, keepdims=True).astype(I32)
            for part in range(ROUTE_CHUNK // TL):
                dest_ref[c * (ROUTE_CHUNK // TL) + part, :, k * TL:(k + 1) * TL] = dest_k[:, part * TL:(part + 1) * TL]
        return carry + jnp.sum(tot, axis=1, keepdims=True)

    lax.fori_loop(0, n_chunks, dest_body, jnp.zeros((N_EXPERTS, 1), F32))


def _dispatch_kernel(n_blocks, pad_start_ref, pad_len_ref, nu_ref, dest_ref, hn_ref, xb_hbm, ztile, sem, pad_sem):
    def row_copy(tt, k):
        src = pl.multiple_of(tt * SUBLANES, SUBLANES)
        dst = pl.multiple_of(dest_ref[0, 0, k * TL + tt] * SUBLANES, SUBLANES)
        return pltpu.make_async_copy(hn_ref.at[pl.ds(src, SUBLANES), :], xb_hbm.at[pl.ds(dst, SUBLANES), :], sem)

    def body(q, carry):
        for u in range(ISSUE_UNROLL):
            for k in range(TOP_K):
                row_copy(q * ISSUE_UNROLL + u, k).start(priority=k % 2)
        return carry

    lax.fori_loop(0, TL // ISSUE_UNROLL, body, 0)

    @pl.when(pl.program_id(0) == 0)
    def _():
        ztile[...] = jnp.zeros(ztile.shape, F32)

        def zero_copy(row, n_slabs):
            dst = pl.multiple_of(row * SUBLANES, SUBLANES)
            return pltpu.make_async_copy(ztile.at[pl.ds(0, n_slabs * SUBLANES), :],
                                         xb_hbm.at[pl.ds(dst, n_slabs * SUBLANES), :], pad_sem)

        def pad_pass(act):
            def per_expert(e, carry):
                row, length = pad_start_ref[e, 0], pad_len_ref[e, 0]
                for bit in reversed(range(ROWS.bit_length() - 1)):
                    take = (length & (1 << bit)) != 0

                    @pl.when(take)
                    def _():
                        act(zero_copy(row, 1 << bit))

                    row = row + jnp.where(take, 1 << bit, 0)
                return carry

            lax.fori_loop(0, N_EXPERTS, per_expert, 0)

        def tail_pass(act):
            def per_block(blk, carry):
                act(zero_copy(blk * ROWS, ROWS))
                return carry

            lax.fori_loop(nu_ref[0, 0], n_blocks, per_block, 0)

        for act in (lambda c: c.start(), lambda c: c.wait()):
            pad_pass(act)
            tail_pass(act)

    n = TL * SUBLANES
    for k in range(TOP_K):
        pltpu.make_async_copy(hn_ref, xb_hbm.at[pl.ds(0, n), :], sem).wait()


def _expert_kernel(be_ref, nu_ref, nv_ref, xs_ref, wgu_hbm, bgu_ref, wd_hbm, bd_ref, o_ref,
                   wgu_f32, wd_f32, wgu_bf, wd_bf, slot_ref, sems):
    b = pl.program_id(0)
    n_used = nu_ref[0, 0]
    e = be_ref[0, b]
    prev = be_ref[0, jnp.maximum(b - 1, 0)]
    active = b < n_used

    def weight_copies(expert, slot):
        return (pltpu.make_async_copy(wgu_hbm.at[expert], wgu_f32.at[slot], sems.at[0, slot]),
                pltpu.make_async_copy(wd_hbm.at[expert], wd_f32.at[slot], sems.at[1, slot]))

    @pl.when(active & (b == 0))
    def _():
        slot_ref[0] = 0
        for c in weight_copies(e, 0):
            c.start()

    @pl.when(active & ((b == 0) | (e != prev)))
    def _():
        slot = slot_ref[0]
        for c in weight_copies(e, slot):
            c.wait()
        last = n_used - 1
        nxt = lax.while_loop(lambda j: (j <= last) & (be_ref[0, jnp.minimum(j, last)] == e), lambda j: j + 1, b + 1)

        @pl.when(nxt <= last)
        def _():
            for c in weight_copies(be_ref[0, jnp.minimum(nxt, last)], 1 - slot):
                c.start(priority=1)

        wgu_bf[...] = wgu_f32[slot].astype(BF16)
        wd_bf[...] = wd_f32[slot].astype(BF16)
        slot_ref[0] = 1 - slot

    def mlp(n_rows):
        xb = _slab_to_rows(xs_ref, 0, n_rows).astype(BF16)
        gu = jnp.dot(xb, wgu_bf[...], preferred_element_type=F32) + bgu_ref[0]
        gate = jnp.minimum(gu[:, :D], SWIGLU_LIMIT)
        up = jnp.clip(gu[:, D:], -SWIGLU_LIMIT, SWIGLU_LIMIT)
        hidden = (up + 1.0) * gate * jax.nn.sigmoid(SWIGLU_ALPHA * gate)
        out = jnp.dot(hidden.astype(BF16), wd_bf[...], preferred_element_type=F32) + bd_ref[0]
        _rows_to_slab(o_ref, 0, out)

    n_valid = nv_ref[0, b]
    for n_rows in range(ROW_GRANULE, ROWS + 1, ROW_GRANULE):
        @pl.when(active & (n_valid > n_rows - ROW_GRANULE) & (n_valid <= n_rows))
        def _(n_rows=n_rows):
            mlp(n_rows)
            if n_rows < ROWS:
                o_ref[n_rows * SUBLANES:ROWS * SUBLANES, :] = jnp.zeros(((ROWS - n_rows) * SUBLANES, LANES), F32)

    @pl.when(jnp.logical_not(active))
    def _():
        o_ref[...] = jnp.zeros(o_ref.shape, F32)


def _combine_kernel(n_tiles, n_prompt_tiles, dest_ref, dest_next_ref, h_ref, gw_ref, gfin_ref, ob_hbm,
                    yp_ref, ys_ref, gath0, gath1, sems):
    i = pl.program_id(0)
    n = TOP_K * TL * SUBLANES

    def row_copy(table_ref, gath, sem, tt, k):
        src = pl.multiple_of(table_ref[0, 0, k * TL + tt] * SUBLANES, SUBLANES)
        dst = pl.multiple_of((k * TL + tt) * SUBLANES, SUBLANES)
        return pltpu.make_async_copy(ob_hbm.at[pl.ds(src, SUBLANES), :], gath.at[pl.ds(dst, SUBLANES), :], sem)

    def issue(table_ref, gath, sem):
        def body(q, carry):
            for u in range(ISSUE_UNROLL):
                for k in range(TOP_K):
                    row_copy(table_ref, gath, sem, q * ISSUE_UNROLL + u, k).start(priority=k % 2)
            return carry

        lax.fori_loop(0, TL // ISSUE_UNROLL, body, 0)

    def wait_tile(gath, sem):
        pltpu.make_async_copy(ob_hbm.at[pl.ds(0, n), :], gath, sem).wait()

    def step(cur, cur_sem, nxt, nxt_sem):
        wait_tile(cur, cur_sem)
        for tt in range(TL):
            for k in range(TOP_K):
                row_copy(dest_next_ref, nxt, nxt_sem, tt, k).start(priority=k % 2)
        acc = h_ref[...]
        gw = jnp.transpose(jnp.concatenate([gw_ref[...], jnp.zeros((SUBLANES - TOP_K, TL), F32)], axis=0))
        for k in range(TOP_K):
            acc = acc + gw[:, k:k + 1] * _slab_to_rows(cur, k * TL, TL)
        y = _rmsnorm(acc, gfin_ref[...])
        is_prompt = i < n_prompt_tiles

        @pl.when(is_prompt)
        def _():
            yp_ref[...] = y

        @pl.when(jnp.logical_not(is_prompt))
        def _():
            ys_ref[...] = y

        @pl.when(i == n_tiles - 1)
        def _():
            wait_tile(nxt, nxt_sem)

    @pl.when(i == 0)
    def _():
        issue(dest_ref, gath0, sems.at[0])

    parity = lax.rem(i, 2)

    @pl.when(parity == 0)
    def _():
        step(gath0, sems.at[0], gath1, sems.at[1])

    @pl.when(parity == 1)
    def _():
        step(gath1, sems.at[1], gath0, sems.at[0])


def _const_spec(shape):
    nd = len(shape)
    return pl.BlockSpec(shape, lambda *_: (0,) * nd, pipeline_mode=pl.Buffered(1))


def kernel(x_prompt, x_sample, state_conv_a, state_conv_b, g_mix, w_in, b_in, w_conv_a, w_out_a, w_conv_b, b_conv_b, ln_g, ln_b, w_out_b, b_out_b, w_o, g_ffn, w_router, b_router, w_gate_up, b_gate_up, w_down, b_down, g_final):
    depth = g_mix.shape[0]
    assert depth == 1
    bp, lp, d = x_prompt.shape
    bs, ls, _ = x_sample.shape
    assert d == D and w_in.shape[2] == 7 * D
    tp, ts = bp * lp, bs * ls
    t = tp + ts
    assert lp % TL == 0 and ls == SEG and ts % TL == 0 and t % ROUTE_CHUNK == 0 and ROUTE_CHUNK % TL == 0
    assert TAPS_B - 1 <= HIST <= SEG and TAPS_A - 1 <= HIST_A
    npt, nst = tp // TL, ts // TL
    nt = npt + nst
    tiles_per_seq = lp // TL
    n_blocks = (t * TOP_K) // ROWS + N_EXPERTS
    n_blocks_pad = -(-n_blocks // LANES) * LANES
    p_rows = n_blocks * ROWS

    row = lambda a: a.reshape(1, -1)
    f32_spec = lambda: _const_spec((1, D))

    mixer = pl.pallas_call(
        functools.partial(_mixer_kernel, npt, tiles_per_seq, bp, bs),
        grid=(nt,),
        in_specs=[
            pl.BlockSpec((TL, D), lambda i: (jnp.minimum(i, npt - 1), 0)),
            pl.BlockSpec((TL, D), lambda i: (jnp.maximum(i - npt, 0), 0)),
            pl.BlockSpec((NSEG, TAPS_A - 1, D), lambda i: (jnp.maximum(i - npt, 0), 0, 0)),
            pl.BlockSpec((NSEG, TAPS_B - 1, D), lambda i: (jnp.maximum(i - npt, 0), 0, 0)),
            f32_spec(),
            _const_spec((D, 7 * D)),
            _const_spec((1, 7 * D)),
            _const_spec((TAPS_A, D)),
            _const_spec((D, D)),
            _const_spec((TAPS_B, SUBLANES, LANES)),
            f32_spec(), f32_spec(), f32_spec(),
            _const_spec((D, D)),
            f32_spec(),
            _const_spec((D, D)),
            f32_spec(),
            _const_spec((N_EXPERTS, D)),
            _const_spec((N_EXPERTS, 1)),
        ],
        out_specs=[
            pl.BlockSpec((TL, D), lambda i: (i, 0)),
            pl.BlockSpec((TL * SUBLANES, LANES), lambda i: (i, 0)),
            pl.BlockSpec((TOP_K, TL), lambda i: (0, i)),
            pl.BlockSpec((TOP_K, TL), lambda i: (0, i)),
            pl.BlockSpec((bp, TAPS_A - 1, D), lambda i: (0, 0, 0)),
            pl.BlockSpec((bp, TAPS_B - 1, D), lambda i: (0, 0, 0)),
            pl.BlockSpec((bs, TAPS_A - 1, D), lambda i: (0, 0, 0)),
            pl.BlockSpec((bs, TAPS_B - 1, D), lambda i: (0, 0, 0)),
        ],
        out_shape=[
            jax.ShapeDtypeStruct((t, D), F32),
            jax.ShapeDtypeStruct((t * SUBLANES, LANES), F32),
            jax.ShapeDtypeStruct((TOP_K, t), I32),
            jax.ShapeDtypeStruct((TOP_K, t), F32),
            jax.ShapeDtypeStruct((bp, TAPS_A - 1, D), F32),
            jax.ShapeDtypeStruct((bp, TAPS_B - 1, D), F32),
            jax.ShapeDtypeStruct((bs, TAPS_A - 1, D), F32),
            jax.ShapeDtypeStruct((bs, TAPS_B - 1, D), F32),
        ],
        scratch_shapes=[
            pltpu.VMEM((NSEG * (HIST_A + SEG), D), F32),
            pltpu.VMEM((HIST_A, D), F32),
            pltpu.VMEM(((HIST + TL + NSEG * HIST) * SUBLANES, LANES), F32),
            pltpu.VMEM((TL * SUBLANES, LANES), F32),
            pltpu.VMEM((bp + bs + 1, TAPS_A - 1, D), F32),
            pltpu.VMEM((bp + bs + 1, TAPS_B - 1, D), F32),
        ],
        compiler_params=pltpu.CompilerParams(dimension_semantics=("arbitrary",), vmem_limit_bytes=VMEM_LIMIT),
        name="mixer",
    )
    h, hn_slab, idx_t, gw_t, na_p, nb_p, na_s, nb_s = mixer(
        x_prompt.reshape(tp, D), x_sample.reshape(ts, D), state_conv_a[0], state_conv_b[0],
        row(g_mix), w_in[0].astype(BF16), row(b_in), w_conv_a[0],
        w_out_a[0].astype(BF16), w_conv_b[0].reshape(TAPS_B, SUBLANES, LANES), row(b_conv_b), row(ln_g), row(ln_b),
        w_out_b[0].astype(BF16), row(b_out_b), w_o[0].astype(BF16), row(g_ffn),
        w_router[0].T, b_router[0].reshape(N_EXPERTS, 1))

    dest_sm, block_e, n_used, n_valid, pad_start, pad_len = pl.pallas_call(
        functools.partial(_route_kernel, t, n_blocks_pad),
        out_shape=[
            jax.ShapeDtypeStruct((nt, 1, TOP_K * TL), I32),
            jax.ShapeDtypeStruct((1, n_blocks_pad), I32),
            jax.ShapeDtypeStruct((1, LANES), I32),
            jax.ShapeDtypeStruct((1, n_blocks_pad), I32),
            jax.ShapeDtypeStruct((N_EXPERTS, LANES), I32),
            jax.ShapeDtypeStruct((N_EXPERTS, LANES), I32),
        ],
        scratch_shapes=[pltpu.VMEM((ROUTE_CHUNK, ROUTE_CHUNK), BF16)],
        compiler_params=pltpu.CompilerParams(vmem_limit_bytes=VMEM_LIMIT),
        name="route",
    )(idx_t)

    smem_tile_spec = pl.BlockSpec((1, 1, TOP_K * TL), lambda i: (i, 0, 0), memory_space=pltpu.SMEM)
    any_spec = pl.BlockSpec(memory_space=pl.ANY)
    x_buf = pl.pallas_call(
        functools.partial(_dispatch_kernel, n_blocks),
        grid_spec=pltpu.PrefetchScalarGridSpec(
            num_scalar_prefetch=3,
            grid=(nt,),
            in_specs=[
                pl.BlockSpec((1, 1, TOP_K * TL), lambda i, *_: (i, 0, 0), memory_space=pltpu.SMEM),
                pl.BlockSpec((TL * SUBLANES, LANES), lambda i, *_: (i, 0)),
            ],
            out_specs=any_spec,
            scratch_shapes=[pltpu.VMEM((ROWS * SUBLANES, LANES), F32), pltpu.SemaphoreType.DMA,
                            pltpu.SemaphoreType.DMA],
        ),
        out_shape=jax.ShapeDtypeStruct((p_rows * SUBLANES, LANES), F32),
        compiler_params=pltpu.CompilerParams(dimension_semantics=("arbitrary",)),
        name="dispatch",
    )(pad_start, pad_len, n_used, dest_sm, hn_slab)

    def blk(b, be, nu):
        return jnp.maximum(jnp.minimum(b, nu[0, 0] - 1), 0)

    out_buf = pl.pallas_call(
        _expert_kernel,
        grid_spec=pltpu.PrefetchScalarGridSpec(
            num_scalar_prefetch=3,
            grid=(n_blocks,),
            in_specs=[
                pl.BlockSpec((ROWS * SUBLANES, LANES), lambda b, be, nu, nv: (blk(b, be, nu), 0)),
                any_spec,
                pl.BlockSpec((1, 1, 2 * D), lambda b, be, nu, nv: (be[0, blk(b, be, nu)], 0, 0)),
                any_spec,
                pl.BlockSpec((1, 1, D), lambda b, be, nu, nv: (be[0, blk(b, be, nu)], 0, 0)),
            ],
            out_specs=pl.BlockSpec((ROWS * SUBLANES, LANES), lambda b, be, nu, nv: (b, 0)),
            scratch_shapes=[
                pltpu.VMEM((2, D, 2 * D), F32), pltpu.VMEM((2, D, D), F32),
                pltpu.VMEM((D, 2 * D), BF16), pltpu.VMEM((D, D), BF16),
                pltpu.SMEM((1,), I32), pltpu.SemaphoreType.DMA((2, 2)),
            ],
        ),
        out_shape=jax.ShapeDtypeStruct((p_rows * SUBLANES, LANES), F32),
        compiler_params=pltpu.CompilerParams(dimension_semantics=("arbitrary",), vmem_limit_bytes=VMEM_LIMIT),
        name="experts",
    )(block_e, n_used, n_valid, x_buf, w_gate_up[0], b_gate_up[0].reshape(N_EXPERTS, 1, 2 * D), w_down[0],
      b_down[0].reshape(N_EXPERTS, 1, D))

    y_p, y_s = pl.pallas_call(
        functools.partial(_combine_kernel, nt, npt),
        grid=(nt,),
        in_specs=[
            smem_tile_spec,
            pl.BlockSpec((1, 1, TOP_K * TL), lambda i: (jnp.minimum(i + 1, nt - 1), 0, 0), memory_space=pltpu.SMEM),
            pl.BlockSpec((TL, D), lambda i: (i, 0)),
            pl.BlockSpec((TOP_K, TL), lambda i: (0, i)),
            pl.BlockSpec((1, D), lambda i: (0, 0)),
            any_spec,
        ],
        out_specs=[
            pl.BlockSpec((TL, D), lambda i: (jnp.minimum(i, npt - 1), 0)),
            pl.BlockSpec((TL, D), lambda i: (jnp.maximum(i - npt, 0), 0)),
        ],
        out_shape=[jax.ShapeDtypeStruct((tp, D), F32), jax.ShapeDtypeStruct((ts, D), F32)],
        scratch_shapes=[pltpu.VMEM((TOP_K * TL * SUBLANES, LANES), F32), pltpu.VMEM((TOP_K * TL * SUBLANES, LANES), F32),
                        pltpu.SemaphoreType.DMA((2,))],
        compiler_params=pltpu.CompilerParams(dimension_semantics=("arbitrary",), vmem_limit_bytes=VMEM_LIMIT),
        name="combine",
    )(dest_sm, dest_sm, h, gw_t, row(g_final), out_buf)

    return (y_p.reshape(bp, lp, D), y_s.reshape(bs, ls, D), na_p[None], nb_p[None], na_s[None], nb_s[None])
```
